```python
import math
import jax, jax.numpy as jnp
from jax import lax
import numpy as np

D_MODEL = 1024
BATCH = 8
SEQ = 4096
DEPTH = 2

DA_HEADS = 4
DA_HEAD_DIM = 64
DA_WIDTH = DA_HEADS * 2 * DA_HEAD_DIM
MLA_HEADS = 4
MLA_Q_RANK = 256
MLA_KV_RANK = 128
MLA_NOPE_DIM = 64
MLA_ROPE_DIM = 32
MLA_V_DIM = 128
ROPE_THETA = 10000.0
AB_IN_WIDTH = 3 * DA_WIDTH + MLA_Q_RANK + MLA_KV_RANK + MLA_ROPE_DIM
AB_MIX_WIDTH = DA_WIDTH + MLA_HEADS * MLA_V_DIM
DIL_HEADS = 16
DIL_HEAD_DIM = 64
DIL_PATTERNS = ((128, 1), (512, 4), (2048, 16))
DIL_Q_BLOCK = 64
Q_BLOCK = 128
MOE_GROUPS = 4
MOE_EXPERTS_PER_GROUP = 8
MOE_EXPERTS = MOE_GROUPS * MOE_EXPERTS_PER_GROUP
MOE_TOP_K = 2
MOE_HIDDEN = 512
MOE_BLOCK = 256
N_EVEN = (DEPTH + 1) // 2
N_ODD = DEPTH // 2
DEEPNORM_ALPHA = (2 * DEPTH) ** 0.25
DEEPNORM_BETA = (8 * DEPTH) ** -0.25
NORM_EPS = 1e-5

kernel_name = 'hybrid_diff_mla_dilated_hmoe_encoder'


def _layer_norm(x, g, b):
    xf = x.astype(jnp.float32)
    mu = jnp.mean(xf, -1, keepdims=True)
    var = jnp.mean(jnp.square(xf - mu), -1, keepdims=True)
    return ((xf - mu) * lax.rsqrt(var + NORM_EPS)).astype(x.dtype) * g + b


def _rms_norm(x, g):
    xf = x.astype(jnp.float32)
    return (xf * lax.rsqrt(jnp.mean(jnp.square(xf), -1, keepdims=True) + NORM_EPS)).astype(x.dtype) * g


def _alibi_slopes(n_heads):
    start = 2.0 ** (-8.0 / n_heads)
    return jnp.asarray([start ** (i + 1) for i in range(n_heads)], dtype=jnp.float32)


def _rope(t):
    s, r = t.shape[1], t.shape[-1]
    half = r // 2
    inv_freq = jnp.power(ROPE_THETA, -jnp.arange(half, dtype=jnp.float32) * 2.0 / r)
    ang = jnp.arange(s, dtype=jnp.float32)[:, None] * inv_freq[None, :]
    cos = jnp.cos(ang)[None, :, None, :]
    sin = jnp.sin(ang)[None, :, None, :]
    tf = t.astype(jnp.float32)
    t1, t2 = tf[..., :half], tf[..., half:]
    return jnp.concatenate([t1 * cos - t2 * sin, t1 * sin + t2 * cos], -1).astype(t.dtype)


def _to_blocks(t, blk):
    return jnp.swapaxes(t.reshape(t.shape[0], t.shape[1] // blk, blk, *t.shape[2:]), 0, 1)


def _from_blocks(t):
    t = jnp.swapaxes(t, 0, 1)
    return t.reshape(t.shape[0], t.shape[1] * t.shape[2], *t.shape[3:])


def _diff_attention(q1, q2, k1, k2, v, lam, slopes):
    s = q1.shape[1]
    scale = q1.shape[-1] ** -0.5
    pos = jnp.arange(s, dtype=jnp.float32)

    def block(args):
        q1b, q2b, qpos = args
        bias = -slopes[:, None, None] * jnp.abs(qpos[:, None] - pos[None, :])[None]
        s1 = jnp.einsum('bqhd,bkhd->bhqk', q1b, k1).astype(jnp.float32) * scale + bias
        s2 = jnp.einsum('bqhd,bkhd->bhqk', q2b, k2).astype(jnp.float32) * scale + bias
        attn = jax.nn.softmax(s1, -1) - lam * jax.nn.softmax(s2, -1)
        return jnp.einsum('bhqk,bkhe->bqhe', attn.astype(v.dtype), v)

    out = lax.map(block, (_to_blocks(q1, Q_BLOCK), _to_blocks(q2, Q_BLOCK), pos.reshape(s // Q_BLOCK, Q_BLOCK)))
    return _from_blocks(out)


def _mla_attention(q_nope, q_rope, k_nope, k_rope, v):
    scale = (q_nope.shape[-1] + q_rope.shape[-1]) ** -0.5

    def block(args):
        qnb, qrb = args
        sc = (jnp.einsum('bqhd,bkhd->bhqk', qnb, k_nope)
              + jnp.einsum('bqhr,bkr->bhqk', qrb, k_rope)).astype(jnp.float32) * scale
        p = jax.nn.softmax(sc, -1)
        return jnp.einsum('bhqk,bkhe->bqhe', p.astype(v.dtype), v)

    out = lax.map(block, (_to_blocks(q_nope, Q_BLOCK), _to_blocks(q_rope, Q_BLOCK)))
    return _from_blocks(out)


def _dilated_attention(q, k, v, slopes):
    s = q.shape[1]
    scale = q.shape[-1] ** -0.5

    def block(args):
        qb, t0 = args
        t = t0 + jnp.arange(DIL_Q_BLOCK, dtype=jnp.int32)
        outs, lses = [], []
        for window, dilation in DIL_PATTERNS:
            n_side = window // (2 * dilation)
            off = dilation * (jnp.arange(2 * n_side + 1, dtype=jnp.int32) - n_side)
            idx = t[:, None] + off[None, :]
            valid = (idx >= 0) & (idx < s)
            idx = jnp.clip(idx, 0, s - 1)
            kg = k[:, idx]
            vg = v[:, idx]
            sc = jnp.einsum('bqhd,bqjhd->bhqj', qb, kg).astype(jnp.float32) * scale
            sc = sc - slopes[:, None, None] * jnp.abs(off).astype(jnp.float32)
            sc = jnp.where(valid, sc, -jnp.inf)
            lse = jax.nn.logsumexp(sc, -1)
            p = jnp.exp(sc - lse[..., None])
            outs.append(jnp.einsum('bhqj,bqjhd->bqhd', p.astype(v.dtype), vg))
            lses.append(lse)
        mix = jax.nn.softmax(jnp.stack(lses), axis=0)
        mix = jnp.swapaxes(mix, 2, 3)[..., None].astype(v.dtype)
        return sum(mix[i] * o for i, o in enumerate(outs))

    starts = jnp.arange(s // DIL_Q_BLOCK, dtype=jnp.int32) * DIL_Q_BLOCK
    out = lax.map(block, (_to_blocks(q, DIL_Q_BLOCK), starts))
    return _from_blocks(out)


def _mixer_ab(x, w_in, lam_q1, lam_k1, lam_q2, lam_k2, diff_g, q_norm_g, w_uq, kv_norm_g, w_ukv, w_out, layer_idx):
    b, s, _ = x.shape
    h = x @ w_in
    cuts = [DA_WIDTH, 2 * DA_WIDTH, 3 * DA_WIDTH, 3 * DA_WIDTH + MLA_Q_RANK, 3 * DA_WIDTH + MLA_Q_RANK + MLA_KV_RANK]
    qa, ka, va, c_q, c_kv, k_r = jnp.split(h, cuts, axis=-1)
    qa = qa.reshape(b, s, DA_HEADS, 2, DA_HEAD_DIM)
    ka = ka.reshape(b, s, DA_HEADS, 2, DA_HEAD_DIM)
    va = va.reshape(b, s, DA_HEADS, 2 * DA_HEAD_DIM)
    lam_init = 0.8 - 0.6 * math.exp(-0.3 * layer_idx)
    lam = (jnp.exp(jnp.sum(lam_q1 * lam_k1).astype(jnp.float32))
           - jnp.exp(jnp.sum(lam_q2 * lam_k2).astype(jnp.float32)) + lam_init)
    oa = _diff_attention(qa[:, :, :, 0], qa[:, :, :, 1], ka[:, :, :, 0], ka[:, :, :, 1], va, lam,
                         _alibi_slopes(DA_HEADS))
    oa = _rms_norm(oa, diff_g) * (1.0 - lam_init)
    q = (_rms_norm(c_q, q_norm_g) @ w_uq).reshape(b, s, MLA_HEADS, MLA_NOPE_DIM + MLA_ROPE_DIM)
    q_nope, q_rope = q[..., :MLA_NOPE_DIM], _rope(q[..., MLA_NOPE_DIM:])
    kv = (_rms_norm(c_kv, kv_norm_g) @ w_ukv).reshape(b, s, MLA_HEADS, MLA_NOPE_DIM + MLA_V_DIM)
    k_nope, vb = kv[..., :MLA_NOPE_DIM], kv[..., MLA_NOPE_DIM:]
    k_rope = _rope(k_r[:, :, None, :])[:, :, 0]
    ob = _mla_attention(q_nope, q_rope, k_nope, k_rope, vb)
    o = jnp.concatenate([oa.reshape(b, s, -1), ob.reshape(b, s, -1)], -1)
    return o @ w_out


def _mixer_c(x, w_in, w_out):
    b, s, _ = x.shape
    h = (x @ w_in).reshape(b, s, 3, DIL_HEADS, DIL_HEAD_DIM)
    o = _dilated_attention(h[:, :, 0], h[:, :, 1], h[:, :, 2], _alibi_slopes(DIL_HEADS))
    return o.reshape(b, s, -1) @ w_out


def _hier_moe(x, w_group, b_group, w_route, b_route, w_gate, w_up, w_down):
    b, s, d = x.shape
    n_tok = b * s
    xf = x.reshape(n_tok, d)
    coarse = (xf @ w_group + b_group).astype(jnp.float32)
    g_sel = jnp.argmax(coarse, -1)
    p_group = jnp.take_along_axis(jax.nn.softmax(coarse, -1), g_sel[:, None], -1)[:, 0]
    fine = (xf @ w_route + b_route).astype(jnp.float32).reshape(n_tok, MOE_GROUPS, MOE_EXPERTS_PER_GROUP)
    fine = jnp.take_along_axis(fine, g_sel[:, None, None], axis=1)[:, 0]
    top_v, top_i = lax.top_k(fine, MOE_TOP_K)
    gate = (p_group[:, None] * jax.nn.softmax(top_v, -1)).astype(x.dtype)
    eid = g_sel[:, None].astype(jnp.int32) * MOE_EXPERTS_PER_GROUP + top_i.astype(jnp.int32)
    n_asg = n_tok * MOE_TOP_K
    a_eid = eid.reshape(-1)
    a_tok = jnp.repeat(jnp.arange(n_tok, dtype=jnp.int32), MOE_TOP_K)
    a_w = gate.reshape(-1)
    order = jnp.argsort(a_eid)
    s_eid, s_tok, s_w = a_eid[order], a_tok[order], a_w[order]
    counts = jnp.bincount(a_eid, length=MOE_EXPERTS).astype(jnp.int32)
    padded = (counts + MOE_BLOCK - 1) // MOE_BLOCK * MOE_BLOCK
    raw_start = jnp.cumsum(counts) - counts
    pad_end = jnp.cumsum(padded)
    pad_start = pad_end - padded
    n_slot = n_asg + MOE_EXPERTS * MOE_BLOCK
    n_blk = n_slot // MOE_BLOCK
    dest = pad_start[s_eid] + jnp.arange(n_asg, dtype=jnp.int32) - raw_start[s_eid]
    slot_tok = jnp.full((n_slot,), n_tok, jnp.int32).at[dest].set(s_tok)
    slot_w = jnp.zeros((n_slot,), x.dtype).at[dest].set(s_w)
    blk_eid = jnp.minimum(jnp.searchsorted(pad_end, jnp.arange(n_blk, dtype=jnp.int32) * MOE_BLOCK, side='right'),
                          MOE_EXPERTS - 1)
    x_pad = jnp.concatenate([xf, jnp.zeros((1, d), xf.dtype)], 0)

    def expert_block(args):
        e, tok = args
        xb = x_pad[tok]
        hb = jax.nn.silu(xb @ w_gate[e]) * (xb @ w_up[e])
        return hb @ w_down[e]

    ys = lax.map(expert_block, (blk_eid, slot_tok.reshape(n_blk, MOE_BLOCK)))
    ys = ys.reshape(n_slot, d) * slot_w[:, None]
    out = jnp.zeros((n_tok + 1, d), x.dtype).at[slot_tok].add(ys)[:n_tok]
    return out.reshape(b, s, d)


def setup_inputs(seed: int = 0) -> dict:
    key = jax.random.key(seed)
    ks = iter(jax.random.split(key, 32))

    def nrm(shape, scale):
        return jax.random.normal(next(ks), shape, jnp.float32) * scale

    def gain(shape):
        return 1.0 + nrm(shape, 0.02)

    d = D_MODEL
    dil_w = DIL_HEADS * DIL_HEAD_DIM
    return {
        'x': nrm((BATCH, SEQ, d), 1.0),
        'w_in_ab': nrm((N_EVEN, d, AB_IN_WIDTH), d ** -0.5),
        'lam_q1': nrm((N_EVEN, DA_HEAD_DIM), 0.1),
        'lam_k1': nrm((N_EVEN, DA_HEAD_DIM), 0.1),
        'lam_q2': nrm((N_EVEN, DA_HEAD_DIM), 0.1),
        'lam_k2': nrm((N_EVEN, DA_HEAD_DIM), 0.1),
        'diff_norm_g': gain((N_EVEN, 2 * DA_HEAD_DIM)),
        'mla_q_norm_g': gain((N_EVEN, MLA_Q_RANK)),
        'w_uq': nrm((N_EVEN, MLA_Q_RANK, MLA_HEADS * (MLA_NOPE_DIM + MLA_ROPE_DIM)), MLA_Q_RANK ** -0.5),
        'mla_kv_norm_g': gain((N_EVEN, MLA_KV_RANK)),
        'w_ukv': nrm((N_EVEN, MLA_KV_RANK, MLA_HEADS * (MLA_NOPE_DIM + MLA_V_DIM)), MLA_KV_RANK ** -0.5),
        'w_out_ab': nrm((N_EVEN, AB_MIX_WIDTH, d), AB_MIX_WIDTH ** -0.5 * DEEPNORM_BETA),
        'w_in_c': nrm((N_ODD, d, 3 * dil_w), d ** -0.5),
        'w_out_c': nrm((N_ODD, dil_w, d), dil_w ** -0.5 * DEEPNORM_BETA),
        'ln_mix_g': gain((DEPTH, d)),
        'ln_mix_b': nrm((DEPTH, d), 0.02),
        'moe_w_group': nrm((DEPTH, d, MOE_GROUPS), d ** -0.5),
        'moe_b_group': nrm((DEPTH, MOE_GROUPS), 0.01),
        'moe_w_route': nrm((DEPTH, d, MOE_EXPERTS), d ** -0.5),
        'moe_b_route': nrm((DEPTH, MOE_EXPERTS), 0.01),
        'moe_w_gate': nrm((DEPTH, MOE_EXPERTS, d, MOE_HIDDEN), d ** -0.5),
        'moe_w_up': nrm((DEPTH, MOE_EXPERTS, d, MOE_HIDDEN), d ** -0.5),
        'moe_w_down': nrm((DEPTH, MOE_EXPERTS, MOE_HIDDEN, d), MOE_HIDDEN ** -0.5 * DEEPNORM_BETA),
        'ln_ffn_g': gain((DEPTH, d)),
        'ln_ffn_b': nrm((DEPTH, d), 0.02),
    }


def reference(x, w_in_ab, lam_q1, lam_k1, lam_q2, lam_k2, diff_norm_g, mla_q_norm_g, w_uq, mla_kv_norm_g, w_ukv,
              w_out_ab, w_in_c, w_out_c, ln_mix_g, ln_mix_b, moe_w_group, moe_b_group, moe_w_route, moe_b_route,
              moe_w_gate, moe_w_up, moe_w_down, ln_ffn_g, ln_ffn_b):
    for layer in range(DEPTH):
        i = layer // 2
        if layer % 2 == 0:
            mixed = _mixer_ab(x, w_in_ab[i], lam_q1[i], lam_k1[i], lam_q2[i], lam_k2[i], diff_norm_g[i],
                              mla_q_norm_g[i], w_uq[i], mla_kv_norm_g[i], w_ukv[i], w_out_ab[i], layer)
        else:
            mixed = _mixer_c(x, w_in_c[i], w_out_c[i])
        x = _layer_norm(DEEPNORM_ALPHA * x + mixed, ln_mix_g[layer], ln_mix_b[layer])
        ffn = _hier_moe(x, moe_w_group[layer], moe_b_group[layer], moe_w_route[layer], moe_b_route[layer],
                        moe_w_gate[layer], moe_w_up[layer], moe_w_down[layer])
        x = _layer_norm(DEEPNORM_ALPHA * x + ffn, ln_ffn_g[layer], ln_ffn_b[layer])
    return x
```

```python
import functools
import math

import jax
import jax.numpy as jnp
from jax import lax
from jax.experimental import pallas as pl
from jax.experimental.pallas import tpu as pltpu

D_MODEL = 1024
DEPTH = 2
DA_HEADS = 4
DA_HEAD_DIM = 64
DA_WIDTH = DA_HEADS * 2 * DA_HEAD_DIM
MLA_HEADS = 4
MLA_Q_RANK = 256
MLA_KV_RANK = 128
MLA_NOPE_DIM = 64
MLA_ROPE_DIM = 32
MLA_V_DIM = 128
ROPE_THETA = 10000.0
DIL_HEADS = 16
DIL_HEAD_DIM = 64
DIL_PATTERNS = ((128, 1), (512, 4), (2048, 16))
MOE_GROUPS = 4
MOE_EXPERTS_PER_GROUP = 8
MOE_EXPERTS = MOE_GROUPS * MOE_EXPERTS_PER_GROUP
MOE_TOP_K = 2
MOE_HIDDEN = 512
MOE_BLOCK = 256
DEEPNORM_ALPHA = (2 * DEPTH) ** 0.25
NORM_EPS = 1e-5

LANES = 128
BF16 = jnp.bfloat16
F32 = jnp.float32
NEG_INF = float("-inf")

TM_PROJ = 512
TM_ROW = 256
TQ_ATTN = 256
TQ_BAND = 256
BAND_HALO = 64
BAND_SUB = 128
VMEM_LIMIT = 56 * 1024 * 1024


def _cparams(*sem):
    return pltpu.CompilerParams(dimension_semantics=sem, vmem_limit_bytes=VMEM_LIMIT)


def _dot(a, b):
    return jnp.dot(a, b, preferred_element_type=F32)


def _dot_nt(a, b):
    return lax.dot_general(a, b, (((1,), (1,)), ((), ())), preferred_element_type=F32)


def _rms(x, g):
    return x * lax.rsqrt(jnp.mean(x * x, axis=-1, keepdims=True) + NORM_EPS) * g


def _layer_norm(y, g, b):
    mu = jnp.mean(y, axis=-1, keepdims=True)
    yc = y - mu
    var = jnp.mean(yc * yc, axis=-1, keepdims=True)
    return yc * lax.rsqrt(var + NORM_EPS) * g + b


def _proj_ab_kernel(x_ref, wa_ref, wc_ref, gq_ref, wuq_ref, gkv_ref, wukv_ref, cq_ref, sq_ref, ck_ref, sk_ref,
                    o_ref):
    xb = x_ref[...].astype(BF16)
    o_ref[:, 0:3 * DA_WIDTH] = _dot(xb, wa_ref[...]).astype(BF16)
    hc = _dot(xb, wc_ref[...])
    c_q = hc[:, 0:MLA_Q_RANK]
    c_kv = hc[:, MLA_Q_RANK:MLA_Q_RANK + MLA_KV_RANK]
    kr = hc[:, 384:512]
    kr_rot = hc[:, 512:640]
    q2 = _dot(_rms(c_q, gq_ref[...]).astype(BF16), wuq_ref[...])
    kv2 = _dot(_rms(c_kv, gkv_ref[...]).astype(BF16), wukv_ref[...])
    cq, sq, ck, sk = cq_ref[...], sq_ref[...], ck_ref[...], sk_ref[...]
    k_rope = kr * ck + kr_rot * sk
    base = 3 * DA_WIDTH
    for h in range(MLA_HEADS):
        lo, hi = h * LANES, (h + 1) * LANES
        qh = q2[:, lo:hi] * cq + q2[:, 512 + lo:512 + hi] * sq
        o_ref[:, base + lo:base + hi] = qh.astype(BF16)
        o_ref[:, base + 512 + lo:base + 512 + hi] = (kv2[:, lo:hi] + k_rope).astype(BF16)
    o_ref[:, base + 1024:base + 1536] = kv2[:, 512:1024].astype(BF16)


def _proj_ab(x2d, wa, wc, gq, wuq, gkv, wukv, cq, sq, ck, sk, seq):
    t = x2d.shape[0]
    tm = TM_PROJ
    nsb = seq // tm
    full = lambda a: pl.BlockSpec(a.shape, lambda i: (0,) * a.ndim)
    tab = pl.BlockSpec((tm, LANES), lambda i: (i % nsb, 0))
    return pl.pallas_call(
        _proj_ab_kernel,
        grid=(t // tm,),
        in_specs=[pl.BlockSpec((tm, D_MODEL), lambda i: (i, 0)), full(wa), full(wc), full(gq), full(wuq),
                  full(gkv), full(wukv), tab, tab, tab, tab],
        out_specs=pl.BlockSpec((tm, 3072), lambda i: (i, 0)),
        out_shape=jax.ShapeDtypeStruct((t, 3072), BF16),
        compiler_params=_cparams("parallel"),
        name="proj_ab",
    )(x2d, wa, wc, gq, wuq, gkv, wukv, cq, sq, ck, sk)


def _diff_attn_kernel(scal_ref, q_ref, k_ref, v_ref, g_ref, o_ref, *, tq, seq, out_scale):
    h = pl.program_id(1)
    i = pl.program_id(2)
    lam = scal_ref[0]
    slope = scal_ref[1 + h]
    q = q_ref[...]
    lane = lax.broadcasted_iota(jnp.int32, q.shape, 1)
    zero = jnp.zeros_like(q)
    q1 = jnp.where(lane < DA_HEAD_DIM, q, zero)
    q2 = jnp.where(lane >= DA_HEAD_DIM, q, zero)
    k = k_ref[...]
    qpos = i * tq + lax.broadcasted_iota(jnp.int32, (tq, seq), 0)
    kpos = lax.broadcasted_iota(jnp.int32, (tq, seq), 1)
    bias = jnp.abs(qpos - kpos).astype(F32) * (-slope)

    def softmax_parts(qm):
        s = _dot_nt(qm, k) + bias
        p = jnp.exp(s - jnp.max(s, axis=-1, keepdims=True))
        return p, jnp.sum(p, axis=-1, keepdims=True)

    p1, l1 = softmax_parts(q1)
    p2, l2 = softmax_parts(q2)
    attn = p1 * (1.0 / l1) - p2 * (lam / l2)
    o = _dot(attn.astype(BF16), v_ref[...])
    o_ref[...] = (_rms(o, g_ref[...]) * out_scale).astype(BF16)


def _mla_attn_kernel(q_ref, k_ref, v_ref, o_ref):
    s = _dot_nt(q_ref[...], k_ref[...])
    p = jnp.exp(s - jnp.max(s, axis=-1, keepdims=True))
    l = jnp.sum(p, axis=-1, keepdims=True)
    o = _dot(p.astype(BF16), v_ref[...])
    o_ref[...] = (o * (1.0 / l)).astype(BF16)


def _attention_ab(hcat, scal, diff_g, batch, seq, out_scale):
    t = hcat.shape[0]
    tq = TQ_ATTN
    nq = seq // tq
    grid = (batch, DA_HEADS, nq)

    def q_spec(col):
        return pl.BlockSpec((tq, LANES), lambda b, h, i, *_: (b * nq + i, col + h))

    def kv_spec(col):
        return pl.BlockSpec((seq, LANES), lambda b, h, i, *_: (b, col + h))

    o_diff = pl.pallas_call(
        functools.partial(_diff_attn_kernel, tq=tq, seq=seq, out_scale=out_scale),
        grid_spec=pltpu.PrefetchScalarGridSpec(
            num_scalar_prefetch=1, grid=grid,
            in_specs=[q_spec(0), kv_spec(4), kv_spec(8), pl.BlockSpec((1, LANES), lambda b, h, i, *_: (0, 0))],
            out_specs=pl.BlockSpec((tq, LANES), lambda b, h, i, *_: (b * nq + i, h))),
        out_shape=jax.ShapeDtypeStruct((t, DA_WIDTH), BF16),
        compiler_params=_cparams("parallel", "parallel", "parallel"),
        name="diff_attn",
    )(scal, hcat, hcat, hcat, diff_g)
    o_mla = pl.pallas_call(
        _mla_attn_kernel,
        grid=grid,
        in_specs=[q_spec(12), kv_spec(16), kv_spec(20)],
        out_specs=pl.BlockSpec((tq, LANES), lambda b, h, i: (b * nq + i, h)),
        out_shape=jax.ShapeDtypeStruct((t, MLA_HEADS * MLA_V_DIM), BF16),
        compiler_params=_cparams("parallel", "parallel", "parallel"),
        name="mla_attn",
    )(hcat, hcat, hcat)
    return o_diff, o_mla


def _proj_c_kernel(x_ref, w_ref, o_ref):
    xb = x_ref[...].astype(BF16)
    for n in range(3):
        o_ref[:, n * 1024:(n + 1) * 1024] = _dot(xb, w_ref[:, n * 1024:(n + 1) * 1024]).astype(BF16)


def _proj_c(x2d, w):
    t = x2d.shape[0]
    tm = TM_PROJ
    return pl.pallas_call(
        _proj_c_kernel,
        grid=(t // tm,),
        in_specs=[pl.BlockSpec((tm, D_MODEL), lambda i: (i, 0)), pl.BlockSpec(w.shape, lambda i: (0, 0))],
        out_specs=pl.BlockSpec((tm, 3072), lambda i: (i, 0)),
        out_shape=jax.ShapeDtypeStruct((t, 3072), BF16),
        compiler_params=_cparams("parallel"),
        name="proj_c",
    )(x2d, w)


def _band_attn_kernel(q_ref, kp_ref, kc_ref, kn_ref, vp_ref, vc_ref, vn_ref, band_ref, o_ref, lse_ref,
                      *, tq, sub_len):
    i = pl.program_id(1)
    n_sub = tq // BAND_SUB
    win = BAND_SUB + 2 * BAND_HALO
    lane = lax.broadcasted_iota(jnp.int32, (BAND_SUB, LANES), 1)
    first_half = lane < DIL_HEAD_DIM
    kcol = lax.broadcasted_iota(jnp.int32, (1, win), 1)
    for pair in range(DIL_HEADS // 2):
        cs = slice(pair * LANES, (pair + 1) * LANES)
        kcat = jnp.concatenate([kp_ref[:, cs], kc_ref[:, cs], kn_ref[:, cs]], axis=0)
        vcat = jnp.concatenate([vp_ref[:, cs], vc_ref[:, cs], vn_ref[:, cs]], axis=0)
        for sb in range(n_sub):
            r0 = sb * BAND_SUB
            qp = q_ref[r0:r0 + BAND_SUB, cs]
            kw = kcat[r0:r0 + win]
            vw = vcat[r0:r0 + win]
            kpos = i * tq + r0 - BAND_HALO + kcol
            kvalid = (kpos >= 0) & (kpos < sub_len)
            outs, lses = [], []
            for half in range(2):
                hd = 2 * pair + half
                qh = jnp.where(first_half if half == 0 else jnp.logical_not(first_half), qp, jnp.zeros_like(qp))
                s = _dot_nt(qh, kw) + band_ref[hd]
                s = jnp.where(kvalid, s, NEG_INF)
                m = jnp.max(s, axis=-1, keepdims=True)
                p = jnp.exp(s - m)
                l = jnp.sum(p, axis=-1, keepdims=True)
                outs.append(_dot(p.astype(BF16), vw) * (1.0 / l))
                lses.append(m + jnp.log(l))
            o_ref[r0:r0 + BAND_SUB, cs] = jnp.where(first_half, outs[0], outs[1]).astype(BF16)
            prev = lse_ref[r0:r0 + BAND_SUB, :] if pair > 0 else jnp.zeros((BAND_SUB, LANES), F32)
            upd = jnp.where(lane == 2 * pair, lses[0], jnp.where(lane == 2 * pair + 1, lses[1], prev))
            lse_ref[r0:r0 + BAND_SUB, :] = upd


def _band_attention(hq, band, n_seq, sub_len):
    t = hq.shape[0]
    tq = TQ_BAND
    nq = sub_len // tq
    hb = tq // BAND_HALO

    def cur(col):
        return pl.BlockSpec((tq, 1024), lambda s, i: (s * nq + i, col))

    def prv(col):
        return pl.BlockSpec((BAND_HALO, 1024), lambda s, i: (jnp.maximum((s * nq + i) * hb - 1, s * nq * hb), col))

    def nxt(col):
        return pl.BlockSpec((BAND_HALO, 1024),
                            lambda s, i: (jnp.minimum((s * nq + i + 1) * hb, (s + 1) * nq * hb - 1), col))

    return pl.pallas_call(
        functools.partial(_band_attn_kernel, tq=tq, sub_len=sub_len),
        grid=(n_seq, nq),
        in_specs=[cur(0), prv(1), cur(1), nxt(1), prv(2), cur(2), nxt(2),
                  pl.BlockSpec(band.shape, lambda s, i: (0, 0, 0))],
        out_specs=[pl.BlockSpec((tq, 1024), lambda s, i: (s * nq + i, 0)),
                   pl.BlockSpec((tq, LANES), lambda s, i: (s * nq + i, 0))],
        out_shape=[jax.ShapeDtypeStruct((t, 1024), BF16), jax.ShapeDtypeStruct((t, LANES), F32)],
        compiler_params=_cparams("parallel", "parallel"),
        name="band_attn",
    )(hq, hq, hq, hq, hq, hq, hq, band)


ROUTE_FINE0 = MOE_GROUPS
ROUTE_FINE1 = MOE_GROUPS + MOE_EXPERTS


def _route_and_rank(x1, wrh_ref, wrm_ref, br_ref, ri_ref, rg_ref, cnt_ref):
    tm = x1.shape[0]
    hi = x1.astype(BF16)
    mid = (x1 - hi.astype(F32)).astype(BF16)
    logits = _dot(hi, wrh_ref[...]) + (_dot(hi, wrm_ref[...]) + _dot(mid, wrh_ref[...])) + br_ref[...]
    lane = lax.broadcasted_iota(jnp.int32, (tm, LANES), 1)
    lane_f = lane.astype(F32)

    def first_lane(mask):
        return jnp.min(jnp.where(mask, lane_f, float(LANES)), axis=-1, keepdims=True).astype(jnp.int32)

    coarse = jnp.where(lane < MOE_GROUPS, logits, NEG_INF)
    cmax = jnp.max(coarse, axis=-1, keepdims=True)
    gsel = first_lane(coarse == cmax)
    p_group = 1.0 / jnp.sum(jnp.exp(coarse - cmax), axis=-1, keepdims=True)
    in_group = ((lane >= ROUTE_FINE0) & (lane < ROUTE_FINE1)
                & (jnp.right_shift(lane - ROUTE_FINE0, 3) == gsel))
    fine = jnp.where(in_group, logits, NEG_INF)
    v1 = jnp.max(fine, axis=-1, keepdims=True)
    i1 = first_lane(fine == v1)
    fine2 = jnp.where(lane == i1, NEG_INF, fine)
    v2 = jnp.max(fine2, axis=-1, keepdims=True)
    i2 = first_lane(fine2 == v2)
    e2 = jnp.exp(v2 - v1)
    den = 1.0 + e2
    g1 = p_group * (1.0 / den)
    g2 = p_group * (e2 / den)
    sel1 = lane == i1
    sel2 = lane == i2
    onehot = jnp.where(sel1 | sel2, 1.0, 0.0)
    row = lax.broadcasted_iota(jnp.int32, (tm, tm), 0)
    col = lax.broadcasted_iota(jnp.int32, (tm, tm), 1)
    ltri = jnp.where(col < row, 1.0, 0.0).astype(BF16)
    before = _dot(ltri, onehot.astype(BF16)) + cnt_ref[...]
    rank1 = jnp.sum(jnp.where(sel1, before, 0.0), axis=-1, keepdims=True).astype(jnp.int32)
    rank2 = jnp.sum(jnp.where(sel2, before, 0.0), axis=-1, keepdims=True).astype(jnp.int32)
    cnt_ref[...] = cnt_ref[...] + jnp.sum(onehot, axis=0, keepdims=True)
    zi = jnp.zeros((tm, LANES), jnp.int32)
    ri_ref[...] = jnp.where(lane == 0, i1 - ROUTE_FINE0,
                            jnp.where(lane == 1, i2 - ROUTE_FINE0,
                                      jnp.where(lane == 2, rank1, jnp.where(lane == 3, rank2, zi))))
    rg_ref[...] = jnp.where(lane == 0, g1, jnp.where(lane == 1, g2, jnp.zeros((tm, LANES), F32)))


def _outproj_ab_kernel(oa_ref, ob_ref, wo_ref, x_ref, g_ref, b_ref, wrh_ref, wrm_ref, br_ref,
                       x1_ref, ri_ref, rg_ref, cnt_ref):
    @pl.when(pl.program_id(0) == 0)
    def _():
        cnt_ref[...] = jnp.zeros_like(cnt_ref)

    mixed = _dot(oa_ref[...], wo_ref[0:512, :]) + _dot(ob_ref[...], wo_ref[512:1024, :])
    x1 = _layer_norm(DEEPNORM_ALPHA * x_ref[...] + mixed, g_ref[...], b_ref[...])
    x1_ref[...] = x1
    _route_and_rank(x1, wrh_ref, wrm_ref, br_ref, ri_ref, rg_ref, cnt_ref)


def _outproj_c_kernel(o1_ref, o2_ref, o3_ref, l1_ref, l2_ref, l3_ref, wo_ref, x_ref, g_ref, b_ref,
                      wrh_ref, wrm_ref, br_ref, x1_ref, ri_ref, rg_ref, cnt_ref):
    @pl.when(pl.program_id(0) == 0)
    def _():
        cnt_ref[...] = jnp.zeros_like(cnt_ref)

    tm = x_ref.shape[0]
    l1, l2, l3 = l1_ref[...], l2_ref[...], l3_ref[...]
    lmax = jnp.maximum(jnp.maximum(l1, l2), l3)
    e1, e2, e3 = jnp.exp(l1 - lmax), jnp.exp(l2 - lmax), jnp.exp(l3 - lmax)
    inv = 1.0 / (e1 + e2 + e3)
    w1, w2, w3 = e1 * inv, e2 * inv, e3 * inv
    first_half = lax.broadcasted_iota(jnp.int32, (tm, LANES), 1) < DIL_HEAD_DIM
    mixed = jnp.zeros((tm, D_MODEL), F32)
    for pair in range(DIL_HEADS // 2):
        cs = slice(pair * LANES, (pair + 1) * LANES)
        ha, hb = 2 * pair, 2 * pair + 1
        om = None
        for w, o_ref in ((w1, o1_ref), (w2, o2_ref), (w3, o3_ref)):
            wf = jnp.where(first_half, w[:, ha:ha + 1], w[:, hb:hb + 1])
            term = wf * o_ref[:, cs].astype(F32)
            om = term if om is None else om + term
        mixed = mixed + _dot(om.astype(BF16), wo_ref[cs, :])
    x1 = _layer_norm(DEEPNORM_ALPHA * x_ref[...] + mixed, g_ref[...], b_ref[...])
    x1_ref[...] = x1
    _route_and_rank(x1, wrh_ref, wrm_ref, br_ref, ri_ref, rg_ref, cnt_ref)


def _outproj_ln_route(kernel_fn, mix_inputs, mix_specs, wo, x2d, g, b, wrh, wrm, br):
    t = x2d.shape[0]
    tm = TM_ROW
    full = lambda a: pl.BlockSpec(a.shape, lambda i: (0,) * a.ndim)
    row = lambda w: pl.BlockSpec((tm, w), lambda i: (i, 0))
    return pl.pallas_call(
        kernel_fn,
        grid=(t // tm,),
        in_specs=list(mix_specs) + [full(wo), row(D_MODEL), full(g), full(b), full(wrh), full(wrm), full(br)],
        out_specs=[row(D_MODEL), row(LANES), row(LANES), pl.BlockSpec((1, LANES), lambda i: (0, 0))],
        out_shape=[jax.ShapeDtypeStruct((t, D_MODEL), F32), jax.ShapeDtypeStruct((t, LANES), jnp.int32),
                   jax.ShapeDtypeStruct((t, LANES), F32), jax.ShapeDtypeStruct((1, LANES), F32)],
        compiler_params=_cparams("arbitrary"),
        name=kernel_fn.__name__.strip("_"),
    )(*mix_inputs, wo, x2d, g, b, wrh, wrm, br)


def _dispatch_kernel(dest_ref, x_ref, xs_in_ref, xs_ref, sem):
    del xs_in_ref
    tm = x_ref.shape[0]

    def row_copy(r, k):
        return pltpu.make_async_copy(x_ref.at[pl.ds(r, 1)], xs_ref.at[pl.ds(dest_ref[0, 0, 2 * r + k], 1)], sem)

    def issue(r, c):
        row_copy(r, 0).start()
        row_copy(r, 1).start()
        return c

    def drain(r, c):
        row_copy(r, 0).wait()
        row_copy(r, 1).wait()
        return c

    lax.fori_loop(0, tm, issue, 0)
    lax.fori_loop(0, tm, drain, 0)


def _dispatch(dest3, x1, n_slot):
    t = x1.shape[0]
    tm = TM_ROW
    xs0 = jnp.zeros((n_slot, D_MODEL), F32)
    return pl.pallas_call(
        _dispatch_kernel,
        grid=(t // tm,),
        in_specs=[pl.BlockSpec((1, 1, 2 * tm), lambda i: (i, 0, 0), memory_space=pltpu.SMEM),
                  pl.BlockSpec((tm, D_MODEL), lambda i: (i, 0)),
                  pl.BlockSpec(memory_space=pl.ANY)],
        out_specs=pl.BlockSpec(memory_space=pl.ANY),
        out_shape=jax.ShapeDtypeStruct((n_slot, D_MODEL), F32),
        scratch_shapes=[pltpu.SemaphoreType.DMA(())],
        input_output_aliases={2: 0},
        compiler_params=_cparams("arbitrary"),
        name="moe_dispatch",
    )(dest3, x1, xs0)


def _expert_kernel(eid_ref, xs_ref, wgu_ref, wd_ref, ys_ref):
    del eid_ref
    xb = xs_ref[...].astype(BF16)
    gu = _dot(xb, wgu_ref[0])
    gate, up = gu[:, 0:MOE_HIDDEN], gu[:, MOE_HIDDEN:2 * MOE_HIDDEN]
    hidden = gate * (1.0 / (1.0 + jnp.exp(-gate))) * up
    ys_ref[...] = _dot(hidden.astype(BF16), wd_ref[0])


def _expert_mlp(blk_eid, xs, wgu, wd):
    n_slot = xs.shape[0]
    n_blk = n_slot // MOE_BLOCK
    return pl.pallas_call(
        _expert_kernel,
        grid_spec=pltpu.PrefetchScalarGridSpec(
            num_scalar_prefetch=1, grid=(n_blk,),
            in_specs=[pl.BlockSpec((MOE_BLOCK, D_MODEL), lambda i, e: (i, 0)),
                      pl.BlockSpec((1, D_MODEL, 2 * MOE_HIDDEN), lambda i, e: (e[i], 0, 0)),
                      pl.BlockSpec((1, MOE_HIDDEN, D_MODEL), lambda i, e: (e[i], 0, 0))],
            out_specs=pl.BlockSpec((MOE_BLOCK, D_MODEL), lambda i, e: (i, 0))),
        out_shape=jax.ShapeDtypeStruct((n_slot, D_MODEL), F32),
        compiler_params=_cparams("parallel"),
        name="expert_mlp",
    )(blk_eid, xs, wgu, wd)


def _combine_kernel(dest_ref, rg_ref, x_ref, g_ref, b_ref, ys_ref, o_ref, buf, sem):
    tm = x_ref.shape[0]

    def row_copy(r, k):
        return pltpu.make_async_copy(ys_ref.at[pl.ds(dest_ref[0, 0, 2 * r + k], 1)], buf.at[k, pl.ds(r, 1)], sem)

    def issue(r, c):
        row_copy(r, 0).start()
        row_copy(r, 1).start()
        return c

    def drain(r, c):
        row_copy(r, 0).wait()
        row_copy(r, 1).wait()
        return c

    lax.fori_loop(0, tm, issue, 0)
    lax.fori_loop(0, tm, drain, 0)
    rg = rg_ref[...]
    ffn = buf[0] * rg[:, 0:1] + buf[1] * rg[:, 1:2]
    o_ref[...] = _layer_norm(DEEPNORM_ALPHA * x_ref[...] + ffn, g_ref[...], b_ref[...])


def _combine_ln(dest3, rg, x1, g, b, ys):
    t = x1.shape[0]
    tm = TM_ROW
    row = lambda w: pl.BlockSpec((tm, w), lambda i: (i, 0))
    full = lambda a: pl.BlockSpec(a.shape, lambda i: (0,) * a.ndim)
    return pl.pallas_call(
        _combine_kernel,
        grid=(t // tm,),
        in_specs=[pl.BlockSpec((1, 1, 2 * tm), lambda i: (i, 0, 0), memory_space=pltpu.SMEM),
                  row(LANES), row(D_MODEL), full(g), full(b), pl.BlockSpec(memory_space=pl.ANY)],
        out_specs=row(D_MODEL),
        out_shape=jax.ShapeDtypeStruct((t, D_MODEL), F32),
        scratch_shapes=[pltpu.VMEM((2, tm, D_MODEL), F32), pltpu.SemaphoreType.DMA(())],
        compiler_params=_cparams("arbitrary"),
        name="moe_combine",
    )(dest3, rg, x1, g, b, ys)


def _moe(x1, ri, rg, cnt, wgu, wd, g, b):
    t = x1.shape[0]
    n_slot = t * MOE_TOP_K + MOE_EXPERTS * MOE_BLOCK
    n_blk = n_slot // MOE_BLOCK
    counts = cnt[0, ROUTE_FINE0:ROUTE_FINE1].astype(jnp.int32)
    padded = (counts + MOE_BLOCK - 1) // MOE_BLOCK * MOE_BLOCK
    pad_end = jnp.cumsum(padded)
    pad_start = pad_end - padded
    dest = pad_start[ri[:, 0:2]] + ri[:, 2:4]
    dest3 = dest.reshape(t // TM_ROW, 1, 2 * TM_ROW)
    blk_eid = jnp.minimum(jnp.searchsorted(pad_end, jnp.arange(n_blk, dtype=jnp.int32) * MOE_BLOCK, side='right'),
                          MOE_EXPERTS - 1).astype(jnp.int32)
    xs = _dispatch(dest3, x1, n_slot)
    ys = _expert_mlp(blk_eid, xs, wgu, wd)
    return _combine_ln(dest3, rg, x1, g, b, ys)


def _rotate_half_cols(w):
    half = MLA_ROPE_DIM // 2
    return jnp.concatenate([-w[..., half:], w[..., :half]], axis=-1)


def _prep_ab(w_in, w_uq, w_ukv):
    wa = jnp.concatenate([w_in[:, 0:DA_WIDTH] * (DA_HEAD_DIM ** -0.5), w_in[:, DA_WIDTH:3 * DA_WIDTH]], axis=1)
    c0 = 3 * DA_WIDTH
    w_cq = w_in[:, c0:c0 + MLA_Q_RANK]
    w_ckv = w_in[:, c0 + MLA_Q_RANK:c0 + MLA_Q_RANK + MLA_KV_RANK]
    w_kr = w_in[:, c0 + MLA_Q_RANK + MLA_KV_RANK:]
    z64 = jnp.zeros((D_MODEL, 64), F32)
    z32 = jnp.zeros((D_MODEL, 32), F32)
    wc = jnp.concatenate([w_cq, w_ckv, z64, w_kr, z32, z64, _rotate_half_cols(w_kr), z32], axis=1)
    qd = MLA_NOPE_DIM + MLA_ROPE_DIM
    wq = w_uq.reshape(MLA_Q_RANK, MLA_HEADS, qd)
    zq = jnp.zeros((MLA_Q_RANK, MLA_HEADS, LANES - qd), F32)
    plain = jnp.concatenate([wq, zq], axis=-1)
    rot = jnp.concatenate([jnp.zeros_like(wq[..., :MLA_NOPE_DIM]), _rotate_half_cols(wq[..., MLA_NOPE_DIM:]), zq],
                          axis=-1)
    wuq = jnp.concatenate([plain.reshape(MLA_Q_RANK, -1), rot.reshape(MLA_Q_RANK, -1)], axis=1)
    wkv = w_ukv.reshape(MLA_KV_RANK, MLA_HEADS, MLA_NOPE_DIM + MLA_V_DIM)
    k_pad = jnp.concatenate([wkv[..., :MLA_NOPE_DIM], jnp.zeros((MLA_KV_RANK, MLA_HEADS, 64), F32)], axis=-1)
    wukv = jnp.concatenate([k_pad.reshape(MLA_KV_RANK, -1), wkv[..., MLA_NOPE_DIM:].reshape(MLA_KV_RANK, -1)],
                           axis=1)
    return wa.astype(BF16), wc.astype(BF16), wuq.astype(BF16), wukv.astype(BF16)


def _rope_tables(seq):
    half = MLA_ROPE_DIM // 2
    inv_freq = jnp.power(ROPE_THETA, -jnp.arange(half, dtype=F32) * 2.0 / MLA_ROPE_DIM)
    ang = jnp.arange(seq, dtype=F32)[:, None] * inv_freq[None, :]
    cos = jnp.concatenate([jnp.cos(ang)] * 2, axis=-1)
    sin = jnp.concatenate([jnp.sin(ang)] * 2, axis=-1)
    scale = (MLA_NOPE_DIM + MLA_ROPE_DIM) ** -0.5
    ones = jnp.ones((seq, 64), F32)
    z64 = jnp.zeros((seq, 64), F32)
    z32 = jnp.zeros((seq, 32), F32)
    cq = jnp.concatenate([ones, cos, z32], axis=-1) * scale
    sq = jnp.concatenate([z64, sin, z32], axis=-1) * scale
    ck = jnp.concatenate([z64, cos, z32], axis=-1)
    sk = jnp.concatenate([z64, sin, z32], axis=-1)
    return cq, sq, ck, sk


def _alibi_slopes(n_heads):
    start = 2.0 ** (-8.0 / n_heads)
    return jnp.asarray([start ** (i + 1) for i in range(n_heads)], dtype=F32)


def _band_table(dilation):
    win = BAND_SUB + 2 * BAND_HALO
    rel = jnp.arange(win, dtype=jnp.int32)[None, :] - BAND_HALO - jnp.arange(BAND_SUB, dtype=jnp.int32)[:, None]
    dist = (jnp.abs(rel) * dilation).astype(F32)
    bias = -_alibi_slopes(DIL_HEADS)[:, None, None] * dist[None]
    return jnp.where((jnp.abs(rel) <= BAND_HALO)[None], bias, NEG_INF)


def _prep_router(w_group, b_group, w_route, b_route):
    pad = LANES - MOE_GROUPS - MOE_EXPERTS
    w = jnp.concatenate([w_group, w_route, jnp.zeros((D_MODEL, pad), F32)], axis=1)
    hi = w.astype(BF16)
    mid = (w - hi.astype(F32)).astype(BF16)
    bias = jnp.concatenate([b_group, b_route, jnp.zeros((pad,), F32)])[None, :]
    return hi, mid, bias


def _prep_experts(w_gate, w_up, w_down):
    return jnp.concatenate([w_gate, w_up], axis=-1).astype(BF16), w_down.astype(BF16)


def kernel(x, w_in_ab, lam_q1, lam_k1, lam_q2, lam_k2, diff_norm_g, mla_q_norm_g, w_uq, mla_kv_norm_g, w_ukv,
           w_out_ab, w_in_c, w_out_c, ln_mix_g, ln_mix_b, moe_w_group, moe_b_group, moe_w_route, moe_b_route,
           moe_w_gate, moe_w_up, moe_w_down, ln_ffn_g, ln_ffn_b):
    batch, seq, d = x.shape
    t = batch * seq
    xs = x.reshape(t, d)

    wa, wc, wuq, wukv = _prep_ab(w_in_ab[0], w_uq[0], w_ukv[0])
    cq, sq, ck, sk = _rope_tables(seq)
    hcat = _proj_ab(xs, wa, wc, mla_q_norm_g[0][None, :], wuq, mla_kv_norm_g[0][None, :], wukv, cq, sq, ck, sk, seq)
    lam_init = 0.8 - 0.6 * math.exp(-0.3 * 0)
    lam = (jnp.exp(jnp.sum(lam_q1[0] * lam_k1[0])) - jnp.exp(jnp.sum(lam_q2[0] * lam_k2[0])) + lam_init)
    scal = jnp.concatenate([lam[None].astype(F32), _alibi_slopes(DA_HEADS), jnp.zeros((3,), F32)])
    o_diff, o_mla = _attention_ab(hcat, scal, diff_norm_g[0][None, :], batch, seq, 1.0 - lam_init)
    wrh, wrm, br = _prep_router(moe_w_group[0], moe_b_group[0], moe_w_route[0], moe_b_route[0])
    row = lambda w: pl.BlockSpec((TM_ROW, w), lambda i: (i, 0))
    x1, ri, rg, cnt = _outproj_ln_route(_outproj_ab_kernel, (o_diff, o_mla), (row(512), row(512)),
                                        w_out_ab[0].astype(BF16), xs, ln_mix_g[0][None, :], ln_mix_b[0][None, :],
                                        wrh, wrm, br)
    wgu, wd = _prep_experts(moe_w_gate[0], moe_w_up[0], moe_w_down[0])
    xs = _moe(x1, ri, rg, cnt, wgu, wd, ln_ffn_g[0][None, :], ln_ffn_b[0][None, :])

    dil_w = DIL_HEADS * DIL_HEAD_DIM
    w_c = jnp.concatenate([w_in_c[0][:, :dil_w] * (DIL_HEAD_DIM ** -0.5), w_in_c[0][:, dil_w:]], axis=1).astype(BF16)
    hq = _proj_c(xs, w_c)
    outs, lses = [], []
    for window, dil in DIL_PATTERNS:
        assert window // (2 * dil) == BAND_HALO
        sub = seq // dil
        hd = hq.reshape(batch, sub, dil, 3072).transpose(0, 2, 1, 3).reshape(t, 3072) if dil > 1 else hq
        o_p, lse_p = _band_attention(hd, _band_table(dil), batch * dil, sub)
        if dil > 1:
            o_p = o_p.reshape(batch, dil, sub, dil_w).transpose(0, 2, 1, 3).reshape(t, dil_w)
            lse_p = lse_p.reshape(batch, dil, sub, LANES).transpose(0, 2, 1, 3).reshape(t, LANES)
        outs.append(o_p)
        lses.append(lse_p)
    wrh, wrm, br = _prep_router(moe_w_group[1], moe_b_group[1], moe_w_route[1], moe_b_route[1])
    x1, ri, rg, cnt = _outproj_ln_route(_outproj_c_kernel, (*outs, *lses),
                                        (row(dil_w),) * 3 + (row(LANES),) * 3,
                                        w_out_c[0].astype(BF16), xs, ln_mix_g[1][None, :], ln_mix_b[1][None, :],
                                        wrh, wrm, br)
    wgu, wd = _prep_experts(moe_w_gate[1], moe_w_up[1], moe_w_down[1])
    xs = _moe(x1, ri, rg, cnt, wgu, wd, ln_ffn_g[1][None, :], ln_ffn_b[1][None, :])
    return xs.reshape(batch, seq, d)
```

```python
import functools
import math

import jax
import jax.numpy as jnp
from jax import lax
from jax.experimental import pallas as pl
from jax.experimental.pallas import tpu as pltpu

D_MODEL = 1024
DEPTH = 2
DA_HEADS = 4
DA_HEAD_DIM = 64
DA_WIDTH = DA_HEADS * 2 * DA_HEAD_DIM
MLA_HEADS = 4
MLA_Q_RANK = 256
MLA_KV_RANK = 128
MLA_NOPE_DIM = 64
MLA_ROPE_DIM = 32
MLA_V_DIM = 128
ROPE_THETA = 10000.0
DIL_HEADS = 16
DIL_HEAD_DIM = 64
DIL_PATTERNS = ((128, 1), (512, 4), (2048, 16))
MOE_GROUPS = 4
MOE_EXPERTS_PER_GROUP = 8
MOE_EXPERTS = MOE_GROUPS * MOE_EXPERTS_PER_GROUP
MOE_TOP_K = 2
MOE_HIDDEN = 512
MOE_BLOCK = 256
DEEPNORM_ALPHA = (2 * DEPTH) ** 0.25
NORM_EPS = 1e-5

LANES = 128
BF16 = jnp.bfloat16
F32 = jnp.float32
NEG_INF = float("-inf")

TM_PROJ = 512
TM_ROW = 256
TQ_ATTN = 256
TQ_BAND = 256
BAND_HALO = 64
BAND_SUB = 128
VMEM_LIMIT = 56 * 1024 * 1024


def _cparams(*sem):
    return pltpu.CompilerParams(dimension_semantics=sem, vmem_limit_bytes=VMEM_LIMIT)


def _dot(a, b):
    return jnp.dot(a, b, preferred_element_type=F32)


def _dot_nt(a, b):
    return lax.dot_general(a, b, (((1,), (1,)), ((), ())), preferred_element_type=F32)


def _rms(x, g):
    return x * lax.rsqrt(jnp.mean(x * x, axis=-1, keepdims=True) + NORM_EPS) * g


def _layer_norm(y, g, b):
    mu = jnp.mean(y, axis=-1, keepdims=True)
    yc = y - mu
    var = jnp.mean(yc * yc, axis=-1, keepdims=True)
    return yc * lax.rsqrt(var + NORM_EPS) * g + b


def _proj_ab_kernel(x_ref, wa_ref, wc_ref, gq_ref, wuq_ref, gkv_ref, wukv_ref, cq_ref, sq_ref, ck_ref, sk_ref,
                    o_ref):
    xb = x_ref[...].astype(BF16)
    ha = _dot(xb, wa_ref[...])
    o_ref[:, 0:DA_WIDTH] = (ha[:, 0:DA_WIDTH] * (DA_HEAD_DIM ** -0.5)).astype(BF16)
    o_ref[:, DA_WIDTH:3 * DA_WIDTH] = ha[:, DA_WIDTH:3 * DA_WIDTH].astype(BF16)
    hc = _dot(xb, wc_ref[...])
    c_q = hc[:, 0:MLA_Q_RANK]
    c_kv = hc[:, MLA_Q_RANK:MLA_Q_RANK + MLA_KV_RANK]
    kr = hc[:, 384:512]
    kr_rot = hc[:, 512:640]
    q2 = _dot(_rms(c_q, gq_ref[...]).astype(BF16), wuq_ref[...])
    kv2 = _dot(_rms(c_kv, gkv_ref[...]).astype(BF16), wukv_ref[...])
    cq, sq, ck, sk = cq_ref[...], sq_ref[...], ck_ref[...], sk_ref[...]
    k_rope = kr * ck + kr_rot * sk
    base = 3 * DA_WIDTH
    for h in range(MLA_HEADS):
        lo, hi = h * LANES, (h + 1) * LANES
        qh = q2[:, lo:hi] * cq + q2[:, 512 + lo:512 + hi] * sq
        o_ref[:, base + lo:base + hi] = qh.astype(BF16)
        o_ref[:, base + 512 + lo:base + 512 + hi] = (kv2[:, lo:hi] + k_rope).astype(BF16)
    o_ref[:, base + 1024:base + 1536] = kv2[:, 512:1024].astype(BF16)


def _proj_ab(x2d, wa, wc, gq, wuq, gkv, wukv, cq, sq, ck, sk, seq):
    t = x2d.shape[0]
    tm = TM_PROJ
    nsb = seq // tm
    full = lambda a: pl.BlockSpec(a.shape, lambda i: (0,) * a.ndim)
    tab = pl.BlockSpec((tm, LANES), lambda i: (i % nsb, 0))
    return pl.pallas_call(
        _proj_ab_kernel,
        grid=(t // tm,),
        in_specs=[pl.BlockSpec((tm, D_MODEL), lambda i: (i, 0)), full(wa), full(wc), full(gq), full(wuq),
                  full(gkv), full(wukv), tab, tab, tab, tab],
        out_specs=pl.BlockSpec((tm, 3072), lambda i: (i, 0)),
        out_shape=jax.ShapeDtypeStruct((t, 3072), BF16),
        compiler_params=_cparams("parallel"),
        name="proj_ab",
    )(x2d, wa, wc, gq, wuq, gkv, wukv, cq, sq, ck, sk)


def _diff_attn_kernel(scal_ref, q_ref, k_ref, v_ref, g_ref, o_ref, *, tq, seq, out_scale):
    h = pl.program_id(1)
    i = pl.program_id(2)
    lam = scal_ref[0]
    slope = scal_ref[1 + h]
    q = q_ref[...]
    lane = lax.broadcasted_iota(jnp.int32, q.shape, 1)
    zero = jnp.zeros_like(q)
    q1 = jnp.where(lane < DA_HEAD_DIM, q, zero)
    q2 = jnp.where(lane >= DA_HEAD_DIM, q, zero)
    k = k_ref[...]
    qpos = i * tq + lax.broadcasted_iota(jnp.int32, (tq, seq), 0)
    kpos = lax.broadcasted_iota(jnp.int32, (tq, seq), 1)
    bias = jnp.abs(qpos - kpos).astype(F32) * (-slope)

    def softmax_parts(qm):
        s = _dot_nt(qm, k) + bias
        p = jnp.exp(s - jnp.max(s, axis=-1, keepdims=True))
        return p, jnp.sum(p, axis=-1, keepdims=True)

    p1, l1 = softmax_parts(q1)
    p2, l2 = softmax_parts(q2)
    attn = p1 * (1.0 / l1) - p2 * (lam / l2)
    o = _dot(attn.astype(BF16), v_ref[...])
    o_ref[...] = (_rms(o, g_ref[...]) * out_scale).astype(BF16)


def _mla_attn_kernel(q_ref, k_ref, v_ref, o_ref):
    s = _dot_nt(q_ref[...], k_ref[...])
    p = jnp.exp(s - jnp.max(s, axis=-1, keepdims=True))
    l = jnp.sum(p, axis=-1, keepdims=True)
    o = _dot(p.astype(BF16), v_ref[...])
    o_ref[...] = (o * (1.0 / l)).astype(BF16)


def _attention_ab(hcat, scal, diff_g, batch, seq, out_scale):
    t = hcat.shape[0]
    tq = TQ_ATTN
    nq = seq // tq
    grid = (batch, DA_HEADS, nq)

    def q_spec(col):
        return pl.BlockSpec((tq, LANES), lambda b, h, i, *_: (b * nq + i, col + h))

    def kv_spec(col):
        return pl.BlockSpec((seq, LANES), lambda b, h, i, *_: (b, col + h))

    o_diff = pl.pallas_call(
        functools.partial(_diff_attn_kernel, tq=tq, seq=seq, out_scale=out_scale),
        grid_spec=pltpu.PrefetchScalarGridSpec(
            num_scalar_prefetch=1, grid=grid,
            in_specs=[q_spec(0), kv_spec(4), kv_spec(8), pl.BlockSpec((1, LANES), lambda b, h, i, *_: (0, 0))],
            out_specs=pl.BlockSpec((tq, LANES), lambda b, h, i, *_: (b * nq + i, h))),
        out_shape=jax.ShapeDtypeStruct((t, DA_WIDTH), BF16),
        compiler_params=_cparams("parallel", "parallel", "parallel"),
        name="diff_attn",
    )(scal, hcat, hcat, hcat, diff_g)
    o_mla = pl.pallas_call(
        _mla_attn_kernel,
        grid=grid,
        in_specs=[q_spec(12), kv_spec(16), kv_spec(20)],
        out_specs=pl.BlockSpec((tq, LANES), lambda b, h, i: (b * nq + i, h)),
        out_shape=jax.ShapeDtypeStruct((t, MLA_HEADS * MLA_V_DIM), BF16),
        compiler_params=_cparams("parallel", "parallel", "parallel"),
        name="mla_attn",
    )(hcat, hcat, hcat)
    return o_diff, o_mla


def _proj_c_kernel(x_ref, w_ref, o_ref):
    xb = x_ref[...].astype(BF16)
    for n in range(3):
        o_ref[:, n * 1024:(n + 1) * 1024] = _dot(xb, w_ref[:, n * 1024:(n + 1) * 1024]).astype(BF16)


def _proj_c(x2d, w):
    t = x2d.shape[0]
    tm = TM_PROJ
    return pl.pallas_call(
        _proj_c_kernel,
        grid=(t // tm,),
        in_specs=[pl.BlockSpec((tm, D_MODEL), lambda i: (i, 0)), pl.BlockSpec(w.shape, lambda i: (0, 0))],
        out_specs=pl.BlockSpec((tm, 3072), lambda i: (i, 0)),
        out_shape=jax.ShapeDtypeStruct((t, 3072), BF16),
        compiler_params=_cparams("parallel"),
        name="proj_c",
    )(x2d, w)


def _band_attn_kernel(q_ref, kp_ref, kc_ref, kn_ref, vp_ref, vc_ref, vn_ref, band_ref, o_ref, lse_ref,
                      *, tq, sub_len):
    i = pl.program_id(1)
    n_sub = tq // BAND_SUB
    win = BAND_SUB + 2 * BAND_HALO
    lane = lax.broadcasted_iota(jnp.int32, (BAND_SUB, LANES), 1)
    first_half = lane < DIL_HEAD_DIM
    kcol = lax.broadcasted_iota(jnp.int32, (1, win), 1)
    for pair in range(DIL_HEADS // 2):
        cs = slice(pair * LANES, (pair + 1) * LANES)
        kcat = jnp.concatenate([kp_ref[:, cs], kc_ref[:, cs], kn_ref[:, cs]], axis=0)
        vcat = jnp.concatenate([vp_ref[:, cs], vc_ref[:, cs], vn_ref[:, cs]], axis=0)
        for sb in range(n_sub):
            r0 = sb * BAND_SUB
            qp = q_ref[r0:r0 + BAND_SUB, cs]
            kw = kcat[r0:r0 + win]
            vw = vcat[r0:r0 + win]
            kpos = i * tq + r0 - BAND_HALO + kcol
            kvalid = (kpos >= 0) & (kpos < sub_len)
            outs, lses = [], []
            for half in range(2):
                hd = 2 * pair + half
                qh = jnp.where(first_half if half == 0 else jnp.logical_not(first_half), qp, jnp.zeros_like(qp))
                s = _dot_nt(qh, kw) + band_ref[hd]
                s = jnp.where(kvalid, s, NEG_INF)
                m = jnp.max(s, axis=-1, keepdims=True)
                p = jnp.exp(s - m)
                l = jnp.sum(p, axis=-1, keepdims=True)
                outs.append(_dot(p.astype(BF16), vw) * (1.0 / l))
                lses.append(m + jnp.log(l))
            o_ref[r0:r0 + BAND_SUB, cs] = jnp.where(first_half, outs[0], outs[1]).astype(BF16)
            prev = lse_ref[r0:r0 + BAND_SUB, :] if pair > 0 else jnp.zeros((BAND_SUB, LANES), F32)
            upd = jnp.where(lane == 2 * pair, lses[0], jnp.where(lane == 2 * pair + 1, lses[1], prev))
            lse_ref[r0:r0 + BAND_SUB, :] = upd


def _band_attention(hq, band, n_seq, sub_len):
    t = hq.shape[0]
    tq = TQ_BAND
    nq = sub_len // tq
    hb = tq // BAND_HALO

    def cur(col):
        return pl.BlockSpec((tq, 1024), lambda s, i: (s * nq + i, col))

    def prv(col):
        return pl.BlockSpec((BAND_HALO, 1024), lambda s, i: (jnp.maximum((s * nq + i) * hb - 1, s * nq * hb), col))

    def nxt(col):
        return pl.BlockSpec((BAND_HALO, 1024),
                            lambda s, i: (jnp.minimum((s * nq + i + 1) * hb, (s + 1) * nq * hb - 1), col))

    return pl.pallas_call(
        functools.partial(_band_attn_kernel, tq=tq, sub_len=sub_len),
        grid=(n_seq, nq),
        in_specs=[cur(0), prv(1), cur(1), nxt(1), prv(2), cur(2), nxt(2),
                  pl.BlockSpec(band.shape, lambda s, i: (0, 0, 0))],
        out_specs=[pl.BlockSpec((tq, 1024), lambda s, i: (s * nq + i, 0)),
                   pl.BlockSpec((tq, LANES), lambda s, i: (s * nq + i, 0))],
        out_shape=[jax.ShapeDtypeStruct((t, 1024), BF16), jax.ShapeDtypeStruct((t, LANES), F32)],
        compiler_params=_cparams("parallel", "parallel"),
        name="band_attn",
    )(hq, hq, hq, hq, hq, hq, hq, band)


ROUTE_FINE0 = MOE_GROUPS
ROUTE_FINE1 = MOE_GROUPS + MOE_EXPERTS


def _route_and_rank(x1, wrh_ref, wrm_ref, br_ref, ri_ref, rg_ref, cnt_ref):
    tm = x1.shape[0]
    hi = x1.astype(BF16)
    mid = (x1 - hi.astype(F32)).astype(BF16)
    logits = _dot(hi, wrh_ref[...]) + (_dot(hi, wrm_ref[...]) + _dot(mid, wrh_ref[...])) + br_ref[...]
    lane = lax.broadcasted_iota(jnp.int32, (tm, LANES), 1)
    lane_f = lane.astype(F32)

    def first_lane(mask):
        return jnp.min(jnp.where(mask, lane_f, float(LANES)), axis=-1, keepdims=True).astype(jnp.int32)

    coarse = jnp.where(lane < MOE_GROUPS, logits, NEG_INF)
    cmax = jnp.max(coarse, axis=-1, keepdims=True)
    gsel = first_lane(coarse == cmax)
    p_group = 1.0 / jnp.sum(jnp.exp(coarse - cmax), axis=-1, keepdims=True)
    in_group = ((lane >= ROUTE_FINE0) & (lane < ROUTE_FINE1)
                & (jnp.right_shift(lane - ROUTE_FINE0, 3) == gsel))
    fine = jnp.where(in_group, logits, NEG_INF)
    v1 = jnp.max(fine, axis=-1, keepdims=True)
    i1 = first_lane(fine == v1)
    fine2 = jnp.where(lane == i1, NEG_INF, fine)
    v2 = jnp.max(fine2, axis=-1, keepdims=True)
    i2 = first_lane(fine2 == v2)
    e2 = jnp.exp(v2 - v1)
    den = 1.0 + e2
    g1 = p_group * (1.0 / den)
    g2 = p_group * (e2 / den)
    sel1 = lane == i1
    sel2 = lane == i2
    onehot = jnp.where(sel1 | sel2, 1.0, 0.0)
    row = lax.broadcasted_iota(jnp.int32, (tm, tm), 0)
    col = lax.broadcasted_iota(jnp.int32, (tm, tm), 1)
    ltri = jnp.where(col < row, 1.0, 0.0).astype(BF16)
    before = _dot(ltri, onehot.astype(BF16)) + cnt_ref[...]
    rank1 = jnp.sum(jnp.where(sel1, before, 0.0), axis=-1, keepdims=True).astype(jnp.int32)
    rank2 = jnp.sum(jnp.where(sel2, before, 0.0), axis=-1, keepdims=True).astype(jnp.int32)
    cnt_ref[...] = cnt_ref[...] + jnp.sum(onehot, axis=0, keepdims=True)
    zi = jnp.zeros((tm, LANES), jnp.int32)
    ri_ref[...] = jnp.where(lane == 0, i1 - ROUTE_FINE0,
                            jnp.where(lane == 1, i2 - ROUTE_FINE0,
                                      jnp.where(lane == 2, rank1, jnp.where(lane == 3, rank2, zi))))
    rg_ref[...] = jnp.where(lane == 0, g1, jnp.where(lane == 1, g2, jnp.zeros((tm, LANES), F32)))


def _outproj_ab_kernel(oa_ref, ob_ref, wo_ref, x_ref, g_ref, b_ref, wrh_ref, wrm_ref, br_ref,
                       x1_ref, ri_ref, rg_ref, cnt_ref):
    @pl.when(pl.program_id(0) == 0)
    def _():
        cnt_ref[...] = jnp.zeros_like(cnt_ref)

    mixed = _dot(oa_ref[...], wo_ref[0:512, :]) + _dot(ob_ref[...], wo_ref[512:1024, :])
    x1 = _layer_norm(DEEPNORM_ALPHA * x_ref[...] + mixed, g_ref[...], b_ref[...])
    x1_ref[...] = x1
    _route_and_rank(x1, wrh_ref, wrm_ref, br_ref, ri_ref, rg_ref, cnt_ref)


def _outproj_c_kernel(o1_ref, o2_ref, o3_ref, l1_ref, l2_ref, l3_ref, wo_ref, x_ref, g_ref, b_ref,
                      wrh_ref, wrm_ref, br_ref, x1_ref, ri_ref, rg_ref, cnt_ref):
    @pl.when(pl.program_id(0) == 0)
    def _():
        cnt_ref[...] = jnp.zeros_like(cnt_ref)

    tm = x_ref.shape[0]
    l1, l2, l3 = l1_ref[...], l2_ref[...], l3_ref[...]
    lmax = jnp.maximum(jnp.maximum(l1, l2), l3)
    e1, e2, e3 = jnp.exp(l1 - lmax), jnp.exp(l2 - lmax), jnp.exp(l3 - lmax)
    inv = 1.0 / (e1 + e2 + e3)
    w1, w2, w3 = e1 * inv, e2 * inv, e3 * inv
    first_half = lax.broadcasted_iota(jnp.int32, (tm, LANES), 1) < DIL_HEAD_DIM
    mixed = jnp.zeros((tm, D_MODEL), F32)
    for pair in range(DIL_HEADS // 2):
        cs = slice(pair * LANES, (pair + 1) * LANES)
        ha, hb = 2 * pair, 2 * pair + 1
        om = None
        for w, o_ref in ((w1, o1_ref), (w2, o2_ref), (w3, o3_ref)):
            wf = jnp.where(first_half, w[:, ha:ha + 1], w[:, hb:hb + 1])
            term = wf * o_ref[:, cs].astype(F32)
            om = term if om is None else om + term
        mixed = mixed + _dot(om.astype(BF16), wo_ref[cs, :])
    x1 = _layer_norm(DEEPNORM_ALPHA * x_ref[...] + mixed, g_ref[...], b_ref[...])
    x1_ref[...] = x1
    _route_and_rank(x1, wrh_ref, wrm_ref, br_ref, ri_ref, rg_ref, cnt_ref)


def _outproj_ln_route(kernel_fn, mix_inputs, mix_specs, wo, x2d, g, b, wrh, wrm, br):
    t = x2d.shape[0]
    tm = TM_ROW
    full = lambda a: pl.BlockSpec(a.shape, lambda i: (0,) * a.ndim)
    row = lambda w: pl.BlockSpec((tm, w), lambda i: (i, 0))
    return pl.pallas_call(
        kernel_fn,
        grid=(t // tm,),
        in_specs=list(mix_specs) + [full(wo), row(D_MODEL), full(g), full(b), full(wrh), full(wrm), full(br)],
        out_specs=[row(D_MODEL), row(LANES), row(LANES), pl.BlockSpec((1, LANES), lambda i: (0, 0))],
        out_shape=[jax.ShapeDtypeStruct((t, D_MODEL), F32), jax.ShapeDtypeStruct((t, LANES), jnp.int32),
                   jax.ShapeDtypeStruct((t, LANES), F32), jax.ShapeDtypeStruct((1, LANES), F32)],
        compiler_params=_cparams("arbitrary"),
        name=kernel_fn.__name__.strip("_"),
    )(*mix_inputs, wo, x2d, g, b, wrh, wrm, br)


DMA_UNROLL = 8


def _dispatch_kernel(pad_end_ref, dest_ref, x_ref, xs_ref, zbuf, sem, zsem):
    tm = x_ref.shape[0]
    n_blk = xs_ref.shape[0] // MOE_BLOCK

    @pl.when(pl.program_id(0) == 0)
    def _():
        zbuf[...] = jnp.zeros_like(zbuf)
        n_used = pad_end_ref[MOE_EXPERTS - 1] // MOE_BLOCK

        def zero_copy(row0):
            return pltpu.make_async_copy(zbuf, xs_ref.at[pl.ds(pl.multiple_of(row0, MOE_BLOCK), MOE_BLOCK)], zsem)

        def has_block(e):
            return pad_end_ref[e] > (pad_end_ref[e - 1] if e > 0 else 0)

        def tail_start(b, c):
            zero_copy(b * MOE_BLOCK).start()
            return c

        def tail_wait(b, c):
            zero_copy(b * MOE_BLOCK).wait()
            return c

        for e in range(MOE_EXPERTS):
            @pl.when(has_block(e))
            def _():
                zero_copy(pad_end_ref[e] - MOE_BLOCK).start()
        lax.fori_loop(n_used, n_blk, tail_start, 0)
        for e in range(MOE_EXPERTS):
            @pl.when(has_block(e))
            def _():
                zero_copy(pad_end_ref[e] - MOE_BLOCK).wait()
        lax.fori_loop(n_used, n_blk, tail_wait, 0)

    def issue(r, c):
        for k in range(MOE_TOP_K):
            pltpu.make_async_copy(x_ref.at[pl.ds(r, 1)], xs_ref.at[pl.ds(dest_ref[0, 0, 2 * r + k], 1)], sem).start()
        return c

    lax.fori_loop(0, tm, issue, 0, unroll=DMA_UNROLL)
    for _ in range(MOE_TOP_K * tm):
        pltpu.make_async_copy(x_ref.at[pl.ds(0, 1)], xs_ref.at[pl.ds(0, 1)], sem).wait()


def _dispatch(pad_end, dest3, x1, n_slot):
    t = x1.shape[0]
    tm = TM_ROW
    return pl.pallas_call(
        _dispatch_kernel,
        grid_spec=pltpu.PrefetchScalarGridSpec(
            num_scalar_prefetch=1, grid=(t // tm,),
            in_specs=[pl.BlockSpec((1, 1, 2 * tm), lambda i, pe: (i, 0, 0), memory_space=pltpu.SMEM),
                      pl.BlockSpec((tm, D_MODEL), lambda i, pe: (i, 0))],
            out_specs=pl.BlockSpec(memory_space=pl.ANY),
            scratch_shapes=[pltpu.VMEM((MOE_BLOCK, D_MODEL), F32), pltpu.SemaphoreType.DMA(()),
                            pltpu.SemaphoreType.DMA(())]),
        out_shape=jax.ShapeDtypeStruct((n_slot, D_MODEL), F32),
        compiler_params=_cparams("arbitrary"),
        name="moe_dispatch",
    )(pad_end, dest3, x1)


def _expert_kernel(eid_ref, nused_ref, xs_ref, wg_ref, wu_ref, wd_ref, ys_ref, wgu_bf, wd_bf):
    i = pl.program_id(0)

    @pl.when((i == 0) | (eid_ref[i] != eid_ref[jnp.maximum(i - 1, 0)]))
    def _():
        wgu_bf[:, 0:MOE_HIDDEN] = wg_ref[0, 0].astype(BF16)
        wgu_bf[:, MOE_HIDDEN:2 * MOE_HIDDEN] = wu_ref[0, 0].astype(BF16)
        wd_bf[...] = wd_ref[0, 0].astype(BF16)

    @pl.when(i < nused_ref[0])
    def _():
        xb = xs_ref[...].astype(BF16)
        gu = _dot(xb, wgu_bf[...])
        gate, up = gu[:, 0:MOE_HIDDEN], gu[:, MOE_HIDDEN:2 * MOE_HIDDEN]
        hidden = gate * (1.0 / (1.0 + jnp.exp(-gate))) * up
        ys_ref[...] = _dot(hidden.astype(BF16), wd_bf[...])

    @pl.when(i >= nused_ref[0])
    def _():
        ys_ref[...] = jnp.zeros_like(ys_ref)


def _expert_mlp(blk_eid, n_used, xs, w_gate, w_up, w_down, layer):
    n_slot = xs.shape[0]
    n_blk = n_slot // MOE_BLOCK
    wspec = lambda k, n: pl.BlockSpec((1, 1, k, n), lambda i, e, u: (layer, e[i], 0, 0))
    return pl.pallas_call(
        _expert_kernel,
        grid_spec=pltpu.PrefetchScalarGridSpec(
            num_scalar_prefetch=2, grid=(n_blk,),
            in_specs=[pl.BlockSpec((MOE_BLOCK, D_MODEL), lambda i, e, u: (i, 0)),
                      wspec(D_MODEL, MOE_HIDDEN), wspec(D_MODEL, MOE_HIDDEN), wspec(MOE_HIDDEN, D_MODEL)],
            out_specs=pl.BlockSpec((MOE_BLOCK, D_MODEL), lambda i, e, u: (i, 0)),
            scratch_shapes=[pltpu.VMEM((D_MODEL, 2 * MOE_HIDDEN), BF16), pltpu.VMEM((MOE_HIDDEN, D_MODEL), BF16)]),
        out_shape=jax.ShapeDtypeStruct((n_slot, D_MODEL), F32),
        compiler_params=_cparams("arbitrary"),
        name="expert_mlp",
    )(blk_eid, n_used, xs, w_gate, w_up, w_down)


def _combine_kernel(dest_ref, rg_ref, x_ref, g_ref, b_ref, ys_ref, o_ref, buf, sem):
    tm = x_ref.shape[0]

    def issue(r, c):
        for k in range(MOE_TOP_K):
            pltpu.make_async_copy(ys_ref.at[pl.ds(dest_ref[0, 0, 2 * r + k], 1)], buf.at[k, pl.ds(r, 1)], sem).start()
        return c

    lax.fori_loop(0, tm, issue, 0, unroll=DMA_UNROLL)
    for _ in range(MOE_TOP_K * tm):
        pltpu.make_async_copy(ys_ref.at[pl.ds(0, 1)], buf.at[0, pl.ds(0, 1)], sem).wait()
    rg = rg_ref[...]
    ffn = buf[0] * rg[:, 0:1] + buf[1] * rg[:, 1:2]
    o_ref[...] = _layer_norm(DEEPNORM_ALPHA * x_ref[...] + ffn, g_ref[...], b_ref[...])


def _combine_ln(dest3, rg, x1, g, b, ys):
    t = x1.shape[0]
    tm = TM_ROW
    row = lambda w: pl.BlockSpec((tm, w), lambda i: (i, 0))
    full = lambda a: pl.BlockSpec(a.shape, lambda i: (0,) * a.ndim)
    return pl.pallas_call(
        _combine_kernel,
        grid=(t // tm,),
        in_specs=[pl.BlockSpec((1, 1, 2 * tm), lambda i: (i, 0, 0), memory_space=pltpu.SMEM),
                  row(LANES), row(D_MODEL), full(g), full(b), pl.BlockSpec(memory_space=pl.ANY)],
        out_specs=row(D_MODEL),
        out_shape=jax.ShapeDtypeStruct((t, D_MODEL), F32),
        scratch_shapes=[pltpu.VMEM((2, tm, D_MODEL), F32), pltpu.SemaphoreType.DMA(())],
        compiler_params=_cparams("arbitrary"),
        name="moe_combine",
    )(dest3, rg, x1, g, b, ys)


def _moe(x1, ri, rg, cnt, w_gate, w_up, w_down, layer, g, b):
    t = x1.shape[0]
    n_slot = t * MOE_TOP_K + MOE_EXPERTS * MOE_BLOCK
    n_blk = n_slot // MOE_BLOCK
    counts = cnt[0, ROUTE_FINE0:ROUTE_FINE1].astype(jnp.int32)
    padded = (counts + MOE_BLOCK - 1) // MOE_BLOCK * MOE_BLOCK
    pad_end = jnp.cumsum(padded).astype(jnp.int32)
    pad_start = pad_end - padded
    dest = pad_start[ri[:, 0:2]] + ri[:, 2:4]
    dest3 = dest.reshape(t // TM_ROW, 1, 2 * TM_ROW)
    blk_start = jnp.arange(n_blk, dtype=jnp.int32) * MOE_BLOCK
    blk_eid = jnp.minimum(jnp.sum((pad_end[None, :] <= blk_start[:, None]).astype(jnp.int32), axis=1),
                          MOE_EXPERTS - 1)
    n_used = pad_end[MOE_EXPERTS - 1:] // MOE_BLOCK
    xs = _dispatch(pad_end, dest3, x1, n_slot)
    ys = _expert_mlp(blk_eid, n_used, xs, w_gate, w_up, w_down, layer)
    return _combine_ln(dest3, rg, x1, g, b, ys)


def _rotate_half_cols(w):
    half = MLA_ROPE_DIM // 2
    return jnp.concatenate([-w[..., half:], w[..., :half]], axis=-1)


def _prep_ab(w_in, w_uq, w_ukv):
    wa = w_in[:, 0:3 * DA_WIDTH]
    c0 = 3 * DA_WIDTH
    w_cq = w_in[:, c0:c0 + MLA_Q_RANK]
    w_ckv = w_in[:, c0 + MLA_Q_RANK:c0 + MLA_Q_RANK + MLA_KV_RANK]
    w_kr = w_in[:, c0 + MLA_Q_RANK + MLA_KV_RANK:]
    z64 = jnp.zeros((D_MODEL, 64), F32)
    z32 = jnp.zeros((D_MODEL, 32), F32)
    wc = jnp.concatenate([w_cq, w_ckv, z64, w_kr, z32, z64, _rotate_half_cols(w_kr), z32], axis=1)
    qd = MLA_NOPE_DIM + MLA_ROPE_DIM
    wq = w_uq.reshape(MLA_Q_RANK, MLA_HEADS, qd)
    zq = jnp.zeros((MLA_Q_RANK, MLA_HEADS, LANES - qd), F32)
    plain = jnp.concatenate([wq, zq], axis=-1)
    rot = jnp.concatenate([jnp.zeros_like(wq[..., :MLA_NOPE_DIM]), _rotate_half_cols(wq[..., MLA_NOPE_DIM:]), zq],
                          axis=-1)
    wuq = jnp.concatenate([plain.reshape(MLA_Q_RANK, -1), rot.reshape(MLA_Q_RANK, -1)], axis=1)
    wkv = w_ukv.reshape(MLA_KV_RANK, MLA_HEADS, MLA_NOPE_DIM + MLA_V_DIM)
    k_pad = jnp.concatenate([wkv[..., :MLA_NOPE_DIM], jnp.zeros((MLA_KV_RANK, MLA_HEADS, 64), F32)], axis=-1)
    wukv = jnp.concatenate([k_pad.reshape(MLA_KV_RANK, -1), wkv[..., MLA_NOPE_DIM:].reshape(MLA_KV_RANK, -1)],
                           axis=1)
    return wa.astype(BF16), wc.astype(BF16), wuq.astype(BF16), wukv.astype(BF16)


def _rope_tables(seq):
    half = MLA_ROPE_DIM // 2
    inv_freq = jnp.power(ROPE_THETA, -jnp.arange(half, dtype=F32) * 2.0 / MLA_ROPE_DIM)
    ang = jnp.arange(seq, dtype=F32)[:, None] * inv_freq[None, :]
    cos = jnp.concatenate([jnp.cos(ang)] * 2, axis=-1)
    sin = jnp.concatenate([jnp.sin(ang)] * 2, axis=-1)
    scale = (MLA_NOPE_DIM + MLA_ROPE_DIM) ** -0.5
    ones = jnp.ones((seq, 64), F32)
    z64 = jnp.zeros((seq, 64), F32)
    z32 = jnp.zeros((seq, 32), F32)
    cq = jnp.concatenate([ones, cos, z32], axis=-1) * scale
    sq = jnp.concatenate([z64, sin, z32], axis=-1) * scale
    ck = jnp.concatenate([z64, cos, z32], axis=-1)
    sk = jnp.concatenate([z64, sin, z32], axis=-1)
    return cq, sq, ck, sk


def _alibi_slopes(n_heads):
    start = 2.0 ** (-8.0 / n_heads)
    return jnp.asarray([start ** (i + 1) for i in range(n_heads)], dtype=F32)


def _band_table(dilation):
    win = BAND_SUB + 2 * BAND_HALO
    rel = jnp.arange(win, dtype=jnp.int32)[None, :] - BAND_HALO - jnp.arange(BAND_SUB, dtype=jnp.int32)[:, None]
    dist = (jnp.abs(rel) * dilation).astype(F32)
    bias = -_alibi_slopes(DIL_HEADS)[:, None, None] * dist[None]
    return jnp.where((jnp.abs(rel) <= BAND_HALO)[None], bias, NEG_INF)


def _prep_router(w_group, b_group, w_route, b_route):
    pad = LANES - MOE_GROUPS - MOE_EXPERTS
    w = jnp.concatenate([w_group, w_route, jnp.zeros((D_MODEL, pad), F32)], axis=1)
    hi = w.astype(BF16)
    mid = (w - hi.astype(F32)).astype(BF16)
    bias = jnp.concatenate([b_group, b_route, jnp.zeros((pad,), F32)])[None, :]
    return hi, mid, bias


def kernel(x, w_in_ab, lam_q1, lam_k1, lam_q2, lam_k2, diff_norm_g, mla_q_norm_g, w_uq, mla_kv_norm_g, w_ukv,
           w_out_ab, w_in_c, w_out_c, ln_mix_g, ln_mix_b, moe_w_group, moe_b_group, moe_w_route, moe_b_route,
           moe_w_gate, moe_w_up, moe_w_down, ln_ffn_g, ln_ffn_b):
    batch, seq, d = x.shape
    t = batch * seq
    xs = x.reshape(t, d)

    wa, wc, wuq, wukv = _prep_ab(w_in_ab[0], w_uq[0], w_ukv[0])
    cq, sq, ck, sk = _rope_tables(seq)
    hcat = _proj_ab(xs, wa, wc, mla_q_norm_g[0][None, :], wuq, mla_kv_norm_g[0][None, :], wukv, cq, sq, ck, sk, seq)
    lam_init = 0.8 - 0.6 * math.exp(-0.3 * 0)
    lam = (jnp.exp(jnp.sum(lam_q1[0] * lam_k1[0])) - jnp.exp(jnp.sum(lam_q2[0] * lam_k2[0])) + lam_init)
    scal = jnp.concatenate([lam[None].astype(F32), _alibi_slopes(DA_HEADS), jnp.zeros((3,), F32)])
    o_diff, o_mla = _attention_ab(hcat, scal, diff_norm_g[0][None, :], batch, seq, 1.0 - lam_init)
    wrh, wrm, br = _prep_router(moe_w_group[0], moe_b_group[0], moe_w_route[0], moe_b_route[0])
    row = lambda w: pl.BlockSpec((TM_ROW, w), lambda i: (i, 0))
    x1, ri, rg, cnt = _outproj_ln_route(_outproj_ab_kernel, (o_diff, o_mla), (row(512), row(512)),
                                        w_out_ab[0].astype(BF16), xs, ln_mix_g[0][None, :], ln_mix_b[0][None, :],
                                        wrh, wrm, br)
    xs = _moe(x1, ri, rg, cnt, moe_w_gate, moe_w_up, moe_w_down, 0, ln_ffn_g[0][None, :], ln_ffn_b[0][None, :])

    dil_w = DIL_HEADS * DIL_HEAD_DIM
    w_c = jnp.concatenate([w_in_c[0][:, :dil_w] * (DIL_HEAD_DIM ** -0.5), w_in_c[0][:, dil_w:]], axis=1).astype(BF16)
    hq = _proj_c(xs, w_c)
    outs, lses = [], []
    for window, dil in DIL_PATTERNS:
        assert window // (2 * dil) == BAND_HALO
        sub = seq // dil
        hd = hq.reshape(batch, sub, dil, 3072).transpose(0, 2, 1, 3).reshape(t, 3072) if dil > 1 else hq
        o_p, lse_p = _band_attention(hd, _band_table(dil), batch * dil, sub)
        if dil > 1:
            o_p = o_p.reshape(batch, dil, sub, dil_w).transpose(0, 2, 1, 3).reshape(t, dil_w)
            lse_p = lse_p.reshape(batch, dil, sub, LANES).transpose(0, 2, 1, 3).reshape(t, LANES)
        outs.append(o_p)
        lses.append(lse_p)
    wrh, wrm, br = _prep_router(moe_w_group[1], moe_b_group[1], moe_w_route[1], moe_b_route[1])
    x1, ri, rg, cnt = _outproj_ln_route(_outproj_c_kernel, (*outs, *lses),
                                        (row(dil_w),) * 3 + (row(LANES),) * 3,
                                        w_out_c[0].astype(BF16), xs, ln_mix_g[1][None, :], ln_mix_b[1][None, :],
                                        wrh, wrm, br)
    xs = _moe(x1, ri, rg, cnt, moe_w_gate, moe_w_up, moe_w_down, 1, ln_ffn_g[1][None, :], ln_ffn_b[1][None, :])
    return xs.reshape(batch, seq, d)
```

```python
import functools
import math

import jax
import jax.numpy as jnp
from jax import lax
from jax.experimental import pallas as pl
from jax.experimental.pallas import tpu as pltpu

D_MODEL = 1024
DEPTH = 2
DA_HEADS = 4
DA_HEAD_DIM = 64
DA_WIDTH = DA_HEADS * 2 * DA_HEAD_DIM
MLA_HEADS = 4
MLA_Q_RANK = 256
MLA_KV_RANK = 128
MLA_NOPE_DIM = 64
MLA_ROPE_DIM = 32
MLA_V_DIM = 128
ROPE_THETA = 10000.0
DIL_HEADS = 16
DIL_HEAD_DIM = 64
DIL_PATTERNS = ((128, 1), (512, 4), (2048, 16))
MOE_GROUPS = 4
MOE_EXPERTS_PER_GROUP = 8
MOE_EXPERTS = MOE_GROUPS * MOE_EXPERTS_PER_GROUP
MOE_TOP_K = 2
MOE_HIDDEN = 512
MOE_BLOCK = 256
DEEPNORM_ALPHA = (2 * DEPTH) ** 0.25
NORM_EPS = 1e-5

LANES = 128
BF16 = jnp.bfloat16
F32 = jnp.float32
NEG_INF = float("-inf")
LOG2E = math.log2(math.e)

TM_PROJ = 512
TM_ROW = 256
TQ_ATTN = 256
MLA_ROW_GROUPS = 2
TQ_BAND = 256
BAND_HALO = 64
BAND_SUB = 128
VMEM_LIMIT = 56 * 1024 * 1024


def _cparams(*sem):
    return pltpu.CompilerParams(dimension_semantics=sem, vmem_limit_bytes=VMEM_LIMIT)


def _dot(a, b):
    return jnp.dot(a, b, preferred_element_type=F32)


def _dot_nt(a, b):
    return lax.dot_general(a, b, (((1,), (1,)), ((), ())), preferred_element_type=F32)


def _rms(x, g):
    return x * lax.rsqrt(jnp.mean(x * x, axis=-1, keepdims=True) + NORM_EPS) * g


def _layer_norm(y, g, b):
    mu = jnp.mean(y, axis=-1, keepdims=True)
    yc = y - mu
    var = jnp.mean(yc * yc, axis=-1, keepdims=True)
    return yc * lax.rsqrt(var + NORM_EPS) * g + b


def _proj_ab_kernel(x_ref, wa_ref, wc_ref, gq_ref, wuq_ref, gkv_ref, wukv_ref, cq_ref, sq_ref, ck_ref, sk_ref,
                    o_ref):
    xb = x_ref[...].astype(BF16)
    ha = _dot(xb, wa_ref[...])
    o_ref[:, 0:DA_WIDTH] = (ha[:, 0:DA_WIDTH] * (DA_HEAD_DIM ** -0.5 * LOG2E)).astype(BF16)
    o_ref[:, DA_WIDTH:3 * DA_WIDTH] = ha[:, DA_WIDTH:3 * DA_WIDTH].astype(BF16)
    hc = _dot(xb, wc_ref[...])
    c_q = hc[:, 0:MLA_Q_RANK]
    c_kv = hc[:, MLA_Q_RANK:MLA_Q_RANK + MLA_KV_RANK]
    kr = hc[:, 384:512]
    kr_rot = hc[:, 512:640]
    q2 = _dot(_rms(c_q, gq_ref[...]).astype(BF16), wuq_ref[...])
    kv2 = _dot(_rms(c_kv, gkv_ref[...]).astype(BF16), wukv_ref[...])
    cq, sq, ck, sk = cq_ref[...], sq_ref[...], ck_ref[...], sk_ref[...]
    k_rope = kr * ck + kr_rot * sk
    base = 3 * DA_WIDTH
    for h in range(MLA_HEADS):
        lo, hi = h * LANES, (h + 1) * LANES
        qh = q2[:, lo:hi] * cq + q2[:, 512 + lo:512 + hi] * sq
        o_ref[:, base + lo:base + hi] = qh.astype(BF16)
        o_ref[:, base + 512 + lo:base + 512 + hi] = (kv2[:, lo:hi] + k_rope).astype(BF16)
    o_ref[:, base + 1024:base + 1536] = kv2[:, 512:1024].astype(BF16)


def _proj_ab(x2d, wa, wc, gq, wuq, gkv, wukv, cq, sq, ck, sk, seq):
    t = x2d.shape[0]
    tm = TM_PROJ
    nsb = seq // tm
    full = lambda a: pl.BlockSpec(a.shape, lambda i: (0,) * a.ndim)
    tab = pl.BlockSpec((tm, LANES), lambda i: (i % nsb, 0))
    return pl.pallas_call(
        _proj_ab_kernel,
        grid=(t // tm,),
        in_specs=[pl.BlockSpec((tm, D_MODEL), lambda i: (i, 0)), full(wa), full(wc), full(gq), full(wuq),
                  full(gkv), full(wukv), tab, tab, tab, tab],
        out_specs=pl.BlockSpec((tm, 3072), lambda i: (i, 0)),
        out_shape=jax.ShapeDtypeStruct((t, 3072), BF16),
        compiler_params=_cparams("parallel"),
        name="proj_ab",
    )(x2d, wa, wc, gq, wuq, gkv, wukv, cq, sq, ck, sk)


def _diff_attn_kernel(lam_ref, q_ref, k_ref, v_ref, g_ref, u_ref, o_ref, *, tq, seq, out_scale):
    i = pl.program_id(2)
    lam = lam_ref[0]
    q = q_ref[...]
    lane = lax.broadcasted_iota(jnp.int32, q.shape, 1)
    zero = jnp.zeros_like(q)
    q1 = jnp.where(lane < DA_HEAD_DIM, q, zero)
    q2 = jnp.where(lane >= DA_HEAD_DIM, q, zero)
    k = k_ref[...]
    bias = u_ref[0, :, pl.ds(pl.multiple_of(seq - tq - i * tq, tq), seq)]

    v = v_ref[...]

    def softmax_pv(qm):
        s = _dot_nt(qm, k) + bias
        p = jnp.exp2(s - jnp.max(s, axis=-1, keepdims=True))
        l = jnp.sum(p, axis=-1, keepdims=True)
        return _dot(p.astype(BF16), v) * (1.0 / l)

    o = softmax_pv(q1) - lam * softmax_pv(q2)
    o_ref[...] = (_rms(o, g_ref[...]) * out_scale).astype(BF16)


def _mla_attn_kernel(q_ref, k_ref, v_ref, o_ref, *, tq):
    k = k_ref[...]
    v = v_ref[...]
    for r in range(0, q_ref.shape[0], tq):
        s = _dot_nt(q_ref[r:r + tq, :], k)
        p = jnp.exp2(s - jnp.max(s, axis=-1, keepdims=True))
        l = jnp.sum(p, axis=-1, keepdims=True)
        o_ref[r:r + tq, :] = (_dot(p.astype(BF16), v) * (1.0 / l)).astype(BF16)


def _alibi_table(slopes, tq, seq):
    r = jnp.arange(tq, dtype=jnp.int32)[:, None]
    m = jnp.arange(2 * seq, dtype=jnp.int32)[None, :]
    dist = jnp.abs(r + (seq - tq) - m).astype(F32)
    return -slopes[:, None, None] * dist[None]


def _attention_ab(hcat, lam, slopes, diff_g, batch, seq, out_scale):
    t = hcat.shape[0]
    tq = TQ_ATTN
    nq = seq // tq
    table = _alibi_table(slopes, tq, seq)

    o_diff = pl.pallas_call(
        functools.partial(_diff_attn_kernel, tq=tq, seq=seq, out_scale=out_scale),
        grid_spec=pltpu.PrefetchScalarGridSpec(
            num_scalar_prefetch=1, grid=(DA_HEADS, batch, nq),
            in_specs=[pl.BlockSpec((tq, LANES), lambda h, b, i, *_: (b * nq + i, h)),
                      pl.BlockSpec((seq, LANES), lambda h, b, i, *_: (b, 4 + h)),
                      pl.BlockSpec((seq, LANES), lambda h, b, i, *_: (b, 8 + h)),
                      pl.BlockSpec((1, LANES), lambda h, b, i, *_: (0, 0)),
                      pl.BlockSpec((1, tq, 2 * seq), lambda h, b, i, *_: (h, 0, 0), pipeline_mode=pl.Buffered(1))],
            out_specs=pl.BlockSpec((tq, LANES), lambda h, b, i, *_: (b * nq + i, h))),
        out_shape=jax.ShapeDtypeStruct((t, DA_WIDTH), BF16),
        compiler_params=_cparams("parallel", "parallel", "parallel"),
        name="diff_attn",
    )(lam, hcat, hcat, hcat, diff_g, table)
    rows = MLA_ROW_GROUPS * tq
    nr = seq // rows
    o_mla = pl.pallas_call(
        functools.partial(_mla_attn_kernel, tq=tq),
        grid=(batch, MLA_HEADS, nr),
        in_specs=[pl.BlockSpec((rows, LANES), lambda b, h, i: (b * nr + i, 12 + h)),
                  pl.BlockSpec((seq, LANES), lambda b, h, i: (b, 16 + h)),
                  pl.BlockSpec((seq, LANES), lambda b, h, i: (b, 20 + h))],
        out_specs=pl.BlockSpec((rows, LANES), lambda b, h, i: (b * nr + i, h)),
        out_shape=jax.ShapeDtypeStruct((t, MLA_HEADS * MLA_V_DIM), BF16),
        compiler_params=_cparams("parallel", "parallel", "parallel"),
        name="mla_attn",
    )(hcat, hcat, hcat)
    return o_diff, o_mla


def _proj_c_kernel(x_ref, w_ref, o_ref):
    xb = x_ref[...].astype(BF16)
    for n in range(3):
        o_ref[:, n * 1024:(n + 1) * 1024] = _dot(xb, w_ref[:, n * 1024:(n + 1) * 1024]).astype(BF16)


def _proj_c(x2d, w):
    t = x2d.shape[0]
    tm = TM_PROJ
    return pl.pallas_call(
        _proj_c_kernel,
        grid=(t // tm,),
        in_specs=[pl.BlockSpec((tm, D_MODEL), lambda i: (i, 0)), pl.BlockSpec(w.shape, lambda i: (0, 0))],
        out_specs=pl.BlockSpec((tm, 3072), lambda i: (i, 0)),
        out_shape=jax.ShapeDtypeStruct((t, 3072), BF16),
        compiler_params=_cparams("parallel"),
        name="proj_c",
    )(x2d, w)


def _band_attn_kernel(q_ref, kp_ref, kc_ref, kn_ref, vp_ref, vc_ref, vn_ref, band_ref, o_ref, lse_ref,
                      *, tq, sub_len):
    i = pl.program_id(1)
    n_sub = tq // BAND_SUB
    win = BAND_SUB + 2 * BAND_HALO
    lane = lax.broadcasted_iota(jnp.int32, (BAND_SUB, LANES), 1)
    first_half = lane < DIL_HEAD_DIM
    kcol = lax.broadcasted_iota(jnp.int32, (1, win), 1)
    for pair in range(DIL_HEADS // 2):
        cs = slice(pair * LANES, (pair + 1) * LANES)
        kcat = jnp.concatenate([kp_ref[:, cs], kc_ref[:, cs], kn_ref[:, cs]], axis=0)
        vcat = jnp.concatenate([vp_ref[:, cs], vc_ref[:, cs], vn_ref[:, cs]], axis=0)
        for sb in range(n_sub):
            r0 = sb * BAND_SUB
            qp = q_ref[r0:r0 + BAND_SUB, cs]
            kw = kcat[r0:r0 + win]
            vw = vcat[r0:r0 + win]
            kpos = i * tq + r0 - BAND_HALO + kcol
            kvalid = (kpos >= 0) & (kpos < sub_len)
            outs, lses = [], []
            for half in range(2):
                hd = 2 * pair + half
                qh = jnp.where(first_half if half == 0 else jnp.logical_not(first_half), qp, jnp.zeros_like(qp))
                s = _dot_nt(qh, kw) + band_ref[hd]
                s = jnp.where(kvalid, s, NEG_INF)
                m = jnp.max(s, axis=-1, keepdims=True)
                p = jnp.exp(s - m)
                l = jnp.sum(p, axis=-1, keepdims=True)
                outs.append(_dot(p.astype(BF16), vw) * (1.0 / l))
                lses.append(m + jnp.log(l))
            o_ref[r0:r0 + BAND_SUB, cs] = jnp.where(first_half, outs[0], outs[1]).astype(BF16)
            prev = lse_ref[r0:r0 + BAND_SUB, :] if pair > 0 else jnp.zeros((BAND_SUB, LANES), F32)
            upd = jnp.where(lane == 2 * pair, lses[0], jnp.where(lane == 2 * pair + 1, lses[1], prev))
            lse_ref[r0:r0 + BAND_SUB, :] = upd


def _band_attention(hq, band, n_seq, sub_len):
    t = hq.shape[0]
    tq = TQ_BAND
    nq = sub_len // tq
    hb = tq // BAND_HALO

    def cur(col):
        return pl.BlockSpec((tq, 1024), lambda s, i: (s * nq + i, col))

    def prv(col):
        return pl.BlockSpec((BAND_HALO, 1024), lambda s, i: (jnp.maximum((s * nq + i) * hb - 1, s * nq * hb), col))

    def nxt(col):
        return pl.BlockSpec((BAND_HALO, 1024),
                            lambda s, i: (jnp.minimum((s * nq + i + 1) * hb, (s + 1) * nq * hb - 1), col))

    return pl.pallas_call(
        functools.partial(_band_attn_kernel, tq=tq, sub_len=sub_len),
        grid=(n_seq, nq),
        in_specs=[cur(0), prv(1), cur(1), nxt(1), prv(2), cur(2), nxt(2),
                  pl.BlockSpec(band.shape, lambda s, i: (0, 0, 0))],
        out_specs=[pl.BlockSpec((tq, 1024), lambda s, i: (s * nq + i, 0)),
                   pl.BlockSpec((tq, LANES), lambda s, i: (s * nq + i, 0))],
        out_shape=[jax.ShapeDtypeStruct((t, 1024), BF16), jax.ShapeDtypeStruct((t, LANES), F32)],
        compiler_params=_cparams("parallel", "parallel"),
        name="band_attn",
    )(hq, hq, hq, hq, hq, hq, hq, band)


ROUTE_FINE0 = MOE_GROUPS
ROUTE_FINE1 = MOE_GROUPS + MOE_EXPERTS


def _route_and_rank(x1, wrh_ref, wrm_ref, br_ref, ri_ref, rg_ref, cnt_ref):
    tm = x1.shape[0]
    hi = x1.astype(BF16)
    mid = (x1 - hi.astype(F32)).astype(BF16)
    logits = _dot(hi, wrh_ref[...]) + (_dot(hi, wrm_ref[...]) + _dot(mid, wrh_ref[...])) + br_ref[...]
    lane = lax.broadcasted_iota(jnp.int32, (tm, LANES), 1)
    lane_f = lane.astype(F32)

    def first_lane(mask):
        return jnp.min(jnp.where(mask, lane_f, float(LANES)), axis=-1, keepdims=True).astype(jnp.int32)

    coarse = jnp.where(lane < MOE_GROUPS, logits, NEG_INF)
    cmax = jnp.max(coarse, axis=-1, keepdims=True)
    gsel = first_lane(coarse == cmax)
    p_group = 1.0 / jnp.sum(jnp.exp(coarse - cmax), axis=-1, keepdims=True)
    in_group = ((lane >= ROUTE_FINE0) & (lane < ROUTE_FINE1)
                & (jnp.right_shift(lane - ROUTE_FINE0, 3) == gsel))
    fine = jnp.where(in_group, logits, NEG_INF)
    v1 = jnp.max(fine, axis=-1, keepdims=True)
    i1 = first_lane(fine == v1)
    fine2 = jnp.where(lane == i1, NEG_INF, fine)
    v2 = jnp.max(fine2, axis=-1, keepdims=True)
    i2 = first_lane(fine2 == v2)
    e2 = jnp.exp(v2 - v1)
    den = 1.0 + e2
    g1 = p_group * (1.0 / den)
    g2 = p_group * (e2 / den)
    sel1 = lane == i1
    sel2 = lane == i2
    onehot = jnp.where(sel1 | sel2, 1.0, 0.0)
    row = lax.broadcasted_iota(jnp.int32, (tm, tm), 0)
    col = lax.broadcasted_iota(jnp.int32, (tm, tm), 1)
    ltri = jnp.where(col < row, 1.0, 0.0).astype(BF16)
    before = _dot(ltri, onehot.astype(BF16)) + cnt_ref[...]
    rank1 = jnp.sum(jnp.where(sel1, before, 0.0), axis=-1, keepdims=True).astype(jnp.int32)
    rank2 = jnp.sum(jnp.where(sel2, before, 0.0), axis=-1, keepdims=True).astype(jnp.int32)
    cnt_ref[...] = cnt_ref[...] + jnp.sum(onehot, axis=0, keepdims=True)
    zi = jnp.zeros((tm, LANES), jnp.int32)
    ri_ref[...] = jnp.where(lane == 0, i1 - ROUTE_FINE0,
                            jnp.where(lane == 1, i2 - ROUTE_FINE0,
                                      jnp.where(lane == 2, rank1, jnp.where(lane == 3, rank2, zi))))
    rg_ref[...] = jnp.where(lane == 0, g1, jnp.where(lane == 1, g2, jnp.zeros((tm, LANES), F32)))


def _outproj_ab_kernel(oa_ref, ob_ref, wo_ref, x_ref, g_ref, b_ref, wrh_ref, wrm_ref, br_ref,
                       x1_ref, ri_ref, rg_ref, cnt_ref):
    @pl.when(pl.program_id(0) == 0)
    def _():
        cnt_ref[...] = jnp.zeros_like(cnt_ref)

    mixed = _dot(oa_ref[...], wo_ref[0:512, :]) + _dot(ob_ref[...], wo_ref[512:1024, :])
    x1 = _layer_norm(DEEPNORM_ALPHA * x_ref[...] + mixed, g_ref[...], b_ref[...])
    x1_ref[...] = x1
    _route_and_rank(x1, wrh_ref, wrm_ref, br_ref, ri_ref, rg_ref, cnt_ref)


def _outproj_c_kernel(o1_ref, o2_ref, o3_ref, l1_ref, l2_ref, l3_ref, wo_ref, x_ref, g_ref, b_ref,
                      wrh_ref, wrm_ref, br_ref, x1_ref, ri_ref, rg_ref, cnt_ref):
    @pl.when(pl.program_id(0) == 0)
    def _():
        cnt_ref[...] = jnp.zeros_like(cnt_ref)

    tm = x_ref.shape[0]
    l1, l2, l3 = l1_ref[...], l2_ref[...], l3_ref[...]
    lmax = jnp.maximum(jnp.maximum(l1, l2), l3)
    e1, e2, e3 = jnp.exp(l1 - lmax), jnp.exp(l2 - lmax), jnp.exp(l3 - lmax)
    inv = 1.0 / (e1 + e2 + e3)
    w1, w2, w3 = e1 * inv, e2 * inv, e3 * inv
    first_half = lax.broadcasted_iota(jnp.int32, (tm, LANES), 1) < DIL_HEAD_DIM
    mixed = jnp.zeros((tm, D_MODEL), F32)
    for pair in range(DIL_HEADS // 2):
        cs = slice(pair * LANES, (pair + 1) * LANES)
        ha, hb = 2 * pair, 2 * pair + 1
        om = None
        for w, o_ref in ((w1, o1_ref), (w2, o2_ref), (w3, o3_ref)):
            wf = jnp.where(first_half, w[:, ha:ha + 1], w[:, hb:hb + 1])
            term = wf * o_ref[:, cs].astype(F32)
            om = term if om is None else om + term
        mixed = mixed + _dot(om.astype(BF16), wo_ref[cs, :])
    x1 = _layer_norm(DEEPNORM_ALPHA * x_ref[...] + mixed, g_ref[...], b_ref[...])
    x1_ref[...] = x1
    _route_and_rank(x1, wrh_ref, wrm_ref, br_ref, ri_ref, rg_ref, cnt_ref)


def _outproj_ln_route(kernel_fn, mix_inputs, mix_specs, wo, x2d, g, b, wrh, wrm, br):
    t = x2d.shape[0]
    tm = TM_ROW
    full = lambda a: pl.BlockSpec(a.shape, lambda i: (0,) * a.ndim)
    row = lambda w: pl.BlockSpec((tm, w), lambda i: (i, 0))
    return pl.pallas_call(
        kernel_fn,
        grid=(t // tm,),
        in_specs=list(mix_specs) + [full(wo), row(D_MODEL), full(g), full(b), full(wrh), full(wrm), full(br)],
        out_specs=[row(D_MODEL), row(LANES), row(LANES), pl.BlockSpec((1, LANES), lambda i: (0, 0))],
        out_shape=[jax.ShapeDtypeStruct((t, D_MODEL), F32), jax.ShapeDtypeStruct((t, LANES), jnp.int32),
                   jax.ShapeDtypeStruct((t, LANES), F32), jax.ShapeDtypeStruct((1, LANES), F32)],
        compiler_params=_cparams("arbitrary"),
        name=kernel_fn.__name__.strip("_"),
    )(*mix_inputs, wo, x2d, g, b, wrh, wrm, br)


DMA_UNROLL = 8


def _dispatch_kernel(pad_end_ref, dest_ref, x_ref, xs_ref, zbuf, sem, zsem):
    tm = x_ref.shape[0]
    n_blk = xs_ref.shape[0] // MOE_BLOCK

    @pl.when(pl.program_id(0) == 0)
    def _():
        zbuf[...] = jnp.zeros_like(zbuf)
        n_used = pad_end_ref[MOE_EXPERTS - 1] // MOE_BLOCK

        def zero_copy(row0):
            return pltpu.make_async_copy(zbuf, xs_ref.at[pl.ds(pl.multiple_of(row0, MOE_BLOCK), MOE_BLOCK)], zsem)

        def has_block(e):
            return pad_end_ref[e] > (pad_end_ref[e - 1] if e > 0 else 0)

        def tail_start(b, c):
            zero_copy(b * MOE_BLOCK).start()
            return c

        def tail_wait(b, c):
            zero_copy(b * MOE_BLOCK).wait()
            return c

        for e in range(MOE_EXPERTS):
            @pl.when(has_block(e))
            def _():
                zero_copy(pad_end_ref[e] - MOE_BLOCK).start()
        lax.fori_loop(n_used, n_blk, tail_start, 0)
        for e in range(MOE_EXPERTS):
            @pl.when(has_block(e))
            def _():
                zero_copy(pad_end_ref[e] - MOE_BLOCK).wait()
        lax.fori_loop(n_used, n_blk, tail_wait, 0)

    def issue(r, c):
        for k in range(MOE_TOP_K):
            pltpu.make_async_copy(x_ref.at[pl.ds(r, 1)], xs_ref.at[pl.ds(dest_ref[0, 0, 2 * r + k], 1)], sem).start()
        return c

    lax.fori_loop(0, tm, issue, 0, unroll=DMA_UNROLL)
    for _ in range(MOE_TOP_K * tm):
        pltpu.make_async_copy(x_ref.at[pl.ds(0, 1)], xs_ref.at[pl.ds(0, 1)], sem).wait()


def _dispatch(pad_end, dest3, x1, n_slot):
    t = x1.shape[0]
    tm = TM_ROW
    return pl.pallas_call(
        _dispatch_kernel,
        grid_spec=pltpu.PrefetchScalarGridSpec(
            num_scalar_prefetch=1, grid=(t // tm,),
            in_specs=[pl.BlockSpec((1, 1, 2 * tm), lambda i, pe: (i, 0, 0), memory_space=pltpu.SMEM),
                      pl.BlockSpec((tm, D_MODEL), lambda i, pe: (i, 0))],
            out_specs=pl.BlockSpec(memory_space=pl.ANY),
            scratch_shapes=[pltpu.VMEM((MOE_BLOCK, D_MODEL), F32), pltpu.SemaphoreType.DMA(()),
                            pltpu.SemaphoreType.DMA(())]),
        out_shape=jax.ShapeDtypeStruct((n_slot, D_MODEL), F32),
        compiler_params=_cparams("arbitrary"),
        name="moe_dispatch",
    )(pad_end, dest3, x1)


def _expert_kernel(eid_ref, nused_ref, xs_ref, wg_ref, wu_ref, wd_ref, ys_ref, wgu_bf, wd_bf):
    i = pl.program_id(0)

    @pl.when((i == 0) | (eid_ref[i] != eid_ref[jnp.maximum(i - 1, 0)]))
    def _():
        wgu_bf[:, 0:MOE_HIDDEN] = wg_ref[0, 0].astype(BF16)
        wgu_bf[:, MOE_HIDDEN:2 * MOE_HIDDEN] = wu_ref[0, 0].astype(BF16)
        wd_bf[...] = wd_ref[0, 0].astype(BF16)

    @pl.when(i < nused_ref[0])
    def _():
        xb = xs_ref[...].astype(BF16)
        gu = _dot(xb, wgu_bf[...])
        gate, up = gu[:, 0:MOE_HIDDEN], gu[:, MOE_HIDDEN:2 * MOE_HIDDEN]
        hidden = gate * (1.0 / (1.0 + jnp.exp(-gate))) * up
        ys_ref[...] = _dot(hidden.astype(BF16), wd_bf[...])

    @pl.when(i >= nused_ref[0])
    def _():
        ys_ref[...] = jnp.zeros_like(ys_ref)


def _expert_mlp(blk_eid, n_used, xs, w_gate, w_up, w_down, layer):
    n_slot = xs.shape[0]
    n_blk = n_slot // MOE_BLOCK
    wspec = lambda k, n: pl.BlockSpec((1, 1, k, n), lambda i, e, u: (layer, e[i], 0, 0))
    return pl.pallas_call(
        _expert_kernel,
        grid_spec=pltpu.PrefetchScalarGridSpec(
            num_scalar_prefetch=2, grid=(n_blk,),
            in_specs=[pl.BlockSpec((MOE_BLOCK, D_MODEL), lambda i, e, u: (i, 0)),
                      wspec(D_MODEL, MOE_HIDDEN), wspec(D_MODEL, MOE_HIDDEN), wspec(MOE_HIDDEN, D_MODEL)],
            out_specs=pl.BlockSpec((MOE_BLOCK, D_MODEL), lambda i, e, u: (i, 0)),
            scratch_shapes=[pltpu.VMEM((D_MODEL, 2 * MOE_HIDDEN), BF16), pltpu.VMEM((MOE_HIDDEN, D_MODEL), BF16)]),
        out_shape=jax.ShapeDtypeStruct((n_slot, D_MODEL), F32),
        compiler_params=_cparams("arbitrary"),
        name="expert_mlp",
    )(blk_eid, n_used, xs, w_gate, w_up, w_down)


def _combine_kernel(dest_ref, rg_ref, x_ref, g_ref, b_ref, ys_ref, o_ref, buf, sem):
    tm = x_ref.shape[0]

    def issue(r, c):
        for k in range(MOE_TOP_K):
            pltpu.make_async_copy(ys_ref.at[pl.ds(dest_ref[0, 0, 2 * r + k], 1)], buf.at[k, pl.ds(r, 1)], sem).start()
        return c

    lax.fori_loop(0, tm, issue, 0, unroll=DMA_UNROLL)
    for _ in range(MOE_TOP_K * tm):
        pltpu.make_async_copy(ys_ref.at[pl.ds(0, 1)], buf.at[0, pl.ds(0, 1)], sem).wait()
    rg = rg_ref[...]
    ffn = buf[0] * rg[:, 0:1] + buf[1] * rg[:, 1:2]
    o_ref[...] = _layer_norm(DEEPNORM_ALPHA * x_ref[...] + ffn, g_ref[...], b_ref[...])


def _combine_ln(dest3, rg, x1, g, b, ys):
    t = x1.shape[0]
    tm = TM_ROW
    row = lambda w: pl.BlockSpec((tm, w), lambda i: (i, 0))
    full = lambda a: pl.BlockSpec(a.shape, lambda i: (0,) * a.ndim)
    return pl.pallas_call(
        _combine_kernel,
        grid=(t // tm,),
        in_specs=[pl.BlockSpec((1, 1, 2 * tm), lambda i: (i, 0, 0), memory_space=pltpu.SMEM),
                  row(LANES), row(D_MODEL), full(g), full(b), pl.BlockSpec(memory_space=pl.ANY)],
        out_specs=row(D_MODEL),
        out_shape=jax.ShapeDtypeStruct((t, D_MODEL), F32),
        scratch_shapes=[pltpu.VMEM((2, tm, D_MODEL), F32), pltpu.SemaphoreType.DMA(())],
        compiler_params=_cparams("arbitrary"),
        name="moe_combine",
    )(dest3, rg, x1, g, b, ys)


def _moe(x1, ri, rg, cnt, w_gate, w_up, w_down, layer, g, b):
    t = x1.shape[0]
    n_slot = t * MOE_TOP_K + MOE_EXPERTS * MOE_BLOCK
    n_blk = n_slot // MOE_BLOCK
    counts = cnt[0, ROUTE_FINE0:ROUTE_FINE1].astype(jnp.int32)
    padded = (counts + MOE_BLOCK - 1) // MOE_BLOCK * MOE_BLOCK
    pad_end = jnp.cumsum(padded).astype(jnp.int32)
    pad_start = pad_end - padded
    dest = pad_start[ri[:, 0:2]] + ri[:, 2:4]
    dest3 = dest.reshape(t // TM_ROW, 1, 2 * TM_ROW)
    blk_start = jnp.arange(n_blk, dtype=jnp.int32) * MOE_BLOCK
    blk_eid = jnp.minimum(jnp.sum((pad_end[None, :] <= blk_start[:, None]).astype(jnp.int32), axis=1),
                          MOE_EXPERTS - 1)
    n_used = pad_end[MOE_EXPERTS - 1:] // MOE_BLOCK
    xs = _dispatch(pad_end, dest3, x1, n_slot)
    ys = _expert_mlp(blk_eid, n_used, xs, w_gate, w_up, w_down, layer)
    return _combine_ln(dest3, rg, x1, g, b, ys)


def _rotate_half_cols(w):
    half = MLA_ROPE_DIM // 2
    return jnp.concatenate([-w[..., half:], w[..., :half]], axis=-1)


def _prep_ab(w_in, w_uq, w_ukv):
    wa = w_in[:, 0:3 * DA_WIDTH]
    c0 = 3 * DA_WIDTH
    w_cq = w_in[:, c0:c0 + MLA_Q_RANK]
    w_ckv = w_in[:, c0 + MLA_Q_RANK:c0 + MLA_Q_RANK + MLA_KV_RANK]
    w_kr = w_in[:, c0 + MLA_Q_RANK + MLA_KV_RANK:]
    z64 = jnp.zeros((D_MODEL, 64), F32)
    z32 = jnp.zeros((D_MODEL, 32), F32)
    wc = jnp.concatenate([w_cq, w_ckv, z64, w_kr, z32, z64, _rotate_half_cols(w_kr), z32], axis=1)
    qd = MLA_NOPE_DIM + MLA_ROPE_DIM
    wq = w_uq.reshape(MLA_Q_RANK, MLA_HEADS, qd)
    zq = jnp.zeros((MLA_Q_RANK, MLA_HEADS, LANES - qd), F32)
    plain = jnp.concatenate([wq, zq], axis=-1)
    rot = jnp.concatenate([jnp.zeros_like(wq[..., :MLA_NOPE_DIM]), _rotate_half_cols(wq[..., MLA_NOPE_DIM:]), zq],
                          axis=-1)
    wuq = jnp.concatenate([plain.reshape(MLA_Q_RANK, -1), rot.reshape(MLA_Q_RANK, -1)], axis=1)
    wkv = w_ukv.reshape(MLA_KV_RANK, MLA_HEADS, MLA_NOPE_DIM + MLA_V_DIM)
    k_pad = jnp.concatenate([wkv[..., :MLA_NOPE_DIM], jnp.zeros((MLA_KV_RANK, MLA_HEADS, 64), F32)], axis=-1)
    wukv = jnp.concatenate([k_pad.reshape(MLA_KV_RANK, -1), wkv[..., MLA_NOPE_DIM:].reshape(MLA_KV_RANK, -1)],
                           axis=1)
    return wa.astype(BF16), wc.astype(BF16), wuq.astype(BF16), wukv.astype(BF16)


def _rope_tables(seq):
    half = MLA_ROPE_DIM // 2
    inv_freq = jnp.power(ROPE_THETA, -jnp.arange(half, dtype=F32) * 2.0 / MLA_ROPE_DIM)
    ang = jnp.arange(seq, dtype=F32)[:, None] * inv_freq[None, :]
    cos = jnp.concatenate([jnp.cos(ang)] * 2, axis=-1)
    sin = jnp.concatenate([jnp.sin(ang)] * 2, axis=-1)
    scale = (MLA_NOPE_DIM + MLA_ROPE_DIM) ** -0.5 * LOG2E
    ones = jnp.ones((seq, 64), F32)
    z64 = jnp.zeros((seq, 64), F32)
    z32 = jnp.zeros((seq, 32), F32)
    cq = jnp.concatenate([ones, cos, z32], axis=-1) * scale
    sq = jnp.concatenate([z64, sin, z32], axis=-1) * scale
    ck = jnp.concatenate([z64, cos, z32], axis=-1)
    sk = jnp.concatenate([z64, sin, z32], axis=-1)
    return cq, sq, ck, sk


def _alibi_slopes(n_heads):
    start = 2.0 ** (-8.0 / n_heads)
    return jnp.asarray([start ** (i + 1) for i in range(n_heads)], dtype=F32)


def _band_table(dilation):
    win = BAND_SUB + 2 * BAND_HALO
    rel = jnp.arange(win, dtype=jnp.int32)[None, :] - BAND_HALO - jnp.arange(BAND_SUB, dtype=jnp.int32)[:, None]
    dist = (jnp.abs(rel) * dilation).astype(F32)
    bias = -_alibi_slopes(DIL_HEADS)[:, None, None] * dist[None]
    return jnp.where((jnp.abs(rel) <= BAND_HALO)[None], bias, NEG_INF)


def _prep_router(w_group, b_group, w_route, b_route):
    pad = LANES - MOE_GROUPS - MOE_EXPERTS
    w = jnp.concatenate([w_group, w_route, jnp.zeros((D_MODEL, pad), F32)], axis=1)
    hi = w.astype(BF16)
    mid = (w - hi.astype(F32)).astype(BF16)
    bias = jnp.concatenate([b_group, b_route, jnp.zeros((pad,), F32)])[None, :]
    return hi, mid, bias


def kernel(x, w_in_ab, lam_q1, lam_k1, lam_q2, lam_k2, diff_norm_g, mla_q_norm_g, w_uq, mla_kv_norm_g, w_ukv,
           w_out_ab, w_in_c, w_out_c, ln_mix_g, ln_mix_b, moe_w_group, moe_b_group, moe_w_route, moe_b_route,
           moe_w_gate, moe_w_up, moe_w_down, ln_ffn_g, ln_ffn_b):
    batch, seq, d = x.shape
    t = batch * seq
    xs = x.reshape(t, d)

    wa, wc, wuq, wukv = _prep_ab(w_in_ab[0], w_uq[0], w_ukv[0])
    cq, sq, ck, sk = _rope_tables(seq)
    hcat = _proj_ab(xs, wa, wc, mla_q_norm_g[0][None, :], wuq, mla_kv_norm_g[0][None, :], wukv, cq, sq, ck, sk, seq)
    lam_init = 0.8 - 0.6 * math.exp(-0.3 * 0)
    lam = (jnp.exp(jnp.sum(lam_q1[0] * lam_k1[0])) - jnp.exp(jnp.sum(lam_q2[0] * lam_k2[0])) + lam_init)
    o_diff, o_mla = _attention_ab(hcat, lam[None].astype(F32), _alibi_slopes(DA_HEADS) * LOG2E,
                                  diff_norm_g[0][None, :], batch, seq, 1.0 - lam_init)
    wrh, wrm, br = _prep_router(moe_w_group[0], moe_b_group[0], moe_w_route[0], moe_b_route[0])
    row = lambda w: pl.BlockSpec((TM_ROW, w), lambda i: (i, 0))
    x1, ri, rg, cnt = _outproj_ln_route(_outproj_ab_kernel, (o_diff, o_mla), (row(512), row(512)),
                                        w_out_ab[0].astype(BF16), xs, ln_mix_g[0][None, :], ln_mix_b[0][None, :],
                                        wrh, wrm, br)
    xs = _moe(x1, ri, rg, cnt, moe_w_gate, moe_w_up, moe_w_down, 0, ln_ffn_g[0][None, :], ln_ffn_b[0][None, :])

    dil_w = DIL_HEADS * DIL_HEAD_DIM
    w_c = jnp.concatenate([w_in_c[0][:, :dil_w] * (DIL_HEAD_DIM ** -0.5), w_in_c[0][:, dil_w:]], axis=1).astype(BF16)
    hq = _proj_c(xs, w_c)
    outs, lses = [], []
    for window, dil in DIL_PATTERNS:
        assert window // (2 * dil) == BAND_HALO
        sub = seq // dil
        hd = hq.reshape(batch, sub, dil, 3072).transpose(0, 2, 1, 3).reshape(t, 3072) if dil > 1 else hq
        o_p, lse_p = _band_attention(hd, _band_table(dil), batch * dil, sub)
        if dil > 1:
            o_p = o_p.reshape(batch, dil, sub, dil_w).transpose(0, 2, 1, 3).reshape(t, dil_w)
            lse_p = lse_p.reshape(batch, dil, sub, LANES).transpose(0, 2, 1, 3).reshape(t, LANES)
        outs.append(o_p)
        lses.append(lse_p)
    wrh, wrm, br = _prep_router(moe_w_group[1], moe_b_group[1], moe_w_route[1], moe_b_route[1])
    x1, ri, rg, cnt = _outproj_ln_route(_outproj_c_kernel, (*outs, *lses),
                                        (row(dil_w),) * 3 + (row(LANES),) * 3,
                                        w_out_c[0].astype(BF16), xs, ln_mix_g[1][None, :], ln_mix_b[1][None, :],
                                        wrh, wrm, br)
    xs = _moe(x1, ri, rg, cnt, moe_w_gate, moe_w_up, moe_w_down, 1, ln_ffn_g[1][None, :], ln_ffn_b[1][None, :])
    return xs.reshape(batch, seq, d)
```

```python
import functools
import math

import jax
import jax.numpy as jnp
from jax import lax
from jax.experimental import pallas as pl
from jax.experimental.pallas import tpu as pltpu

D_MODEL = 1024
DEPTH = 2
DA_HEADS = 4
DA_HEAD_DIM = 64
DA_WIDTH = DA_HEADS * 2 * DA_HEAD_DIM
MLA_HEADS = 4
MLA_Q_RANK = 256
MLA_KV_RANK = 128
MLA_NOPE_DIM = 64
MLA_ROPE_DIM = 32
MLA_V_DIM = 128
ROPE_THETA = 10000.0
DIL_HEADS = 16
DIL_HEAD_DIM = 64
DIL_PATTERNS = ((128, 1), (512, 4), (2048, 16))
MOE_GROUPS = 4
MOE_EXPERTS_PER_GROUP = 8
MOE_EXPERTS = MOE_GROUPS * MOE_EXPERTS_PER_GROUP
MOE_TOP_K = 2
MOE_HIDDEN = 512
MOE_BLOCK = 256
DEEPNORM_ALPHA = (2 * DEPTH) ** 0.25
NORM_EPS = 1e-5

LANES = 128
BF16 = jnp.bfloat16
F32 = jnp.float32
NEG_INF = float("-inf")
LOG2E = math.log2(math.e)

TM_PROJ = 512
TM_ROW = 256
TQ_ATTN = 256
MLA_ROW_GROUPS = 2
MLA_GROUP_ROWS = 256
TQ_BAND = 256
BAND_HALO = 64
BAND_SUB = 128
BAND_INTERIOR, BAND_FIRST, BAND_LAST = 0, 1, 2
VMEM_LIMIT = 56 * 1024 * 1024


def _cparams(*sem):
    return pltpu.CompilerParams(dimension_semantics=sem, vmem_limit_bytes=VMEM_LIMIT)


def _dot(a, b):
    return jnp.dot(a, b, preferred_element_type=F32)


def _dot_nt(a, b):
    return lax.dot_general(a, b, (((1,), (1,)), ((), ())), preferred_element_type=F32)


def _rms(x, g):
    return x * lax.rsqrt(jnp.mean(x * x, axis=-1, keepdims=True) + NORM_EPS) * g


def _layer_norm(y, g, b):
    mu = jnp.mean(y, axis=-1, keepdims=True)
    yc = y - mu
    var = jnp.mean(yc * yc, axis=-1, keepdims=True)
    return yc * lax.rsqrt(var + NORM_EPS) * g + b


def _proj_ab_kernel(x_ref, wa_ref, wc_ref, gq_ref, wuq_ref, gkv_ref, wukv_ref, cq_ref, sq_ref, ck_ref, sk_ref,
                    o_ref):
    xb = x_ref[...].astype(BF16)
    ha = _dot(xb, wa_ref[...])
    o_ref[:, 0:DA_WIDTH] = (ha[:, 0:DA_WIDTH] * (DA_HEAD_DIM ** -0.5 * LOG2E)).astype(BF16)
    o_ref[:, DA_WIDTH:3 * DA_WIDTH] = ha[:, DA_WIDTH:3 * DA_WIDTH].astype(BF16)
    hc = _dot(xb, wc_ref[...])
    c_q = hc[:, 0:MLA_Q_RANK]
    c_kv = hc[:, MLA_Q_RANK:MLA_Q_RANK + MLA_KV_RANK]
    kr = hc[:, 384:512]
    kr_rot = hc[:, 512:640]
    q2 = _dot(_rms(c_q, gq_ref[...]).astype(BF16), wuq_ref[...])
    kv2 = _dot(_rms(c_kv, gkv_ref[...]).astype(BF16), wukv_ref[...])
    cq, sq, ck, sk = cq_ref[...], sq_ref[...], ck_ref[...], sk_ref[...]
    k_rope = kr * ck + kr_rot * sk
    base = 3 * DA_WIDTH
    for h in range(MLA_HEADS):
        lo, hi = h * LANES, (h + 1) * LANES
        qh = q2[:, lo:hi] * cq + q2[:, 512 + lo:512 + hi] * sq
        o_ref[:, base + lo:base + hi] = qh.astype(BF16)
        o_ref[:, base + 512 + lo:base + 512 + hi] = (kv2[:, lo:hi] + k_rope).astype(BF16)
    o_ref[:, base + 1024:base + 1536] = kv2[:, 512:1024].astype(BF16)


def _proj_ab(x2d, wa, wc, gq, wuq, gkv, wukv, cq, sq, ck, sk, seq):
    t = x2d.shape[0]
    tm = TM_PROJ
    nsb = seq // tm
    full = lambda a: pl.BlockSpec(a.shape, lambda i: (0,) * a.ndim)
    tab = pl.BlockSpec((tm, LANES), lambda i: (i % nsb, 0))
    return pl.pallas_call(
        _proj_ab_kernel,
        grid=(t // tm,),
        in_specs=[pl.BlockSpec((tm, D_MODEL), lambda i: (i, 0)), full(wa), full(wc), full(gq), full(wuq),
                  full(gkv), full(wukv), tab, tab, tab, tab],
        out_specs=pl.BlockSpec((tm, 3072), lambda i: (i, 0)),
        out_shape=jax.ShapeDtypeStruct((t, 3072), BF16),
        compiler_params=_cparams("parallel"),
        name="proj_ab",
    )(x2d, wa, wc, gq, wuq, gkv, wukv, cq, sq, ck, sk)


def _diff_attn_kernel(lam_ref, q_ref, k_ref, v_ref, g_ref, u_ref, o_ref, *, tq, seq, out_scale):
    i = pl.program_id(2)
    lam = lam_ref[0]
    q = q_ref[...]
    lane = lax.broadcasted_iota(jnp.int32, q.shape, 1)
    zero = jnp.zeros_like(q)
    q1 = jnp.where(lane < DA_HEAD_DIM, q, zero)
    q2 = jnp.where(lane >= DA_HEAD_DIM, q, zero)
    k = k_ref[...]
    bias = u_ref[0, :, pl.ds(pl.multiple_of(seq - tq - i * tq, tq), seq)]

    v = v_ref[...]

    def softmax_pv(qm):
        s = _dot_nt(qm, k) + bias
        p = jnp.exp2(s - jnp.max(s, axis=-1, keepdims=True))
        l = jnp.sum(p, axis=-1, keepdims=True)
        return _dot(p.astype(BF16), v) * (1.0 / l)

    o = softmax_pv(q1) - lam * softmax_pv(q2)
    o_ref[...] = (_rms(o, g_ref[...]) * out_scale).astype(BF16)


def _mla_attn_kernel(q_ref, k_ref, v_ref, o_ref, *, tq):
    k = k_ref[...]
    v = v_ref[...]
    for r in range(0, q_ref.shape[0], tq):
        s = _dot_nt(q_ref[r:r + tq, :], k)
        p = jnp.exp2(s - jnp.max(s, axis=-1, keepdims=True))
        l = jnp.sum(p, axis=-1, keepdims=True)
        o_ref[r:r + tq, :] = (_dot(p.astype(BF16), v) * (1.0 / l)).astype(BF16)


def _alibi_table(slopes, tq, seq):
    r = jnp.arange(tq, dtype=jnp.int32)[:, None]
    m = jnp.arange(2 * seq, dtype=jnp.int32)[None, :]
    dist = jnp.abs(r + (seq - tq) - m).astype(F32)
    return -slopes[:, None, None] * dist[None]


def _attention_ab(hcat, lam, slopes, diff_g, batch, seq, out_scale):
    t = hcat.shape[0]
    tq = TQ_ATTN
    nq = seq // tq
    table = _alibi_table(slopes, tq, seq)

    o_diff = pl.pallas_call(
        functools.partial(_diff_attn_kernel, tq=tq, seq=seq, out_scale=out_scale),
        grid_spec=pltpu.PrefetchScalarGridSpec(
            num_scalar_prefetch=1, grid=(DA_HEADS, batch, nq),
            in_specs=[pl.BlockSpec((tq, LANES), lambda h, b, i, *_: (b * nq + i, h)),
                      pl.BlockSpec((seq, LANES), lambda h, b, i, *_: (b, 4 + h)),
                      pl.BlockSpec((seq, LANES), lambda h, b, i, *_: (b, 8 + h)),
                      pl.BlockSpec((1, LANES), lambda h, b, i, *_: (0, 0)),
                      pl.BlockSpec((1, tq, 2 * seq), lambda h, b, i, *_: (h, 0, 0), pipeline_mode=pl.Buffered(1))],
            out_specs=pl.BlockSpec((tq, LANES), lambda h, b, i, *_: (b * nq + i, h))),
        out_shape=jax.ShapeDtypeStruct((t, DA_WIDTH), BF16),
        compiler_params=_cparams("parallel", "parallel", "parallel"),
        name="diff_attn",
    )(lam, hcat, hcat, hcat, diff_g, table)
    rows = MLA_ROW_GROUPS * MLA_GROUP_ROWS
    nr = seq // rows
    o_mla = pl.pallas_call(
        functools.partial(_mla_attn_kernel, tq=MLA_GROUP_ROWS),
        grid=(batch, MLA_HEADS, nr),
        in_specs=[pl.BlockSpec((rows, LANES), lambda b, h, i: (b * nr + i, 12 + h)),
                  pl.BlockSpec((seq, LANES), lambda b, h, i: (b, 16 + h)),
                  pl.BlockSpec((seq, LANES), lambda b, h, i: (b, 20 + h))],
        out_specs=pl.BlockSpec((rows, LANES), lambda b, h, i: (b * nr + i, h)),
        out_shape=jax.ShapeDtypeStruct((t, MLA_HEADS * MLA_V_DIM), BF16),
        compiler_params=_cparams("parallel", "parallel", "parallel"),
        name="mla_attn",
    )(hcat, hcat, hcat)
    return o_diff, o_mla


def _proj_c_kernel(x_ref, w_ref, *refs):
    (o1_ref, o4_ref, o16_ref), (h_scr, h4_scr) = refs[:3], refs[3:]
    assert [d for _, d in DIL_PATTERNS] == [1, 4, 16]
    tm = x_ref.shape[0]
    xb = x_ref[...].astype(BF16)
    for n in range(3):
        cols = slice(n * 1024, (n + 1) * 1024)
        h = _dot(xb, w_ref[:, cols])
        if n == 0:
            h = h * (DIL_HEAD_DIM ** -0.5 * LOG2E)
        o1_ref[0, 0, :, cols] = h.astype(BF16)
        for c in range(1024 // LANES):
            lanes = slice(cols.start + c * LANES, cols.start + (c + 1) * LANES)
            h_scr[...] = h[:, c * LANES:(c + 1) * LANES]
            for r4 in range(4):
                piece = h_scr[pl.ds(r4, tm // 4, stride=4), :]
                o4_ref[0, r4, :, lanes] = piece.astype(BF16)
                h4_scr[r4] = piece
            for r in range(16):
                o16_ref[0, r, :, lanes] = h4_scr[r % 4, pl.ds(r // 4, tm // 16, stride=4), :].astype(BF16)


def _proj_c(x2d, w, batch, seq):
    tm = TM_PROJ
    nb = seq // tm
    outs = pl.pallas_call(
        _proj_c_kernel,
        grid=(batch, nb),
        in_specs=[pl.BlockSpec((tm, D_MODEL), lambda b, i: (b * nb + i, 0)), pl.BlockSpec(w.shape, lambda b, i: (0, 0))],
        out_specs=[pl.BlockSpec((1, dil, tm // dil, 3072), lambda b, i: (b, 0, i, 0)) for _, dil in DIL_PATTERNS],
        out_shape=[jax.ShapeDtypeStruct((batch, dil, seq // dil, 3072), BF16) for _, dil in DIL_PATTERNS],
        scratch_shapes=[pltpu.VMEM((tm, LANES), F32), pltpu.VMEM((4, tm // 4, LANES), F32)],
        compiler_params=_cparams("parallel", "parallel"),
        name="proj_c",
    )(x2d, w)
    return [o.reshape(batch * seq, 3072) for o in outs]


def _band_attn_kernel(q_ref, kp_ref, kc_ref, kn_ref, vp_ref, vc_ref, vn_ref, band_ref, o_ref, lse_ref,
                      *, tq, sub_len):
    i = pl.program_id(1)
    n_sub = tq // BAND_SUB
    assert n_sub >= 2
    win = BAND_SUB + 2 * BAND_HALO
    lane = lax.broadcasted_iota(jnp.int32, (BAND_SUB, LANES), 1)
    first_half = lane < DIL_HEAD_DIM
    variant = [BAND_INTERIOR] * n_sub
    variant[0] = jnp.where(i == 0, BAND_FIRST, BAND_INTERIOR)
    variant[-1] = jnp.where(i == sub_len // tq - 1, BAND_LAST, BAND_INTERIOR)
    for pair in range(DIL_HEADS // 2):
        cs = slice(pair * LANES, (pair + 1) * LANES)
        kcat = jnp.concatenate([kp_ref[:, cs], kc_ref[:, cs], kn_ref[:, cs]], axis=0)
        vcat = jnp.concatenate([vp_ref[:, cs], vc_ref[:, cs], vn_ref[:, cs]], axis=0)
        for sb in range(n_sub):
            r0 = sb * BAND_SUB
            qp = q_ref[r0:r0 + BAND_SUB, cs]
            kw = kcat[r0:r0 + win]
            vw = vcat[r0:r0 + win]
            outs, lses = [], []
            for half in range(2):
                hd = 2 * pair + half
                qh = jnp.where(first_half if half == 0 else jnp.logical_not(first_half), qp, jnp.zeros_like(qp))
                s = _dot_nt(qh, kw) + band_ref[variant[sb], hd]
                m = jnp.max(s, axis=-1, keepdims=True)
                p = jnp.exp2(s - m)
                l = jnp.sum(p, axis=-1, keepdims=True)
                outs.append(_dot(p.astype(BF16), vw) * (1.0 / l))
                lses.append(m + jnp.log2(l))
            o_ref[r0:r0 + BAND_SUB, cs] = jnp.where(first_half, outs[0], outs[1]).astype(BF16)
            prev = lse_ref[r0:r0 + BAND_SUB, :] if pair > 0 else jnp.zeros((BAND_SUB, LANES), F32)
            upd = jnp.where(lane == 2 * pair, lses[0], jnp.where(lane == 2 * pair + 1, lses[1], prev))
            lse_ref[r0:r0 + BAND_SUB, :] = upd


def _band_attention(hq, band, n_seq, sub_len):
    t = hq.shape[0]
    tq = TQ_BAND
    nq = sub_len // tq
    hb = tq // BAND_HALO

    def cur(col):
        return pl.BlockSpec((tq, 1024), lambda s, i: (s * nq + i, col))

    def prv(col):
        return pl.BlockSpec((BAND_HALO, 1024), lambda s, i: (jnp.maximum((s * nq + i) * hb - 1, s * nq * hb), col))

    def nxt(col):
        return pl.BlockSpec((BAND_HALO, 1024),
                            lambda s, i: (jnp.minimum((s * nq + i + 1) * hb, (s + 1) * nq * hb - 1), col))

    return pl.pallas_call(
        functools.partial(_band_attn_kernel, tq=tq, sub_len=sub_len),
        grid=(n_seq, nq),
        in_specs=[cur(0), prv(1), cur(1), nxt(1), prv(2), cur(2), nxt(2),
                  pl.BlockSpec(band.shape, lambda s, i: (0, 0, 0, 0), pipeline_mode=pl.Buffered(1))],
        out_specs=[pl.BlockSpec((tq, 1024), lambda s, i: (s * nq + i, 0)),
                   pl.BlockSpec((tq, LANES), lambda s, i: (s * nq + i, 0))],
        out_shape=[jax.ShapeDtypeStruct((t, 1024), BF16), jax.ShapeDtypeStruct((t, LANES), F32)],
        compiler_params=_cparams("parallel", "parallel"),
        name="band_attn",
    )(hq, hq, hq, hq, hq, hq, hq, band)


ROUTE_FINE0 = MOE_GROUPS
ROUTE_FINE1 = MOE_GROUPS + MOE_EXPERTS


def _route_and_rank(x1, wrh_ref, wrm_ref, br_ref, ri_ref, rg_ref, cnt_ref):
    tm = x1.shape[0]
    hi = x1.astype(BF16)
    mid = (x1 - hi.astype(F32)).astype(BF16)
    logits = _dot(hi, wrh_ref[...]) + (_dot(hi, wrm_ref[...]) + _dot(mid, wrh_ref[...])) + br_ref[...]
    lane = lax.broadcasted_iota(jnp.int32, (tm, LANES), 1)
    lane_f = lane.astype(F32)

    def first_lane(mask):
        return jnp.min(jnp.where(mask, lane_f, float(LANES)), axis=-1, keepdims=True).astype(jnp.int32)

    coarse = jnp.where(lane < MOE_GROUPS, logits, NEG_INF)
    cmax = jnp.max(coarse, axis=-1, keepdims=True)
    gsel = first_lane(coarse == cmax)
    p_group = 1.0 / jnp.sum(jnp.exp(coarse - cmax), axis=-1, keepdims=True)
    in_group = ((lane >= ROUTE_FINE0) & (lane < ROUTE_FINE1)
                & (jnp.right_shift(lane - ROUTE_FINE0, 3) == gsel))
    fine = jnp.where(in_group, logits, NEG_INF)
    v1 = jnp.max(fine, axis=-1, keepdims=True)
    i1 = first_lane(fine == v1)
    fine2 = jnp.where(lane == i1, NEG_INF, fine)
    v2 = jnp.max(fine2, axis=-1, keepdims=True)
    i2 = first_lane(fine2 == v2)
    e2 = jnp.exp(v2 - v1)
    den = 1.0 + e2
    g1 = p_group * (1.0 / den)
    g2 = p_group * (e2 / den)
    sel1 = lane == i1
    sel2 = lane == i2
    onehot = jnp.where(sel1 | sel2, 1.0, 0.0)
    row = lax.broadcasted_iota(jnp.int32, (tm, tm), 0)
    col = lax.broadcasted_iota(jnp.int32, (tm, tm), 1)
    ltri = jnp.where(col < row, 1.0, 0.0).astype(BF16)
    before = _dot(ltri, onehot.astype(BF16)) + cnt_ref[...]
    rank1 = jnp.sum(jnp.where(sel1, before, 0.0), axis=-1, keepdims=True).astype(jnp.int32)
    rank2 = jnp.sum(jnp.where(sel2, before, 0.0), axis=-1, keepdims=True).astype(jnp.int32)
    cnt_ref[...] = cnt_ref[...] + jnp.sum(onehot, axis=0, keepdims=True)
    zi = jnp.zeros((tm, LANES), jnp.int32)
    ri_ref[...] = jnp.where(lane == 0, i1 - ROUTE_FINE0,
                            jnp.where(lane == 1, i2 - ROUTE_FINE0,
                                      jnp.where(lane == 2, rank1, jnp.where(lane == 3, rank2, zi))))
    rg_ref[...] = jnp.where(lane == 0, g1, jnp.where(lane == 1, g2, jnp.zeros((tm, LANES), F32)))


def _outproj_ab_kernel(oa_ref, ob_ref, wo_ref, x_ref, g_ref, b_ref, wrh_ref, wrm_ref, br_ref,
                       x1_ref, ri_ref, rg_ref, cnt_ref):
    @pl.when(pl.program_id(0) == 0)
    def _():
        cnt_ref[...] = jnp.zeros_like(cnt_ref)

    mixed = _dot(oa_ref[...], wo_ref[0:512, :]) + _dot(ob_ref[...], wo_ref[512:1024, :])
    x1 = _layer_norm(DEEPNORM_ALPHA * x_ref[...] + mixed, g_ref[...], b_ref[...])
    x1_ref[...] = x1
    _route_and_rank(x1, wrh_ref, wrm_ref, br_ref, ri_ref, rg_ref, cnt_ref)


def _outproj_c_kernel(o1_ref, o2_ref, o3_ref, l1_ref, l2_ref, l3_ref, wo_ref, x_ref, g_ref, b_ref,
                      wrh_ref, wrm_ref, br_ref, x1_ref, ri_ref, rg_ref, cnt_ref, o_nat, l_nat):
    @pl.when(pl.program_id(0) == 0)
    def _():
        cnt_ref[...] = jnp.zeros_like(cnt_ref)

    tm = x_ref.shape[0]
    for p, ((_, dil), o_ref, l_ref) in enumerate(zip(DIL_PATTERNS, (o1_ref, o2_ref, o3_ref), (l1_ref, l2_ref, l3_ref))):
        for r in range(dil):
            rows = pl.ds(r, tm // dil, stride=dil)
            for c in range(DIL_HEADS // 2):
                o_nat[p, c, rows, :] = o_ref[0, r, :, c * LANES:(c + 1) * LANES].astype(F32)
            l_nat[p, rows, :] = l_ref[0, r]
    l1, l2, l3 = l_nat[0], l_nat[1], l_nat[2]
    lmax = jnp.maximum(jnp.maximum(l1, l2), l3)
    e1, e2, e3 = jnp.exp2(l1 - lmax), jnp.exp2(l2 - lmax), jnp.exp2(l3 - lmax)
    inv = 1.0 / (e1 + e2 + e3)
    w1, w2, w3 = e1 * inv, e2 * inv, e3 * inv
    first_half = lax.broadcasted_iota(jnp.int32, (tm, LANES), 1) < DIL_HEAD_DIM
    mixed = jnp.zeros((tm, D_MODEL), F32)
    for pair in range(DIL_HEADS // 2):
        cs = slice(pair * LANES, (pair + 1) * LANES)
        ha, hb = 2 * pair, 2 * pair + 1
        om = None
        for p, w in enumerate((w1, w2, w3)):
            wf = jnp.where(first_half, w[:, ha:ha + 1], w[:, hb:hb + 1])
            term = wf * o_nat[p, pair]
            om = term if om is None else om + term
        mixed = mixed + _dot(om.astype(BF16), wo_ref[cs, :])
    x1 = _layer_norm(DEEPNORM_ALPHA * x_ref[...] + mixed, g_ref[...], b_ref[...])
    x1_ref[...] = x1
    _route_and_rank(x1, wrh_ref, wrm_ref, br_ref, ri_ref, rg_ref, cnt_ref)


def _outproj_ln_route(kernel_fn, mix_inputs, mix_specs, wo, x2d, g, b, wrh, wrm, br, scratch=()):
    t = x2d.shape[0]
    tm = TM_ROW
    full = lambda a: pl.BlockSpec(a.shape, lambda i: (0,) * a.ndim)
    row = lambda w: pl.BlockSpec((tm, w), lambda i: (i, 0))
    return pl.pallas_call(
        kernel_fn,
        grid=(t // tm,),
        in_specs=list(mix_specs) + [full(wo), row(D_MODEL), full(g), full(b), full(wrh), full(wrm), full(br)],
        out_specs=[row(D_MODEL), row(LANES), row(LANES), pl.BlockSpec((1, LANES), lambda i: (0, 0))],
        out_shape=[jax.ShapeDtypeStruct((t, D_MODEL), F32), jax.ShapeDtypeStruct((t, LANES), jnp.int32),
                   jax.ShapeDtypeStruct((t, LANES), F32), jax.ShapeDtypeStruct((1, LANES), F32)],
        scratch_shapes=list(scratch),
        compiler_params=_cparams("arbitrary"),
        name=kernel_fn.__name__.strip("_"),
    )(*mix_inputs, wo, x2d, g, b, wrh, wrm, br)


DMA_UNROLL = 8


def _dispatch_kernel(pad_end_ref, dest_ref, x_ref, xs_ref, zbuf, sem, zsem):
    tm = x_ref.shape[0]
    n_blk = xs_ref.shape[0] // MOE_BLOCK

    @pl.when(pl.program_id(0) == 0)
    def _():
        zbuf[...] = jnp.zeros_like(zbuf)
        n_used = pad_end_ref[MOE_EXPERTS - 1] // MOE_BLOCK

        def zero_copy(row0):
            return pltpu.make_async_copy(zbuf, xs_ref.at[pl.ds(pl.multiple_of(row0, MOE_BLOCK), MOE_BLOCK)], zsem)

        def has_block(e):
            return pad_end_ref[e] > (pad_end_ref[e - 1] if e > 0 else 0)

        def tail_start(b, c):
            zero_copy(b * MOE_BLOCK).start()
            return c

        def tail_wait(b, c):
            zero_copy(b * MOE_BLOCK).wait()
            return c

        for e in range(MOE_EXPERTS):
            @pl.when(has_block(e))
            def _():
                zero_copy(pad_end_ref[e] - MOE_BLOCK).start()
        lax.fori_loop(n_used, n_blk, tail_start, 0)
        for e in range(MOE_EXPERTS):
            @pl.when(has_block(e))
            def _():
                zero_copy(pad_end_ref[e] - MOE_BLOCK).wait()
        lax.fori_loop(n_used, n_blk, tail_wait, 0)

    def issue(r, c):
        for k in range(MOE_TOP_K):
            pltpu.make_async_copy(x_ref.at[pl.ds(r, 1)], xs_ref.at[pl.ds(dest_ref[0, 0, 2 * r + k], 1)], sem).start()
        return c

    lax.fori_loop(0, tm, issue, 0, unroll=DMA_UNROLL)
    for _ in range(MOE_TOP_K * tm):
        pltpu.make_async_copy(x_ref.at[pl.ds(0, 1)], xs_ref.at[pl.ds(0, 1)], sem).wait()


def _dispatch(pad_end, dest3, x1, n_slot):
    t = x1.shape[0]
    tm = TM_ROW
    return pl.pallas_call(
        _dispatch_kernel,
        grid_spec=pltpu.PrefetchScalarGridSpec(
            num_scalar_prefetch=1, grid=(t // tm,),
            in_specs=[pl.BlockSpec((1, 1, 2 * tm), lambda i, pe: (i, 0, 0), memory_space=pltpu.SMEM),
                      pl.BlockSpec((tm, D_MODEL), lambda i, pe: (i, 0))],
            out_specs=pl.BlockSpec(memory_space=pl.ANY),
            scratch_shapes=[pltpu.VMEM((MOE_BLOCK, D_MODEL), F32), pltpu.SemaphoreType.DMA(()),
                            pltpu.SemaphoreType.DMA(())]),
        out_shape=jax.ShapeDtypeStruct((n_slot, D_MODEL), F32),
        compiler_params=_cparams("arbitrary"),
        name="moe_dispatch",
    )(pad_end, dest3, x1)


def _expert_kernel(eid_ref, nused_ref, xs_ref, wg_ref, wu_ref, wd_ref, ys_ref, wgu_bf, wd_bf):
    i = pl.program_id(0)

    @pl.when((i == 0) | (eid_ref[i] != eid_ref[jnp.maximum(i - 1, 0)]))
    def _():
        wgu_bf[:, 0:MOE_HIDDEN] = wg_ref[0, 0].astype(BF16)
        wgu_bf[:, MOE_HIDDEN:2 * MOE_HIDDEN] = wu_ref[0, 0].astype(BF16)
        wd_bf[...] = wd_ref[0, 0].astype(BF16)

    @pl.when(i < nused_ref[0])
    def _():
        xb = xs_ref[...].astype(BF16)
        gu = _dot(xb, wgu_bf[...])
        gate, up = gu[:, 0:MOE_HIDDEN], gu[:, MOE_HIDDEN:2 * MOE_HIDDEN]
        hidden = gate * (1.0 / (1.0 + jnp.exp(-gate))) * up
        ys_ref[...] = _dot(hidden.astype(BF16), wd_bf[...])

    @pl.when(i >= nused_ref[0])
    def _():
        ys_ref[...] = jnp.zeros_like(ys_ref)


def _expert_mlp(blk_eid, n_used, xs, w_gate, w_up, w_down, layer):
    n_slot = xs.shape[0]
    n_blk = n_slot // MOE_BLOCK
    wspec = lambda k, n: pl.BlockSpec((1, 1, k, n), lambda i, e, u: (layer, e[i], 0, 0))
    return pl.pallas_call(
        _expert_kernel,
        grid_spec=pltpu.PrefetchScalarGridSpec(
            num_scalar_prefetch=2, grid=(n_blk,),
            in_specs=[pl.BlockSpec((MOE_BLOCK, D_MODEL), lambda i, e, u: (i, 0)),
                      wspec(D_MODEL, MOE_HIDDEN), wspec(D_MODEL, MOE_HIDDEN), wspec(MOE_HIDDEN, D_MODEL)],
            out_specs=pl.BlockSpec((MOE_BLOCK, D_MODEL), lambda i, e, u: (i, 0)),
            scratch_shapes=[pltpu.VMEM((D_MODEL, 2 * MOE_HIDDEN), BF16), pltpu.VMEM((MOE_HIDDEN, D_MODEL), BF16)]),
        out_shape=jax.ShapeDtypeStruct((n_slot, D_MODEL), F32),
        compiler_params=_cparams("arbitrary"),
        name="expert_mlp",
    )(blk_eid, n_used, xs, w_gate, w_up, w_down)


def _combine_kernel(dest_ref, rg_ref, x_ref, g_ref, b_ref, ys_ref, o_ref, buf, sem):
    tm = x_ref.shape[0]

    def issue(r, c):
        for k in range(MOE_TOP_K):
            pltpu.make_async_copy(ys_ref.at[pl.ds(dest_ref[0, 0, 2 * r + k], 1)], buf.at[k, pl.ds(r, 1)], sem).start()
        return c

    lax.fori_loop(0, tm, issue, 0, unroll=DMA_UNROLL)
    for _ in range(MOE_TOP_K * tm):
        pltpu.make_async_copy(ys_ref.at[pl.ds(0, 1)], buf.at[0, pl.ds(0, 1)], sem).wait()
    rg = rg_ref[...]
    ffn = buf[0] * rg[:, 0:1] + buf[1] * rg[:, 1:2]
    o_ref[...] = _layer_norm(DEEPNORM_ALPHA * x_ref[...] + ffn, g_ref[...], b_ref[...])


def _combine_ln(dest3, rg, x1, g, b, ys):
    t = x1.shape[0]
    tm = TM_ROW
    row = lambda w: pl.BlockSpec((tm, w), lambda i: (i, 0))
    full = lambda a: pl.BlockSpec(a.shape, lambda i: (0,) * a.ndim)
    return pl.pallas_call(
        _combine_kernel,
        grid=(t // tm,),
        in_specs=[pl.BlockSpec((1, 1, 2 * tm), lambda i: (i, 0, 0), memory_space=pltpu.SMEM),
                  row(LANES), row(D_MODEL), full(g), full(b), pl.BlockSpec(memory_space=pl.ANY)],
        out_specs=row(D_MODEL),
        out_shape=jax.ShapeDtypeStruct((t, D_MODEL), F32),
        scratch_shapes=[pltpu.VMEM((2, tm, D_MODEL), F32), pltpu.SemaphoreType.DMA(())],
        compiler_params=_cparams("arbitrary"),
        name="moe_combine",
    )(dest3, rg, x1, g, b, ys)


def _moe(x1, ri, rg, cnt, w_gate, w_up, w_down, layer, g, b):
    t = x1.shape[0]
    n_slot = t * MOE_TOP_K + MOE_EXPERTS * MOE_BLOCK
    n_blk = n_slot // MOE_BLOCK
    counts = cnt[0, ROUTE_FINE0:ROUTE_FINE1].astype(jnp.int32)
    padded = (counts + MOE_BLOCK - 1) // MOE_BLOCK * MOE_BLOCK
    pad_end = jnp.cumsum(padded).astype(jnp.int32)
    pad_start = pad_end - padded
    dest = pad_start[ri[:, 0:2]] + ri[:, 2:4]
    dest3 = dest.reshape(t // TM_ROW, 1, 2 * TM_ROW)
    blk_start = jnp.arange(n_blk, dtype=jnp.int32) * MOE_BLOCK
    blk_eid = jnp.minimum(jnp.sum((pad_end[None, :] <= blk_start[:, None]).astype(jnp.int32), axis=1),
                          MOE_EXPERTS - 1)
    n_used = pad_end[MOE_EXPERTS - 1:] // MOE_BLOCK
    xs = _dispatch(pad_end, dest3, x1, n_slot)
    ys = _expert_mlp(blk_eid, n_used, xs, w_gate, w_up, w_down, layer)
    return _combine_ln(dest3, rg, x1, g, b, ys)


def _rotate_half_cols(w):
    half = MLA_ROPE_DIM // 2
    return jnp.concatenate([-w[..., half:], w[..., :half]], axis=-1)


def _prep_ab(w_in, w_uq, w_ukv):
    wa = w_in[:, 0:3 * DA_WIDTH]
    c0 = 3 * DA_WIDTH
    w_cq = w_in[:, c0:c0 + MLA_Q_RANK]
    w_ckv = w_in[:, c0 + MLA_Q_RANK:c0 + MLA_Q_RANK + MLA_KV_RANK]
    w_kr = w_in[:, c0 + MLA_Q_RANK + MLA_KV_RANK:]
    z64 = jnp.zeros((D_MODEL, 64), F32)
    z32 = jnp.zeros((D_MODEL, 32), F32)
    wc = jnp.concatenate([w_cq, w_ckv, z64, w_kr, z32, z64, _rotate_half_cols(w_kr), z32], axis=1)
    qd = MLA_NOPE_DIM + MLA_ROPE_DIM
    wq = w_uq.reshape(MLA_Q_RANK, MLA_HEADS, qd)
    zq = jnp.zeros((MLA_Q_RANK, MLA_HEADS, LANES - qd), F32)
    plain = jnp.concatenate([wq, zq], axis=-1)
    rot = jnp.concatenate([jnp.zeros_like(wq[..., :MLA_NOPE_DIM]), _rotate_half_cols(wq[..., MLA_NOPE_DIM:]), zq],
                          axis=-1)
    wuq = jnp.concatenate([plain.reshape(MLA_Q_RANK, -1), rot.reshape(MLA_Q_RANK, -1)], axis=1)
    wkv = w_ukv.reshape(MLA_KV_RANK, MLA_HEADS, MLA_NOPE_DIM + MLA_V_DIM)
    k_pad = jnp.concatenate([wkv[..., :MLA_NOPE_DIM], jnp.zeros((MLA_KV_RANK, MLA_HEADS, 64), F32)], axis=-1)
    wukv = jnp.concatenate([k_pad.reshape(MLA_KV_RANK, -1), wkv[..., MLA_NOPE_DIM:].reshape(MLA_KV_RANK, -1)],
                           axis=1)
    return wa.astype(BF16), wc.astype(BF16), wuq.astype(BF16), wukv.astype(BF16)


def _rope_tables(seq):
    half = MLA_ROPE_DIM // 2
    inv_freq = jnp.power(ROPE_THETA, -jnp.arange(half, dtype=F32) * 2.0 / MLA_ROPE_DIM)
    ang = jnp.arange(seq, dtype=F32)[:, None] * inv_freq[None, :]
    cos = jnp.concatenate([jnp.cos(ang)] * 2, axis=-1)
    sin = jnp.concatenate([jnp.sin(ang)] * 2, axis=-1)
    scale = (MLA_NOPE_DIM + MLA_ROPE_DIM) ** -0.5 * LOG2E
    ones = jnp.ones((seq, 64), F32)
    z64 = jnp.zeros((seq, 64), F32)
    z32 = jnp.zeros((seq, 32), F32)
    cq = jnp.concatenate([ones, cos, z32], axis=-1) * scale
    sq = jnp.concatenate([z64, sin, z32], axis=-1) * scale
    ck = jnp.concatenate([z64, cos, z32], axis=-1)
    sk = jnp.concatenate([z64, sin, z32], axis=-1)
    return cq, sq, ck, sk


def _alibi_slopes(n_heads):
    start = 2.0 ** (-8.0 / n_heads)
    return jnp.asarray([start ** (i + 1) for i in range(n_heads)], dtype=F32)


def _band_table(dilation):
    win = BAND_SUB + 2 * BAND_HALO
    kk = jnp.arange(win, dtype=jnp.int32)[None, :]
    rel = kk - BAND_HALO - jnp.arange(BAND_SUB, dtype=jnp.int32)[:, None]
    dist = (jnp.abs(rel) * dilation).astype(F32)
    bias = -(_alibi_slopes(DIL_HEADS) * LOG2E)[:, None, None] * dist[None]
    interior = jnp.where((jnp.abs(rel) <= BAND_HALO)[None], bias, NEG_INF)
    first = jnp.where((kk >= BAND_HALO)[None], interior, NEG_INF)
    last = jnp.where((kk < BAND_SUB + BAND_HALO)[None], interior, NEG_INF)
    return jnp.stack([interior, first, last])


def _prep_router(w_group, b_group, w_route, b_route):
    pad = LANES - MOE_GROUPS - MOE_EXPERTS
    w = jnp.concatenate([w_group, w_route, jnp.zeros((D_MODEL, pad), F32)], axis=1)
    hi = w.astype(BF16)
    mid = (w - hi.astype(F32)).astype(BF16)
    bias = jnp.concatenate([b_group, b_route, jnp.zeros((pad,), F32)])[None, :]
    return hi, mid, bias


def kernel(x, w_in_ab, lam_q1, lam_k1, lam_q2, lam_k2, diff_norm_g, mla_q_norm_g, w_uq, mla_kv_norm_g, w_ukv,
           w_out_ab, w_in_c, w_out_c, ln_mix_g, ln_mix_b, moe_w_group, moe_b_group, moe_w_route, moe_b_route,
           moe_w_gate, moe_w_up, moe_w_down, ln_ffn_g, ln_ffn_b):
    batch, seq, d = x.shape
    t = batch * seq
    xs = x.reshape(t, d)

    wa, wc, wuq, wukv = _prep_ab(w_in_ab[0], w_uq[0], w_ukv[0])
    cq, sq, ck, sk = _rope_tables(seq)
    hcat = _proj_ab(xs, wa, wc, mla_q_norm_g[0][None, :], wuq, mla_kv_norm_g[0][None, :], wukv, cq, sq, ck, sk, seq)
    lam_init = 0.8 - 0.6 * math.exp(-0.3 * 0)
    lam = (jnp.exp(jnp.sum(lam_q1[0] * lam_k1[0])) - jnp.exp(jnp.sum(lam_q2[0] * lam_k2[0])) + lam_init)
    o_diff, o_mla = _attention_ab(hcat, lam[None].astype(F32), _alibi_slopes(DA_HEADS) * LOG2E,
                                  diff_norm_g[0][None, :], batch, seq, 1.0 - lam_init)
    wrh, wrm, br = _prep_router(moe_w_group[0], moe_b_group[0], moe_w_route[0], moe_b_route[0])
    row = lambda w: pl.BlockSpec((TM_ROW, w), lambda i: (i, 0))
    x1, ri, rg, cnt = _outproj_ln_route(_outproj_ab_kernel, (o_diff, o_mla), (row(512), row(512)),
                                        w_out_ab[0].astype(BF16), xs, ln_mix_g[0][None, :], ln_mix_b[0][None, :],
                                        wrh, wrm, br)
    xs = _moe(x1, ri, rg, cnt, moe_w_gate, moe_w_up, moe_w_down, 0, ln_ffn_g[0][None, :], ln_ffn_b[0][None, :])

    dil_w = DIL_HEADS * DIL_HEAD_DIM
    hqs = _proj_c(xs, w_in_c[0].astype(BF16), batch, seq)
    nb = seq // TM_ROW
    outs, lses, o_specs, l_specs = [], [], [], []
    for (window, dil), hq in zip(DIL_PATTERNS, hqs):
        assert window // (2 * dil) == BAND_HALO
        sub = seq // dil
        o_p, lse_p = _band_attention(hq, _band_table(dil), batch * dil, sub)
        outs.append(o_p.reshape(batch, dil, sub, dil_w))
        lses.append(lse_p.reshape(batch, dil, sub, LANES))
        o_specs.append(pl.BlockSpec((1, dil, TM_ROW // dil, dil_w), lambda i: (i // nb, 0, i % nb, 0)))
        l_specs.append(pl.BlockSpec((1, dil, TM_ROW // dil, LANES), lambda i: (i // nb, 0, i % nb, 0)))
    wrh, wrm, br = _prep_router(moe_w_group[1], moe_b_group[1], moe_w_route[1], moe_b_route[1])
    n_pat = len(DIL_PATTERNS)
    x1, ri, rg, cnt = _outproj_ln_route(_outproj_c_kernel, (*outs, *lses), (*o_specs, *l_specs),
                                        w_out_c[0].astype(BF16), xs, ln_mix_g[1][None, :], ln_mix_b[1][None, :],
                                        wrh, wrm, br,
                                        scratch=(pltpu.VMEM((n_pat, DIL_HEADS // 2, TM_ROW, LANES), F32),
                                                 pltpu.VMEM((n_pat, TM_ROW, LANES), F32)))
    xs = _moe(x1, ri, rg, cnt, moe_w_gate, moe_w_up, moe_w_down, 1, ln_ffn_g[1][None, :], ln_ffn_b[1][None, :])
    return xs.reshape(batch, seq, d)
```

```python
import functools
import math

import jax
import jax.numpy as jnp
from jax import lax
from jax.experimental import pallas as pl
from jax.experimental.pallas import tpu as pltpu

D_MODEL = 1024
DEPTH = 2
DA_HEADS = 4
DA_HEAD_DIM = 64
DA_WIDTH = DA_HEADS * 2 * DA_HEAD_DIM
MLA_HEADS = 4
MLA_Q_RANK = 256
MLA_KV_RANK = 128
MLA_NOPE_DIM = 64
MLA_ROPE_DIM = 32
MLA_V_DIM = 128
ROPE_THETA = 10000.0
DIL_HEADS = 16
DIL_HEAD_DIM = 64
DIL_PATTERNS = ((128, 1), (512, 4), (2048, 16))
MOE_GROUPS = 4
MOE_EXPERTS_PER_GROUP = 8
MOE_EXPERTS = MOE_GROUPS * MOE_EXPERTS_PER_GROUP
MOE_TOP_K = 2
MOE_HIDDEN = 512
MOE_BLOCK = 256
DEEPNORM_ALPHA = (2 * DEPTH) ** 0.25
NORM_EPS = 1e-5

LANES = 128
BF16 = jnp.bfloat16
F32 = jnp.float32
NEG_INF = float("-inf")
LOG2E = math.log2(math.e)

TM_PROJ = 512
TM_ROW = 512
TQ_ATTN = 256
MLA_ROW_GROUPS = 2
MLA_GROUP_ROWS = 256
TQ_BAND = 256
BAND_HALO = 64
BAND_SUB = 128
BAND_INTERIOR, BAND_FIRST, BAND_LAST = 0, 1, 2
VMEM_LIMIT = 56 * 1024 * 1024


def _cparams(*sem):
    return pltpu.CompilerParams(dimension_semantics=sem, vmem_limit_bytes=VMEM_LIMIT)


def _dot(a, b):
    return jnp.dot(a, b, preferred_element_type=F32)


def _dot_nt(a, b):
    return lax.dot_general(a, b, (((1,), (1,)), ((), ())), preferred_element_type=F32)


def _rms(x, g):
    return x * lax.rsqrt(jnp.mean(x * x, axis=-1, keepdims=True) + NORM_EPS) * g


def _layer_norm(y, g, b):
    mu = jnp.mean(y, axis=-1, keepdims=True)
    yc = y - mu
    var = jnp.mean(yc * yc, axis=-1, keepdims=True)
    return yc * lax.rsqrt(var + NORM_EPS) * g + b


def _proj_ab_kernel(x_ref, wa_ref, wc_ref, gq_ref, wuq_ref, gkv_ref, wukv_ref, cq_ref, sq_ref, ck_ref, sk_ref,
                    o_ref):
    xb = x_ref[...].astype(BF16)
    ha = _dot(xb, wa_ref[...])
    o_ref[:, 0:DA_WIDTH] = (ha[:, 0:DA_WIDTH] * (DA_HEAD_DIM ** -0.5 * LOG2E)).astype(BF16)
    o_ref[:, DA_WIDTH:3 * DA_WIDTH] = ha[:, DA_WIDTH:3 * DA_WIDTH].astype(BF16)
    hc = _dot(xb, wc_ref[...])
    c_q = hc[:, 0:MLA_Q_RANK]
    c_kv = hc[:, MLA_Q_RANK:MLA_Q_RANK + MLA_KV_RANK]
    kr = hc[:, 384:512]
    kr_rot = hc[:, 512:640]
    q2 = _dot(_rms(c_q, gq_ref[...]).astype(BF16), wuq_ref[...])
    kv2 = _dot(_rms(c_kv, gkv_ref[...]).astype(BF16), wukv_ref[...])
    cq, sq, ck, sk = cq_ref[...], sq_ref[...], ck_ref[...], sk_ref[...]
    k_rope = kr * ck + kr_rot * sk
    base = 3 * DA_WIDTH
    for h in range(MLA_HEADS):
        lo, hi = h * LANES, (h + 1) * LANES
        qh = q2[:, lo:hi] * cq + q2[:, 512 + lo:512 + hi] * sq
        o_ref[:, base + lo:base + hi] = qh.astype(BF16)
        o_ref[:, base + 512 + lo:base + 512 + hi] = (kv2[:, lo:hi] + k_rope).astype(BF16)
    o_ref[:, base + 1024:base + 1536] = kv2[:, 512:1024].astype(BF16)


def _proj_ab(x2d, wa, wc, gq, wuq, gkv, wukv, cq, sq, ck, sk, seq):
    t = x2d.shape[0]
    tm = TM_PROJ
    nsb = seq // tm
    full = lambda a: pl.BlockSpec(a.shape, lambda i: (0,) * a.ndim)
    tab = pl.BlockSpec((tm, LANES), lambda i: (i % nsb, 0))
    return pl.pallas_call(
        _proj_ab_kernel,
        grid=(t // tm,),
        in_specs=[pl.BlockSpec((tm, D_MODEL), lambda i: (i, 0)), full(wa), full(wc), full(gq), full(wuq),
                  full(gkv), full(wukv), tab, tab, tab, tab],
        out_specs=pl.BlockSpec((tm, 3072), lambda i: (i, 0)),
        out_shape=jax.ShapeDtypeStruct((t, 3072), BF16),
        compiler_params=_cparams("parallel"),
        name="proj_ab",
    )(x2d, wa, wc, gq, wuq, gkv, wukv, cq, sq, ck, sk)


def _diff_attn_kernel(lam_ref, q_ref, k_ref, v_ref, g_ref, u_ref, o_ref, *, tq, seq, out_scale):
    i = pl.program_id(2)
    lam = lam_ref[0]
    q = q_ref[...]
    lane = lax.broadcasted_iota(jnp.int32, q.shape, 1)
    zero = jnp.zeros_like(q)
    q1 = jnp.where(lane < DA_HEAD_DIM, q, zero)
    q2 = jnp.where(lane >= DA_HEAD_DIM, q, zero)
    k = k_ref[...]
    bias = u_ref[0, :, pl.ds(pl.multiple_of(seq - tq - i * tq, tq), seq)]

    v = v_ref[...]

    def softmax_pv(qm):
        s = _dot_nt(qm, k) + bias
        p = jnp.exp2(s - jnp.max(s, axis=-1, keepdims=True))
        l = jnp.sum(p, axis=-1, keepdims=True)
        return _dot(p.astype(BF16), v) * (1.0 / l)

    o = softmax_pv(q1) - lam * softmax_pv(q2)
    o_ref[...] = (_rms(o, g_ref[...]) * out_scale).astype(BF16)


def _mla_attn_kernel(q_ref, k_ref, v_ref, o_ref, *, tq):
    k = k_ref[...]
    v = v_ref[...]
    for r in range(0, q_ref.shape[0], tq):
        s = _dot_nt(q_ref[r:r + tq, :], k)
        p = jnp.exp2(s - jnp.max(s, axis=-1, keepdims=True))
        l = jnp.sum(p, axis=-1, keepdims=True)
        o_ref[r:r + tq, :] = (_dot(p.astype(BF16), v) * (1.0 / l)).astype(BF16)


def _alibi_table(slopes, tq, seq):
    r = jnp.arange(tq, dtype=jnp.int32)[:, None]
    m = jnp.arange(2 * seq, dtype=jnp.int32)[None, :]
    dist = jnp.abs(r + (seq - tq) - m).astype(F32)
    return -slopes[:, None, None] * dist[None]


def _attention_ab(hcat, lam, slopes, diff_g, batch, seq, out_scale):
    t = hcat.shape[0]
    tq = TQ_ATTN
    nq = seq // tq
    table = _alibi_table(slopes, tq, seq)

    o_diff = pl.pallas_call(
        functools.partial(_diff_attn_kernel, tq=tq, seq=seq, out_scale=out_scale),
        grid_spec=pltpu.PrefetchScalarGridSpec(
            num_scalar_prefetch=1, grid=(DA_HEADS, batch, nq),
            in_specs=[pl.BlockSpec((tq, LANES), lambda h, b, i, *_: (b * nq + i, h)),
                      pl.BlockSpec((seq, LANES), lambda h, b, i, *_: (b, 4 + h)),
                      pl.BlockSpec((seq, LANES), lambda h, b, i, *_: (b, 8 + h)),
                      pl.BlockSpec((1, LANES), lambda h, b, i, *_: (0, 0)),
                      pl.BlockSpec((1, tq, 2 * seq), lambda h, b, i, *_: (h, 0, 0), pipeline_mode=pl.Buffered(1))],
            out_specs=pl.BlockSpec((tq, LANES), lambda h, b, i, *_: (b * nq + i, h))),
        out_shape=jax.ShapeDtypeStruct((t, DA_WIDTH), BF16),
        compiler_params=_cparams("parallel", "parallel", "parallel"),
        name="diff_attn",
    )(lam, hcat, hcat, hcat, diff_g, table)
    rows = MLA_ROW_GROUPS * MLA_GROUP_ROWS
    nr = seq // rows
    o_mla = pl.pallas_call(
        functools.partial(_mla_attn_kernel, tq=MLA_GROUP_ROWS),
        grid=(batch, MLA_HEADS, nr),
        in_specs=[pl.BlockSpec((rows, LANES), lambda b, h, i: (b * nr + i, 12 + h)),
                  pl.BlockSpec((seq, LANES), lambda b, h, i: (b, 16 + h)),
                  pl.BlockSpec((seq, LANES), lambda b, h, i: (b, 20 + h))],
        out_specs=pl.BlockSpec((rows, LANES), lambda b, h, i: (b * nr + i, h)),
        out_shape=jax.ShapeDtypeStruct((t, MLA_HEADS * MLA_V_DIM), BF16),
        compiler_params=_cparams("parallel", "parallel", "parallel"),
        name="mla_attn",
    )(hcat, hcat, hcat)
    return o_diff, o_mla


def _proj_c_kernel(x_ref, w_ref, *refs):
    (o1_ref, o4_ref, o16_ref), (h_scr, h4_scr) = refs[:3], refs[3:]
    assert [d for _, d in DIL_PATTERNS] == [1, 4, 16]
    tm = x_ref.shape[0]
    xb = x_ref[...].astype(BF16)
    for n in range(3):
        cols = slice(n * 1024, (n + 1) * 1024)
        h = _dot(xb, w_ref[:, cols])
        if n == 0:
            h = h * (DIL_HEAD_DIM ** -0.5 * LOG2E)
        o1_ref[0, 0, :, cols] = h.astype(BF16)
        for c in range(1024 // LANES):
            lanes = slice(cols.start + c * LANES, cols.start + (c + 1) * LANES)
            h_scr[...] = h[:, c * LANES:(c + 1) * LANES]
            for r4 in range(4):
                piece = h_scr[pl.ds(r4, tm // 4, stride=4), :]
                o4_ref[0, r4, :, lanes] = piece.astype(BF16)
                h4_scr[r4] = piece
            for r in range(16):
                o16_ref[0, r, :, lanes] = h4_scr[r % 4, pl.ds(r // 4, tm // 16, stride=4), :].astype(BF16)


def _proj_c(x2d, w, batch, seq):
    tm = TM_PROJ
    nb = seq // tm
    outs = pl.pallas_call(
        _proj_c_kernel,
        grid=(batch, nb),
        in_specs=[pl.BlockSpec((tm, D_MODEL), lambda b, i: (b * nb + i, 0)), pl.BlockSpec(w.shape, lambda b, i: (0, 0))],
        out_specs=[pl.BlockSpec((1, dil, tm // dil, 3072), lambda b, i: (b, 0, i, 0)) for _, dil in DIL_PATTERNS],
        out_shape=[jax.ShapeDtypeStruct((batch, dil, seq // dil, 3072), BF16) for _, dil in DIL_PATTERNS],
        scratch_shapes=[pltpu.VMEM((tm, LANES), F32), pltpu.VMEM((4, tm // 4, LANES), F32)],
        compiler_params=_cparams("parallel", "parallel"),
        name="proj_c",
    )(x2d, w)
    return [o.reshape(batch * seq, 3072) for o in outs]


def _band_attn_kernel(q_ref, kp_ref, kc_ref, kn_ref, vp_ref, vc_ref, vn_ref, band_ref, o_ref, lse_ref,
                      *, tq, sub_len):
    i = pl.program_id(1)
    n_sub = tq // BAND_SUB
    assert n_sub >= 2
    win = BAND_SUB + 2 * BAND_HALO
    lane = lax.broadcasted_iota(jnp.int32, (BAND_SUB, LANES), 1)
    first_half = lane < DIL_HEAD_DIM
    variant = [BAND_INTERIOR] * n_sub
    variant[0] = jnp.where(i == 0, BAND_FIRST, BAND_INTERIOR)
    variant[-1] = jnp.where(i == sub_len // tq - 1, BAND_LAST, BAND_INTERIOR)
    for pair in range(DIL_HEADS // 2):
        cs = slice(pair * LANES, (pair + 1) * LANES)
        kcat = jnp.concatenate([kp_ref[:, cs], kc_ref[:, cs], kn_ref[:, cs]], axis=0)
        vcat = jnp.concatenate([vp_ref[:, cs], vc_ref[:, cs], vn_ref[:, cs]], axis=0)
        for sb in range(n_sub):
            r0 = sb * BAND_SUB
            qp = q_ref[r0:r0 + BAND_SUB, cs]
            kw = kcat[r0:r0 + win]
            vw = vcat[r0:r0 + win]
            outs, lses = [], []
            for half in range(2):
                hd = 2 * pair + half
                qh = jnp.where(first_half if half == 0 else jnp.logical_not(first_half), qp, jnp.zeros_like(qp))
                s = _dot_nt(qh, kw) + band_ref[variant[sb], hd]
                m = jnp.max(s, axis=-1, keepdims=True)
                p = jnp.exp2(s - m)
                l = jnp.sum(p, axis=-1, keepdims=True)
                outs.append(_dot(p.astype(BF16), vw) * (1.0 / l))
                lses.append(m + jnp.log2(l))
            o_ref[r0:r0 + BAND_SUB, cs] = jnp.where(first_half, outs[0], outs[1]).astype(BF16)
            prev = lse_ref[r0:r0 + BAND_SUB, :] if pair > 0 else jnp.zeros((BAND_SUB, LANES), F32)
            upd = jnp.where(lane == 2 * pair, lses[0], jnp.where(lane == 2 * pair + 1, lses[1], prev))
            lse_ref[r0:r0 + BAND_SUB, :] = upd


def _band_attention(hq, band, n_seq, sub_len):
    t = hq.shape[0]
    tq = TQ_BAND
    nq = sub_len // tq
    hb = tq // BAND_HALO

    def cur(col):
        return pl.BlockSpec((tq, 1024), lambda s, i: (s * nq + i, col))

    def prv(col):
        return pl.BlockSpec((BAND_HALO, 1024), lambda s, i: (jnp.maximum((s * nq + i) * hb - 1, s * nq * hb), col))

    def nxt(col):
        return pl.BlockSpec((BAND_HALO, 1024),
                            lambda s, i: (jnp.minimum((s * nq + i + 1) * hb, (s + 1) * nq * hb - 1), col))

    return pl.pallas_call(
        functools.partial(_band_attn_kernel, tq=tq, sub_len=sub_len),
        grid=(n_seq, nq),
        in_specs=[cur(0), prv(1), cur(1), nxt(1), prv(2), cur(2), nxt(2),
                  pl.BlockSpec(band.shape, lambda s, i: (0, 0, 0, 0), pipeline_mode=pl.Buffered(1))],
        out_specs=[pl.BlockSpec((tq, 1024), lambda s, i: (s * nq + i, 0)),
                   pl.BlockSpec((tq, LANES), lambda s, i: (s * nq + i, 0))],
        out_shape=[jax.ShapeDtypeStruct((t, 1024), BF16), jax.ShapeDtypeStruct((t, LANES), F32)],
        compiler_params=_cparams("parallel", "parallel"),
        name="band_attn",
    )(hq, hq, hq, hq, hq, hq, hq, band)


ROUTE_FINE0 = MOE_GROUPS
ROUTE_FINE1 = MOE_GROUPS + MOE_EXPERTS


def _route_and_rank(x1, wrh_ref, wrm_ref, br_ref, ri_ref, rg_ref, cnt_ref):
    tm = x1.shape[0]
    hi = x1.astype(BF16)
    mid = (x1 - hi.astype(F32)).astype(BF16)
    logits = _dot(hi, wrh_ref[...]) + (_dot(hi, wrm_ref[...]) + _dot(mid, wrh_ref[...])) + br_ref[...]
    lane = lax.broadcasted_iota(jnp.int32, (tm, LANES), 1)
    lane_f = lane.astype(F32)

    def first_lane(mask):
        return jnp.min(jnp.where(mask, lane_f, float(LANES)), axis=-1, keepdims=True).astype(jnp.int32)

    coarse = jnp.where(lane < MOE_GROUPS, logits, NEG_INF)
    cmax = jnp.max(coarse, axis=-1, keepdims=True)
    gsel = first_lane(coarse == cmax)
    p_group = 1.0 / jnp.sum(jnp.exp(coarse - cmax), axis=-1, keepdims=True)
    in_group = ((lane >= ROUTE_FINE0) & (lane < ROUTE_FINE1)
                & (jnp.right_shift(lane - ROUTE_FINE0, 3) == gsel))
    fine = jnp.where(in_group, logits, NEG_INF)
    v1 = jnp.max(fine, axis=-1, keepdims=True)
    i1 = first_lane(fine == v1)
    fine2 = jnp.where(lane == i1, NEG_INF, fine)
    v2 = jnp.max(fine2, axis=-1, keepdims=True)
    i2 = first_lane(fine2 == v2)
    e2 = jnp.exp(v2 - v1)
    den = 1.0 + e2
    g1 = p_group * (1.0 / den)
    g2 = p_group * (e2 / den)
    sel1 = lane == i1
    sel2 = lane == i2
    onehot = jnp.where(sel1 | sel2, 1.0, 0.0)
    row = lax.broadcasted_iota(jnp.int32, (tm, tm), 0)
    col = lax.broadcasted_iota(jnp.int32, (tm, tm), 1)
    ltri = jnp.where(col < row, 1.0, 0.0).astype(BF16)
    before = _dot(ltri, onehot.astype(BF16)) + cnt_ref[...]
    rank1 = jnp.sum(jnp.where(sel1, before, 0.0), axis=-1, keepdims=True).astype(jnp.int32)
    rank2 = jnp.sum(jnp.where(sel2, before, 0.0), axis=-1, keepdims=True).astype(jnp.int32)
    cnt_ref[...] = cnt_ref[...] + jnp.sum(onehot, axis=0, keepdims=True)
    zi = jnp.zeros((tm, LANES), jnp.int32)
    ri_ref[...] = jnp.where(lane == 0, i1 - ROUTE_FINE0,
                            jnp.where(lane == 1, i2 - ROUTE_FINE0,
                                      jnp.where(lane == 2, rank1, jnp.where(lane == 3, rank2, zi))))
    rg_ref[...] = jnp.where(lane == 0, g1, jnp.where(lane == 1, g2, jnp.zeros((tm, LANES), F32)))


def _outproj_ab_kernel(oa_ref, ob_ref, wo_ref, x_ref, g_ref, b_ref, wrh_ref, wrm_ref, br_ref,
                       x1_ref, ri_ref, rg_ref, cnt_ref):
    @pl.when(pl.program_id(0) == 0)
    def _():
        cnt_ref[...] = jnp.zeros_like(cnt_ref)

    mixed = _dot(oa_ref[...], wo_ref[0:512, :]) + _dot(ob_ref[...], wo_ref[512:1024, :])
    x1 = _layer_norm(DEEPNORM_ALPHA * x_ref[...] + mixed, g_ref[...], b_ref[...])
    x1_ref[...] = x1
    _route_and_rank(x1, wrh_ref, wrm_ref, br_ref, ri_ref, rg_ref, cnt_ref)


def _outproj_c_kernel(o1_ref, o2_ref, o3_ref, l1_ref, l2_ref, l3_ref, wo_ref, x_ref, g_ref, b_ref,
                      wrh_ref, wrm_ref, br_ref, x1_ref, ri_ref, rg_ref, cnt_ref, o_nat, l_nat):
    @pl.when(pl.program_id(0) == 0)
    def _():
        cnt_ref[...] = jnp.zeros_like(cnt_ref)

    tm = x_ref.shape[0]
    for p, ((_, dil), o_ref, l_ref) in enumerate(zip(DIL_PATTERNS, (o1_ref, o2_ref, o3_ref), (l1_ref, l2_ref, l3_ref))):
        for r in range(dil):
            rows = pl.ds(r, tm // dil, stride=dil)
            for c in range(DIL_HEADS // 2):
                o_nat[p, c, rows, :] = o_ref[0, r, :, c * LANES:(c + 1) * LANES].astype(F32)
            l_nat[p, rows, :] = l_ref[0, r]
    l1, l2, l3 = l_nat[0], l_nat[1], l_nat[2]
    lmax = jnp.maximum(jnp.maximum(l1, l2), l3)
    e1, e2, e3 = jnp.exp2(l1 - lmax), jnp.exp2(l2 - lmax), jnp.exp2(l3 - lmax)
    inv = 1.0 / (e1 + e2 + e3)
    w1, w2, w3 = e1 * inv, e2 * inv, e3 * inv
    first_half = lax.broadcasted_iota(jnp.int32, (tm, LANES), 1) < DIL_HEAD_DIM
    mixed = jnp.zeros((tm, D_MODEL), F32)
    for pair in range(DIL_HEADS // 2):
        cs = slice(pair * LANES, (pair + 1) * LANES)
        ha, hb = 2 * pair, 2 * pair + 1
        om = None
        for p, w in enumerate((w1, w2, w3)):
            wf = jnp.where(first_half, w[:, ha:ha + 1], w[:, hb:hb + 1])
            term = wf * o_nat[p, pair]
            om = term if om is None else om + term
        mixed = mixed + _dot(om.astype(BF16), wo_ref[cs, :])
    x1 = _layer_norm(DEEPNORM_ALPHA * x_ref[...] + mixed, g_ref[...], b_ref[...])
    x1_ref[...] = x1
    _route_and_rank(x1, wrh_ref, wrm_ref, br_ref, ri_ref, rg_ref, cnt_ref)


def _outproj_ln_route(kernel_fn, mix_inputs, mix_specs, wo, x2d, g, b, wrh, wrm, br, scratch=()):
    t = x2d.shape[0]
    tm = TM_ROW
    full = lambda a: pl.BlockSpec(a.shape, lambda i: (0,) * a.ndim)
    row = lambda w: pl.BlockSpec((tm, w), lambda i: (i, 0))
    return pl.pallas_call(
        kernel_fn,
        grid=(t // tm,),
        in_specs=list(mix_specs) + [full(wo), row(D_MODEL), full(g), full(b), full(wrh), full(wrm), full(br)],
        out_specs=[row(D_MODEL), row(LANES), row(LANES), pl.BlockSpec((1, LANES), lambda i: (0, 0))],
        out_shape=[jax.ShapeDtypeStruct((t, D_MODEL), F32), jax.ShapeDtypeStruct((t, LANES), jnp.int32),
                   jax.ShapeDtypeStruct((t, LANES), F32), jax.ShapeDtypeStruct((1, LANES), F32)],
        scratch_shapes=list(scratch),
        compiler_params=_cparams("arbitrary"),
        name=kernel_fn.__name__.strip("_"),
    )(*mix_inputs, wo, x2d, g, b, wrh, wrm, br)


def _expert_fused_kernel(eid_ref, tok0_ref, tok_next_ref, dst_prev_ref, dst_cur_ref, x_hbm, wg_ref, wu_ref, wd_ref,
                         ytk_hbm, wgu_bf, wd_bf, xbuf, ybuf, gsem, ssem):
    i = pl.program_id(0)
    n = pl.num_programs(0)
    par = i % 2
    blk = MOE_BLOCK

    def gather(tok_ref, r, slot):
        return pltpu.make_async_copy(x_hbm.at[pl.ds(tok_ref[0, 0, r], 1)], xbuf.at[slot, pl.ds(r, 1)], gsem)

    def scatter(row, r, slot):
        return pltpu.make_async_copy(ybuf.at[slot, pl.ds(r, 1)], ytk_hbm.at[pl.ds(row, 1)], ssem)

    def wait_rows(sem_copy):
        for _ in range(blk):
            sem_copy.wait()

    @pl.when(i == 0)
    def _():
        ybuf[...] = jnp.zeros_like(ybuf)
        for r in range(blk):
            gather(tok0_ref, r, 0).start()

    @pl.when((i == 0) | (eid_ref[i] != eid_ref[jnp.maximum(i - 1, 0)]))
    def _():
        wgu_bf[:, 0:MOE_HIDDEN] = wg_ref[0, 0].astype(BF16)
        wgu_bf[:, MOE_HIDDEN:2 * MOE_HIDDEN] = wu_ref[0, 0].astype(BF16)
        wd_bf[...] = wd_ref[0, 0].astype(BF16)

    wait_rows(gather(tok0_ref, 0, 0))

    @pl.when(i > 0)
    def _():
        wait_rows(scatter(0, 0, 0))

    for r in range(blk):
        gather(tok_next_ref, r, 1 - par).start()
        scatter(dst_prev_ref[0, 0, r], r, 1 - par).start()
    xb = xbuf[par].astype(BF16)
    gu = _dot(xb, wgu_bf[...])
    gate, up = gu[:, 0:MOE_HIDDEN], gu[:, MOE_HIDDEN:2 * MOE_HIDDEN]
    hidden = gate * (1.0 / (1.0 + jnp.exp(-gate))) * up
    ybuf[par] = _dot(hidden.astype(BF16), wd_bf[...])

    @pl.when(i == n - 1)
    def _():
        for r in range(blk):
            scatter(dst_cur_ref[0, 0, r], r, par).start()
        wait_rows(scatter(0, 0, 0))
        wait_rows(scatter(0, 0, 0))
        wait_rows(gather(tok0_ref, 0, 0))


def _expert_fused(blk_eid, slot_tok, slot_dst, x1, w_gate, w_up, w_down, layer):
    n_blk = slot_tok.shape[0]
    idx = lambda f: pl.BlockSpec((1, 1, MOE_BLOCK), lambda i, e: (f(i), 0, 0), memory_space=pltpu.SMEM)
    wspec = lambda k, n: pl.BlockSpec((1, 1, k, n), lambda i, e: (layer, e[i], 0, 0))
    return pl.pallas_call(
        _expert_fused_kernel,
        grid_spec=pltpu.PrefetchScalarGridSpec(
            num_scalar_prefetch=1, grid=(n_blk,),
            in_specs=[idx(lambda i: 0), idx(lambda i: jnp.minimum(i + 1, n_blk - 1)),
                      idx(lambda i: jnp.maximum(i - 1, 0)), idx(lambda i: i),
                      pl.BlockSpec(memory_space=pl.ANY),
                      wspec(D_MODEL, MOE_HIDDEN), wspec(D_MODEL, MOE_HIDDEN), wspec(MOE_HIDDEN, D_MODEL)],
            out_specs=pl.BlockSpec(memory_space=pl.ANY),
            scratch_shapes=[pltpu.VMEM((D_MODEL, 2 * MOE_HIDDEN), BF16), pltpu.VMEM((MOE_HIDDEN, D_MODEL), BF16),
                            pltpu.VMEM((2, MOE_BLOCK, D_MODEL), F32), pltpu.VMEM((2, MOE_BLOCK, D_MODEL), F32),
                            pltpu.SemaphoreType.DMA(()), pltpu.SemaphoreType.DMA(())]),
        out_shape=jax.ShapeDtypeStruct((n_blk * MOE_BLOCK, D_MODEL), F32),
        compiler_params=_cparams("arbitrary"),
        name="expert_fused",
    )(blk_eid, slot_tok, slot_tok, slot_dst, slot_dst, x1, w_gate, w_up, w_down)


def _combine_dense_kernel(rg_ref, x_ref, y_ref, g_ref, b_ref, o_ref):
    rg = rg_ref[...]
    ffn = y_ref[:, 0:D_MODEL] * rg[:, 0:1] + y_ref[:, D_MODEL:2 * D_MODEL] * rg[:, 1:2]
    o_ref[...] = _layer_norm(DEEPNORM_ALPHA * x_ref[...] + ffn, g_ref[...], b_ref[...])


def _combine_dense(rg, x1, ytk2, g, b):
    t = x1.shape[0]
    tm = TM_ROW
    row = lambda w: pl.BlockSpec((tm, w), lambda i: (i, 0))
    full = lambda a: pl.BlockSpec(a.shape, lambda i: (0,) * a.ndim)
    return pl.pallas_call(
        _combine_dense_kernel,
        grid=(t // tm,),
        in_specs=[row(LANES), row(D_MODEL), row(2 * D_MODEL), full(g), full(b)],
        out_specs=row(D_MODEL),
        out_shape=jax.ShapeDtypeStruct((t, D_MODEL), F32),
        compiler_params=_cparams("parallel"),
        name="moe_combine_dense",
    )(rg, x1, ytk2, g, b)


def _moe_fused(x1, ri, rg, cnt, w_gate, w_up, w_down, layer, g, b):
    t = x1.shape[0]
    n_asg = t * MOE_TOP_K
    n_slot = n_asg + MOE_EXPERTS * MOE_BLOCK
    n_blk = n_slot // MOE_BLOCK
    counts = cnt[0, ROUTE_FINE0:ROUTE_FINE1].astype(jnp.int32)
    padded = (counts + MOE_BLOCK - 1) // MOE_BLOCK * MOE_BLOCK
    pad_end = jnp.cumsum(padded).astype(jnp.int32)
    pad_start = pad_end - padded
    dest = (pad_start[ri[:, 0:2]] + ri[:, 2:4]).reshape(-1)
    slot_asg = jnp.full((n_slot,), n_asg, jnp.int32).at[dest].set(jnp.arange(n_asg, dtype=jnp.int32))
    is_pad = slot_asg >= n_asg
    pad_rank = jnp.cumsum(is_pad.astype(jnp.int32)) - 1
    slot_tok = jnp.where(is_pad, 0, slot_asg // MOE_TOP_K).reshape(n_blk, 1, MOE_BLOCK)
    slot_dst = jnp.where(is_pad, n_asg + pad_rank, slot_asg).reshape(n_blk, 1, MOE_BLOCK)
    blk_start = jnp.arange(n_blk, dtype=jnp.int32) * MOE_BLOCK
    blk_eid = jnp.minimum(jnp.sum((pad_end[None, :] <= blk_start[:, None]).astype(jnp.int32), axis=1),
                          MOE_EXPERTS - 1)
    ytk = _expert_fused(blk_eid, slot_tok, slot_dst, x1, w_gate, w_up, w_down, layer)
    ytk2 = ytk.reshape(n_slot // MOE_TOP_K, MOE_TOP_K * D_MODEL)
    return _combine_dense(rg, x1, ytk2, g, b)


def _rotate_half_cols(w):
    half = MLA_ROPE_DIM // 2
    return jnp.concatenate([-w[..., half:], w[..., :half]], axis=-1)


def _prep_ab(w_in, w_uq, w_ukv):
    wa = w_in[:, 0:3 * DA_WIDTH]
    c0 = 3 * DA_WIDTH
    w_cq = w_in[:, c0:c0 + MLA_Q_RANK]
    w_ckv = w_in[:, c0 + MLA_Q_RANK:c0 + MLA_Q_RANK + MLA_KV_RANK]
    w_kr = w_in[:, c0 + MLA_Q_RANK + MLA_KV_RANK:]
    z64 = jnp.zeros((D_MODEL, 64), F32)
    z32 = jnp.zeros((D_MODEL, 32), F32)
    wc = jnp.concatenate([w_cq, w_ckv, z64, w_kr, z32, z64, _rotate_half_cols(w_kr), z32], axis=1)
    qd = MLA_NOPE_DIM + MLA_ROPE_DIM
    wq = w_uq.reshape(MLA_Q_RANK, MLA_HEADS, qd)
    zq = jnp.zeros((MLA_Q_RANK, MLA_HEADS, LANES - qd), F32)
    plain = jnp.concatenate([wq, zq], axis=-1)
    rot = jnp.concatenate([jnp.zeros_like(wq[..., :MLA_NOPE_DIM]), _rotate_half_cols(wq[..., MLA_NOPE_DIM:]), zq],
                          axis=-1)
    wuq = jnp.concatenate([plain.reshape(MLA_Q_RANK, -1), rot.reshape(MLA_Q_RANK, -1)], axis=1)
    wkv = w_ukv.reshape(MLA_KV_RANK, MLA_HEADS, MLA_NOPE_DIM + MLA_V_DIM)
    k_pad = jnp.concatenate([wkv[..., :MLA_NOPE_DIM], jnp.zeros((MLA_KV_RANK, MLA_HEADS, 64), F32)], axis=-1)
    wukv = jnp.concatenate([k_pad.reshape(MLA_KV_RANK, -1), wkv[..., MLA_NOPE_DIM:].reshape(MLA_KV_RANK, -1)],
                           axis=1)
    return wa.astype(BF16), wc.astype(BF16), wuq.astype(BF16), wukv.astype(BF16)


def _rope_tables(seq):
    half = MLA_ROPE_DIM // 2
    inv_freq = jnp.power(ROPE_THETA, -jnp.arange(half, dtype=F32) * 2.0 / MLA_ROPE_DIM)
    ang = jnp.arange(seq, dtype=F32)[:, None] * inv_freq[None, :]
    cos = jnp.concatenate([jnp.cos(ang)] * 2, axis=-1)
    sin = jnp.concatenate([jnp.sin(ang)] * 2, axis=-1)
    scale = (MLA_NOPE_DIM + MLA_ROPE_DIM) ** -0.5 * LOG2E
    ones = jnp.ones((seq, 64), F32)
    z64 = jnp.zeros((seq, 64), F32)
    z32 = jnp.zeros((seq, 32), F32)
    cq = jnp.concatenate([ones, cos, z32], axis=-1) * scale
    sq = jnp.concatenate([z64, sin, z32], axis=-1) * scale
    ck = jnp.concatenate([z64, cos, z32], axis=-1)
    sk = jnp.concatenate([z64, sin, z32], axis=-1)
    return cq, sq, ck, sk


def _alibi_slopes(n_heads):
    start = 2.0 ** (-8.0 / n_heads)
    return jnp.asarray([start ** (i + 1) for i in range(n_heads)], dtype=F32)


def _band_table(dilation):
    win = BAND_SUB + 2 * BAND_HALO
    kk = jnp.arange(win, dtype=jnp.int32)[None, :]
    rel = kk - BAND_HALO - jnp.arange(BAND_SUB, dtype=jnp.int32)[:, None]
    dist = (jnp.abs(rel) * dilation).astype(F32)
    bias = -(_alibi_slopes(DIL_HEADS) * LOG2E)[:, None, None] * dist[None]
    interior = jnp.where((jnp.abs(rel) <= BAND_HALO)[None], bias, NEG_INF)
    first = jnp.where((kk >= BAND_HALO)[None], interior, NEG_INF)
    last = jnp.where((kk < BAND_SUB + BAND_HALO)[None], interior, NEG_INF)
    return jnp.stack([interior, first, last])


def _prep_router(w_group, b_group, w_route, b_route):
    pad = LANES - MOE_GROUPS - MOE_EXPERTS
    w = jnp.concatenate([w_group, w_route, jnp.zeros((D_MODEL, pad), F32)], axis=1)
    hi = w.astype(BF16)
    mid = (w - hi.astype(F32)).astype(BF16)
    bias = jnp.concatenate([b_group, b_route, jnp.zeros((pad,), F32)])[None, :]
    return hi, mid, bias


def kernel(x, w_in_ab, lam_q1, lam_k1, lam_q2, lam_k2, diff_norm_g, mla_q_norm_g, w_uq, mla_kv_norm_g, w_ukv,
           w_out_ab, w_in_c, w_out_c, ln_mix_g, ln_mix_b, moe_w_group, moe_b_group, moe_w_route, moe_b_route,
           moe_w_gate, moe_w_up, moe_w_down, ln_ffn_g, ln_ffn_b):
    batch, seq, d = x.shape
    t = batch * seq
    xs = x.reshape(t, d)

    wa, wc, wuq, wukv = _prep_ab(w_in_ab[0], w_uq[0], w_ukv[0])
    cq, sq, ck, sk = _rope_tables(seq)
    hcat = _proj_ab(xs, wa, wc, mla_q_norm_g[0][None, :], wuq, mla_kv_norm_g[0][None, :], wukv, cq, sq, ck, sk, seq)
    lam_init = 0.8 - 0.6 * math.exp(-0.3 * 0)
    lam = (jnp.exp(jnp.sum(lam_q1[0] * lam_k1[0])) - jnp.exp(jnp.sum(lam_q2[0] * lam_k2[0])) + lam_init)
    o_diff, o_mla = _attention_ab(hcat, lam[None].astype(F32), _alibi_slopes(DA_HEADS) * LOG2E,
                                  diff_norm_g[0][None, :], batch, seq, 1.0 - lam_init)
    wrh, wrm, br = _prep_router(moe_w_group[0], moe_b_group[0], moe_w_route[0], moe_b_route[0])
    row = lambda w: pl.BlockSpec((TM_ROW, w), lambda i: (i, 0))
    x1, ri, rg, cnt = _outproj_ln_route(_outproj_ab_kernel, (o_diff, o_mla), (row(512), row(512)),
                                        w_out_ab[0].astype(BF16), xs, ln_mix_g[0][None, :], ln_mix_b[0][None, :],
                                        wrh, wrm, br)
    xs = _moe_fused(x1, ri, rg, cnt, moe_w_gate, moe_w_up, moe_w_down,0, ln_ffn_g[0][None, :], ln_ffn_b[0][None, :])

    dil_w = DIL_HEADS * DIL_HEAD_DIM
    hqs = _proj_c(xs, w_in_c[0].astype(BF16), batch, seq)
    nb = seq // TM_ROW
    outs, lses, o_specs, l_specs = [], [], [], []
    for (window, dil), hq in zip(DIL_PATTERNS, hqs):
        assert window // (2 * dil) == BAND_HALO
        sub = seq // dil
        o_p, lse_p = _band_attention(hq, _band_table(dil), batch * dil, sub)
        outs.append(o_p.reshape(batch, dil, sub, dil_w))
        lses.append(lse_p.reshape(batch, dil, sub, LANES))
        o_specs.append(pl.BlockSpec((1, dil, TM_ROW // dil, dil_w), lambda i: (i // nb, 0, i % nb, 0)))
        l_specs.append(pl.BlockSpec((1, dil, TM_ROW // dil, LANES), lambda i: (i // nb, 0, i % nb, 0)))
    wrh, wrm, br = _prep_router(moe_w_group[1], moe_b_group[1], moe_w_route[1], moe_b_route[1])
    n_pat = len(DIL_PATTERNS)
    x1, ri, rg, cnt = _outproj_ln_route(_outproj_c_kernel, (*outs, *lses), (*o_specs, *l_specs),
                                        w_out_c[0].astype(BF16), xs, ln_mix_g[1][None, :], ln_mix_b[1][None, :],
                                        wrh, wrm, br,
                                        scratch=(pltpu.VMEM((n_pat, DIL_HEADS // 2, TM_ROW, LANES), F32),
                                                 pltpu.VMEM((n_pat, TM_ROW, LANES), F32)))
    xs = _moe_fused(x1, ri, rg, cnt, moe_w_gate, moe_w_up, moe_w_down,1, ln_ffn_g[1][None, :], ln_ffn_b[1][None, :])
    return xs.reshape(batch, seq, d)
```

```python
import functools
import math

import jax
import jax.numpy as jnp
from jax import lax
from jax.experimental import pallas as pl
from jax.experimental.pallas import tpu as pltpu

D_MODEL = 1024
DEPTH = 2
DA_HEADS = 4
DA_HEAD_DIM = 64
DA_WIDTH = DA_HEADS * 2 * DA_HEAD_DIM
MLA_HEADS = 4
MLA_Q_RANK = 256
MLA_KV_RANK = 128
MLA_NOPE_DIM = 64
MLA_ROPE_DIM = 32
MLA_V_DIM = 128
ROPE_THETA = 10000.0
DIL_HEADS = 16
DIL_HEAD_DIM = 64
DIL_PATTERNS = ((128, 1), (512, 4), (2048, 16))
MOE_GROUPS = 4
MOE_EXPERTS_PER_GROUP = 8
MOE_EXPERTS = MOE_GROUPS * MOE_EXPERTS_PER_GROUP
MOE_TOP_K = 2
MOE_HIDDEN = 512
MOE_BLOCK = 256
DEEPNORM_ALPHA = (2 * DEPTH) ** 0.25
NORM_EPS = 1e-5

LANES = 128
BF16 = jnp.bfloat16
F32 = jnp.float32
NEG_INF = float("-inf")
LOG2E = math.log2(math.e)

TM_PROJ = 512
TM_ROW = 512
TM_DMA = 256
TQ_ATTN = 256
MLA_ROW_GROUPS = 2
MLA_GROUP_ROWS = 256
TQ_BAND = 256
BAND_HALO = 64
BAND_SUB = 128
BAND_INTERIOR, BAND_FIRST, BAND_LAST = 0, 1, 2
VMEM_LIMIT = 56 * 1024 * 1024


def _cparams(*sem):
    return pltpu.CompilerParams(dimension_semantics=sem, vmem_limit_bytes=VMEM_LIMIT)


def _dot(a, b):
    return jnp.dot(a, b, preferred_element_type=F32)


def _dot_nt(a, b):
    return lax.dot_general(a, b, (((1,), (1,)), ((), ())), preferred_element_type=F32)


def _rms(x, g):
    return x * lax.rsqrt(jnp.mean(x * x, axis=-1, keepdims=True) + NORM_EPS) * g


def _layer_norm(y, g, b):
    mu = jnp.mean(y, axis=-1, keepdims=True)
    yc = y - mu
    var = jnp.mean(yc * yc, axis=-1, keepdims=True)
    return yc * lax.rsqrt(var + NORM_EPS) * g + b


def _proj_ab_kernel(x_ref, wa_ref, wc_ref, gq_ref, wuq_ref, gkv_ref, wukv_ref, cq_ref, sq_ref, ck_ref, sk_ref,
                    o_ref):
    xb = x_ref[...].astype(BF16)
    ha = _dot(xb, wa_ref[...])
    o_ref[:, 0:DA_WIDTH] = (ha[:, 0:DA_WIDTH] * (DA_HEAD_DIM ** -0.5 * LOG2E)).astype(BF16)
    o_ref[:, DA_WIDTH:3 * DA_WIDTH] = ha[:, DA_WIDTH:3 * DA_WIDTH].astype(BF16)
    hc = _dot(xb, wc_ref[...])
    c_q = hc[:, 0:MLA_Q_RANK]
    c_kv = hc[:, MLA_Q_RANK:MLA_Q_RANK + MLA_KV_RANK]
    kr = hc[:, 384:512]
    kr_rot = hc[:, 512:640]
    q2 = _dot(_rms(c_q, gq_ref[...]).astype(BF16), wuq_ref[...])
    kv2 = _dot(_rms(c_kv, gkv_ref[...]).astype(BF16), wukv_ref[...])
    cq, sq, ck, sk = cq_ref[...], sq_ref[...], ck_ref[...], sk_ref[...]
    k_rope = kr * ck + kr_rot * sk
    base = 3 * DA_WIDTH
    for h in range(MLA_HEADS):
        lo, hi = h * LANES, (h + 1) * LANES
        qh = q2[:, lo:hi] * cq + q2[:, 512 + lo:512 + hi] * sq
        o_ref[:, base + lo:base + hi] = qh.astype(BF16)
        o_ref[:, base + 512 + lo:base + 512 + hi] = (kv2[:, lo:hi] + k_rope).astype(BF16)
    o_ref[:, base + 1024:base + 1536] = kv2[:, 512:1024].astype(BF16)


def _proj_ab(x2d, wa, wc, gq, wuq, gkv, wukv, cq, sq, ck, sk, seq):
    t = x2d.shape[0]
    tm = TM_PROJ
    nsb = seq // tm
    full = lambda a: pl.BlockSpec(a.shape, lambda i: (0,) * a.ndim)
    tab = pl.BlockSpec((tm, LANES), lambda i: (i % nsb, 0))
    return pl.pallas_call(
        _proj_ab_kernel,
        grid=(t // tm,),
        in_specs=[pl.BlockSpec((tm, D_MODEL), lambda i: (i, 0)), full(wa), full(wc), full(gq), full(wuq),
                  full(gkv), full(wukv), tab, tab, tab, tab],
        out_specs=pl.BlockSpec((tm, 3072), lambda i: (i, 0)),
        out_shape=jax.ShapeDtypeStruct((t, 3072), BF16),
        compiler_params=_cparams("parallel"),
        name="proj_ab",
    )(x2d, wa, wc, gq, wuq, gkv, wukv, cq, sq, ck, sk)


def _diff_attn_kernel(lam_ref, q_ref, k_ref, v_ref, g_ref, u_ref, o_ref, *, tq, seq, out_scale):
    i = pl.program_id(2)
    lam = lam_ref[0]
    q = q_ref[...]
    lane = lax.broadcasted_iota(jnp.int32, q.shape, 1)
    zero = jnp.zeros_like(q)
    q1 = jnp.where(lane < DA_HEAD_DIM, q, zero)
    q2 = jnp.where(lane >= DA_HEAD_DIM, q, zero)
    k = k_ref[...]
    bias = u_ref[0, :, pl.ds(pl.multiple_of(seq - tq - i * tq, tq), seq)]

    v = v_ref[...]

    def softmax_pv(qm):
        s = _dot_nt(qm, k) + bias
        p = jnp.exp2(s - jnp.max(s, axis=-1, keepdims=True))
        l = jnp.sum(p, axis=-1, keepdims=True)
        return _dot(p.astype(BF16), v) * (1.0 / l)

    o = softmax_pv(q1) - lam * softmax_pv(q2)
    o_ref[...] = (_rms(o, g_ref[...]) * out_scale).astype(BF16)


def _mla_attn_kernel(q_ref, k_ref, v_ref, o_ref, *, tq):
    k = k_ref[...]
    v = v_ref[...]
    for r in range(0, q_ref.shape[0], tq):
        s = _dot_nt(q_ref[r:r + tq, :], k)
        p = jnp.exp2(s - jnp.max(s, axis=-1, keepdims=True))
        l = jnp.sum(p, axis=-1, keepdims=True)
        o_ref[r:r + tq, :] = (_dot(p.astype(BF16), v) * (1.0 / l)).astype(BF16)


def _alibi_table(slopes, tq, seq):
    r = jnp.arange(tq, dtype=jnp.int32)[:, None]
    m = jnp.arange(2 * seq, dtype=jnp.int32)[None, :]
    dist = jnp.abs(r + (seq - tq) - m).astype(F32)
    return -slopes[:, None, None] * dist[None]


def _attention_ab(hcat, lam, slopes, diff_g, batch, seq, out_scale):
    t = hcat.shape[0]
    tq = TQ_ATTN
    nq = seq // tq
    table = _alibi_table(slopes, tq, seq)

    o_diff = pl.pallas_call(
        functools.partial(_diff_attn_kernel, tq=tq, seq=seq, out_scale=out_scale),
        grid_spec=pltpu.PrefetchScalarGridSpec(
            num_scalar_prefetch=1, grid=(DA_HEADS, batch, nq),
            in_specs=[pl.BlockSpec((tq, LANES), lambda h, b, i, *_: (b * nq + i, h)),
                      pl.BlockSpec((seq, LANES), lambda h, b, i, *_: (b, 4 + h)),
                      pl.BlockSpec((seq, LANES), lambda h, b, i, *_: (b, 8 + h)),
                      pl.BlockSpec((1, LANES), lambda h, b, i, *_: (0, 0)),
                      pl.BlockSpec((1, tq, 2 * seq), lambda h, b, i, *_: (h, 0, 0), pipeline_mode=pl.Buffered(1))],
            out_specs=pl.BlockSpec((tq, LANES), lambda h, b, i, *_: (b * nq + i, h))),
        out_shape=jax.ShapeDtypeStruct((t, DA_WIDTH), BF16),
        compiler_params=_cparams("parallel", "parallel", "parallel"),
        name="diff_attn",
    )(lam, hcat, hcat, hcat, diff_g, table)
    rows = MLA_ROW_GROUPS * MLA_GROUP_ROWS
    nr = seq // rows
    o_mla = pl.pallas_call(
        functools.partial(_mla_attn_kernel, tq=MLA_GROUP_ROWS),
        grid=(batch, MLA_HEADS, nr),
        in_specs=[pl.BlockSpec((rows, LANES), lambda b, h, i: (b * nr + i, 12 + h)),
                  pl.BlockSpec((seq, LANES), lambda b, h, i: (b, 16 + h)),
                  pl.BlockSpec((seq, LANES), lambda b, h, i: (b, 20 + h))],
        out_specs=pl.BlockSpec((rows, LANES), lambda b, h, i: (b * nr + i, h)),
        out_shape=jax.ShapeDtypeStruct((t, MLA_HEADS * MLA_V_DIM), BF16),
        compiler_params=_cparams("parallel", "parallel", "parallel"),
        name="mla_attn",
    )(hcat, hcat, hcat)
    return o_diff, o_mla


def _proj_c_kernel(x_ref, w_ref, *refs):
    (o1_ref, o4_ref, o16_ref), (h_scr, h4_scr) = refs[:3], refs[3:]
    assert [d for _, d in DIL_PATTERNS] == [1, 4, 16]
    tm = x_ref.shape[0]
    xb = x_ref[...].astype(BF16)
    for n in range(3):
        cols = slice(n * 1024, (n + 1) * 1024)
        h = _dot(xb, w_ref[:, cols])
        if n == 0:
            h = h * (DIL_HEAD_DIM ** -0.5 * LOG2E)
        o1_ref[0, 0, :, cols] = h.astype(BF16)
        for c in range(1024 // LANES):
            lanes = slice(cols.start + c * LANES, cols.start + (c + 1) * LANES)
            h_scr[...] = h[:, c * LANES:(c + 1) * LANES]
            for r4 in range(4):
                piece = h_scr[pl.ds(r4, tm // 4, stride=4), :]
                o4_ref[0, r4, :, lanes] = piece.astype(BF16)
                h4_scr[r4] = piece
            for r in range(16):
                o16_ref[0, r, :, lanes] = h4_scr[r % 4, pl.ds(r // 4, tm // 16, stride=4), :].astype(BF16)


def _proj_c(x2d, w, batch, seq):
    tm = TM_PROJ
    nb = seq // tm
    outs = pl.pallas_call(
        _proj_c_kernel,
        grid=(batch, nb),
        in_specs=[pl.BlockSpec((tm, D_MODEL), lambda b, i: (b * nb + i, 0)), pl.BlockSpec(w.shape, lambda b, i: (0, 0))],
        out_specs=[pl.BlockSpec((1, dil, tm // dil, 3072), lambda b, i: (b, 0, i, 0)) for _, dil in DIL_PATTERNS],
        out_shape=[jax.ShapeDtypeStruct((batch, dil, seq // dil, 3072), BF16) for _, dil in DIL_PATTERNS],
        scratch_shapes=[pltpu.VMEM((tm, LANES), F32), pltpu.VMEM((4, tm // 4, LANES), F32)],
        compiler_params=_cparams("parallel", "parallel"),
        name="proj_c",
    )(x2d, w)
    return [o.reshape(batch * seq, 3072) for o in outs]


def _band_attn_kernel(q_ref, kp_ref, kc_ref, kn_ref, vp_ref, vc_ref, vn_ref, band_ref, o_ref, lse_ref,
                      *, tq, sub_len):
    i = pl.program_id(1)
    n_sub = tq // BAND_SUB
    assert n_sub >= 2
    win = BAND_SUB + 2 * BAND_HALO
    lane = lax.broadcasted_iota(jnp.int32, (BAND_SUB, LANES), 1)
    first_half = lane < DIL_HEAD_DIM
    variant = [BAND_INTERIOR] * n_sub
    variant[0] = jnp.where(i == 0, BAND_FIRST, BAND_INTERIOR)
    variant[-1] = jnp.where(i == sub_len // tq - 1, BAND_LAST, BAND_INTERIOR)
    for pair in range(DIL_HEADS // 2):
        cs = slice(pair * LANES, (pair + 1) * LANES)
        kcat = jnp.concatenate([kp_ref[:, cs], kc_ref[:, cs], kn_ref[:, cs]], axis=0)
        vcat = jnp.concatenate([vp_ref[:, cs], vc_ref[:, cs], vn_ref[:, cs]], axis=0)
        for sb in range(n_sub):
            r0 = sb * BAND_SUB
            qp = q_ref[r0:r0 + BAND_SUB, cs]
            kw = kcat[r0:r0 + win]
            vw = vcat[r0:r0 + win]
            outs, ms, ls = [], [], []
            for half in range(2):
                hd = 2 * pair + half
                qh = jnp.where(first_half if half == 0 else jnp.logical_not(first_half), qp, jnp.zeros_like(qp))
                s = _dot_nt(qh, kw) + band_ref[variant[sb], hd]
                m = jnp.max(s, axis=-1, keepdims=True)
                p = jnp.exp2(s - m)
                ms.append(m)
                ls.append(jnp.sum(p, axis=-1, keepdims=True))
                outs.append(_dot(p.astype(BF16), vw))
            o_ref[r0:r0 + BAND_SUB, cs] = jnp.where(first_half, outs[0], outs[1]).astype(BF16)
            st = lse_ref[r0:r0 + BAND_SUB, :] if pair > 0 else jnp.zeros((BAND_SUB, LANES), F32)
            for half in range(2):
                st = jnp.where(lane == 2 * pair + half, ms[half], st)
                st = jnp.where(lane == DIL_HEADS + 2 * pair + half, ls[half], st)
            lse_ref[r0:r0 + BAND_SUB, :] = st


def _band_attention(hq, band, n_seq, sub_len):
    t = hq.shape[0]
    tq = TQ_BAND
    nq = sub_len // tq
    hb = tq // BAND_HALO

    def cur(col):
        return pl.BlockSpec((tq, 1024), lambda s, i: (s * nq + i, col))

    def prv(col):
        return pl.BlockSpec((BAND_HALO, 1024), lambda s, i: (jnp.maximum((s * nq + i) * hb - 1, s * nq * hb), col))

    def nxt(col):
        return pl.BlockSpec((BAND_HALO, 1024),
                            lambda s, i: (jnp.minimum((s * nq + i + 1) * hb, (s + 1) * nq * hb - 1), col))

    return pl.pallas_call(
        functools.partial(_band_attn_kernel, tq=tq, sub_len=sub_len),
        grid=(n_seq, nq),
        in_specs=[cur(0), prv(1), cur(1), nxt(1), prv(2), cur(2), nxt(2),
                  pl.BlockSpec(band.shape, lambda s, i: (0, 0, 0, 0), pipeline_mode=pl.Buffered(1))],
        out_specs=[pl.BlockSpec((tq, 1024), lambda s, i: (s * nq + i, 0)),
                   pl.BlockSpec((tq, LANES), lambda s, i: (s * nq + i, 0))],
        out_shape=[jax.ShapeDtypeStruct((t, 1024), BF16), jax.ShapeDtypeStruct((t, LANES), F32)],
        compiler_params=_cparams("parallel", "parallel"),
        name="band_attn",
    )(hq, hq, hq, hq, hq, hq, hq, band)


ROUTE_FINE0 = MOE_GROUPS
ROUTE_FINE1 = MOE_GROUPS + MOE_EXPERTS


def _route_and_rank(x1, wrh_ref, wrm_ref, br_ref, ri_ref, rg_ref, cnt_ref):
    tm = x1.shape[0]
    hi = x1.astype(BF16)
    mid = (x1 - hi.astype(F32)).astype(BF16)
    logits = _dot(hi, wrh_ref[...]) + (_dot(hi, wrm_ref[...]) + _dot(mid, wrh_ref[...])) + br_ref[...]
    lane = lax.broadcasted_iota(jnp.int32, (tm, LANES), 1)
    lane_f = lane.astype(F32)

    def first_lane(mask):
        return jnp.min(jnp.where(mask, lane_f, float(LANES)), axis=-1, keepdims=True).astype(jnp.int32)

    coarse = jnp.where(lane < MOE_GROUPS, logits, NEG_INF)
    cmax = jnp.max(coarse, axis=-1, keepdims=True)
    gsel = first_lane(coarse == cmax)
    p_group = 1.0 / jnp.sum(jnp.exp(coarse - cmax), axis=-1, keepdims=True)
    in_group = ((lane >= ROUTE_FINE0) & (lane < ROUTE_FINE1)
                & (jnp.right_shift(lane - ROUTE_FINE0, 3) == gsel))
    fine = jnp.where(in_group, logits, NEG_INF)
    v1 = jnp.max(fine, axis=-1, keepdims=True)
    i1 = first_lane(fine == v1)
    fine2 = jnp.where(lane == i1, NEG_INF, fine)
    v2 = jnp.max(fine2, axis=-1, keepdims=True)
    i2 = first_lane(fine2 == v2)
    e2 = jnp.exp(v2 - v1)
    den = 1.0 + e2
    g1 = p_group * (1.0 / den)
    g2 = p_group * (e2 / den)
    sel1 = lane == i1
    sel2 = lane == i2
    onehot = jnp.where(sel1 | sel2, 1.0, 0.0)
    row = lax.broadcasted_iota(jnp.int32, (tm, tm), 0)
    col = lax.broadcasted_iota(jnp.int32, (tm, tm), 1)
    ltri = jnp.where(col < row, 1.0, 0.0).astype(BF16)
    before = _dot(ltri, onehot.astype(BF16)) + cnt_ref[...]
    rank1 = jnp.sum(jnp.where(sel1, before, 0.0), axis=-1, keepdims=True).astype(jnp.int32)
    rank2 = jnp.sum(jnp.where(sel2, before, 0.0), axis=-1, keepdims=True).astype(jnp.int32)
    cnt_ref[...] = cnt_ref[...] + jnp.sum(onehot, axis=0, keepdims=True)
    zi = jnp.zeros((tm, LANES), jnp.int32)
    ri_ref[...] = jnp.where(lane == 0, i1 - ROUTE_FINE0,
                            jnp.where(lane == 1, i2 - ROUTE_FINE0,
                                      jnp.where(lane == 2, rank1, jnp.where(lane == 3, rank2, zi))))
    rg_ref[...] = jnp.where(lane == 0, g1, jnp.where(lane == 1, g2, jnp.zeros((tm, LANES), F32)))


def _outproj_ab_kernel(oa_ref, ob_ref, wo_ref, x_ref, g_ref, b_ref, wrh_ref, wrm_ref, br_ref,
                       x1_ref, ri_ref, rg_ref, cnt_ref):
    @pl.when(pl.program_id(0) == 0)
    def _():
        cnt_ref[...] = jnp.zeros_like(cnt_ref)

    mixed = _dot(oa_ref[...], wo_ref[0:512, :]) + _dot(ob_ref[...], wo_ref[512:1024, :])
    x1 = _layer_norm(DEEPNORM_ALPHA * x_ref[...] + mixed, g_ref[...], b_ref[...])
    x1_ref[...] = x1
    _route_and_rank(x1, wrh_ref, wrm_ref, br_ref, ri_ref, rg_ref, cnt_ref)


def _outproj_c_kernel(o1_ref, o2_ref, o3_ref, l1_ref, l2_ref, l3_ref, wo_ref, x_ref, g_ref, b_ref,
                      wrh_ref, wrm_ref, br_ref, x1_ref, ri_ref, rg_ref, cnt_ref, o_nat, l_nat):
    @pl.when(pl.program_id(0) == 0)
    def _():
        cnt_ref[...] = jnp.zeros_like(cnt_ref)

    tm = x_ref.shape[0]
    for p, ((_, dil), o_ref, l_ref) in enumerate(zip(DIL_PATTERNS, (o1_ref, o2_ref, o3_ref), (l1_ref, l2_ref, l3_ref))):
        for r in range(dil):
            rows = pl.ds(r, tm // dil, stride=dil)
            for c in range(DIL_HEADS // 2):
                o_nat[p, c, rows, :] = o_ref[0, r, :, c * LANES:(c + 1) * LANES].astype(F32)
            l_nat[p, rows, :] = l_ref[0, r]
    stats = [l_nat[p] for p in range(len(DIL_PATTERNS))]
    m_max = functools.reduce(jnp.maximum, stats)
    es = [jnp.exp2(st - m_max) for st in stats]
    den = sum(e * pltpu.roll(st, LANES - DIL_HEADS, 1) for e, st in zip(es, stats))
    inv = 1.0 / den
    w1, w2, w3 = [e * inv for e in es]
    first_half = lax.broadcasted_iota(jnp.int32, (tm, LANES), 1) < DIL_HEAD_DIM
    mixed = jnp.zeros((tm, D_MODEL), F32)
    for pair in range(DIL_HEADS // 2):
        cs = slice(pair * LANES, (pair + 1) * LANES)
        ha, hb = 2 * pair, 2 * pair + 1
        om = None
        for p, w in enumerate((w1, w2, w3)):
            wf = jnp.where(first_half, w[:, ha:ha + 1], w[:, hb:hb + 1])
            term = wf * o_nat[p, pair]
            om = term if om is None else om + term
        mixed = mixed + _dot(om.astype(BF16), wo_ref[cs, :])
    x1 = _layer_norm(DEEPNORM_ALPHA * x_ref[...] + mixed, g_ref[...], b_ref[...])
    x1_ref[...] = x1
    _route_and_rank(x1, wrh_ref, wrm_ref, br_ref, ri_ref, rg_ref, cnt_ref)


def _outproj_ln_route(kernel_fn, mix_inputs, mix_specs, wo, x2d, g, b, wrh, wrm, br, scratch=()):
    t = x2d.shape[0]
    tm = TM_ROW
    full = lambda a: pl.BlockSpec(a.shape, lambda i: (0,) * a.ndim)
    row = lambda w: pl.BlockSpec((tm, w), lambda i: (i, 0))
    return pl.pallas_call(
        kernel_fn,
        grid=(t // tm,),
        in_specs=list(mix_specs) + [full(wo), row(D_MODEL), full(g), full(b), full(wrh), full(wrm), full(br)],
        out_specs=[row(D_MODEL), row(LANES), row(LANES), pl.BlockSpec((1, LANES), lambda i: (0, 0))],
        out_shape=[jax.ShapeDtypeStruct((t, D_MODEL), F32), jax.ShapeDtypeStruct((t, LANES), jnp.int32),
                   jax.ShapeDtypeStruct((t, LANES), F32), jax.ShapeDtypeStruct((1, LANES), F32)],
        scratch_shapes=list(scratch),
        compiler_params=_cparams("arbitrary"),
        name=kernel_fn.__name__.strip("_"),
    )(*mix_inputs, wo, x2d, g, b, wrh, wrm, br)


DMA_UNROLL = 8


def _dispatch_kernel(pad_end_ref, dest_ref, x_ref, xs_ref, zbuf, sem, zsem):
    tm = x_ref.shape[0]
    n_blk = xs_ref.shape[0] // MOE_BLOCK

    @pl.when(pl.program_id(0) == 0)
    def _():
        zbuf[...] = jnp.zeros_like(zbuf)
        n_used = pad_end_ref[MOE_EXPERTS - 1] // MOE_BLOCK

        def zero_copy(row0):
            return pltpu.make_async_copy(zbuf, xs_ref.at[pl.ds(pl.multiple_of(row0, MOE_BLOCK), MOE_BLOCK)], zsem)

        def has_block(e):
            return pad_end_ref[e] > (pad_end_ref[e - 1] if e > 0 else 0)

        def tail_start(b, c):
            zero_copy(b * MOE_BLOCK).start()
            return c

        def tail_wait(b, c):
            zero_copy(b * MOE_BLOCK).wait()
            return c

        for e in range(MOE_EXPERTS):
            @pl.when(has_block(e))
            def _():
                zero_copy(pad_end_ref[e] - MOE_BLOCK).start()
        lax.fori_loop(n_used, n_blk, tail_start, 0)
        for e in range(MOE_EXPERTS):
            @pl.when(has_block(e))
            def _():
                zero_copy(pad_end_ref[e] - MOE_BLOCK).wait()
        lax.fori_loop(n_used, n_blk, tail_wait, 0)

    def issue(r, c):
        for k in range(MOE_TOP_K):
            pltpu.make_async_copy(x_ref.at[pl.ds(r, 1)], xs_ref.at[pl.ds(dest_ref[0, 0, 2 * r + k], 1)], sem).start()
        return c

    lax.fori_loop(0, tm, issue, 0, unroll=DMA_UNROLL)
    for _ in range(MOE_TOP_K * tm):
        pltpu.make_async_copy(x_ref.at[pl.ds(0, 1)], xs_ref.at[pl.ds(0, 1)], sem).wait()


def _dispatch(pad_end, dest3, x1, n_slot):
    t = x1.shape[0]
    tm = TM_DMA
    return pl.pallas_call(
        _dispatch_kernel,
        grid_spec=pltpu.PrefetchScalarGridSpec(
            num_scalar_prefetch=1, grid=(t // tm,),
            in_specs=[pl.BlockSpec((1, 1, 2 * tm), lambda i, pe: (i, 0, 0), memory_space=pltpu.SMEM),
                      pl.BlockSpec((tm, D_MODEL), lambda i, pe: (i, 0))],
            out_specs=pl.BlockSpec(memory_space=pl.ANY),
            scratch_shapes=[pltpu.VMEM((MOE_BLOCK, D_MODEL), F32), pltpu.SemaphoreType.DMA(()),
                            pltpu.SemaphoreType.DMA(())]),
        out_shape=jax.ShapeDtypeStruct((n_slot, D_MODEL), F32),
        compiler_params=_cparams("arbitrary"),
        name="moe_dispatch",
    )(pad_end, dest3, x1)


def _expert_kernel(eid_ref, nused_ref, xs_ref, wg_ref, wu_ref, wd_ref, ys_ref, wgu_bf, wd_bf):
    i = pl.program_id(0)

    @pl.when((i == 0) | (eid_ref[i] != eid_ref[jnp.maximum(i - 1, 0)]))
    def _():
        wgu_bf[:, 0:MOE_HIDDEN] = wg_ref[0, 0].astype(BF16)
        wgu_bf[:, MOE_HIDDEN:2 * MOE_HIDDEN] = wu_ref[0, 0].astype(BF16)
        wd_bf[...] = wd_ref[0, 0].astype(BF16)

    @pl.when(i < nused_ref[0])
    def _():
        xb = xs_ref[...].astype(BF16)
        gu = _dot(xb, wgu_bf[...])
        gate, up = gu[:, 0:MOE_HIDDEN], gu[:, MOE_HIDDEN:2 * MOE_HIDDEN]
        hidden = gate * (1.0 / (1.0 + jnp.exp(-gate))) * up
        ys_ref[...] = _dot(hidden.astype(BF16), wd_bf[...])

    @pl.when(i >= nused_ref[0])
    def _():
        ys_ref[...] = jnp.zeros_like(ys_ref)


def _expert_mlp(blk_eid, n_used, xs, w_gate, w_up, w_down, layer):
    n_slot = xs.shape[0]
    n_blk = n_slot // MOE_BLOCK
    wspec = lambda k, n: pl.BlockSpec((1, 1, k, n), lambda i, e, u: (layer, e[i], 0, 0))
    return pl.pallas_call(
        _expert_kernel,
        grid_spec=pltpu.PrefetchScalarGridSpec(
            num_scalar_prefetch=2, grid=(n_blk,),
            in_specs=[pl.BlockSpec((MOE_BLOCK, D_MODEL), lambda i, e, u: (i, 0)),
                      wspec(D_MODEL, MOE_HIDDEN), wspec(D_MODEL, MOE_HIDDEN), wspec(MOE_HIDDEN, D_MODEL)],
            out_specs=pl.BlockSpec((MOE_BLOCK, D_MODEL), lambda i, e, u: (i, 0)),
            scratch_shapes=[pltpu.VMEM((D_MODEL, 2 * MOE_HIDDEN), BF16), pltpu.VMEM((MOE_HIDDEN, D_MODEL), BF16)]),
        out_shape=jax.ShapeDtypeStruct((n_slot, D_MODEL), F32),
        compiler_params=_cparams("arbitrary"),
        name="expert_mlp",
    )(blk_eid, n_used, xs, w_gate, w_up, w_down)


def _combine_kernel(dest_ref, rg_ref, x_ref, g_ref, b_ref, ys_ref, o_ref, buf, sem):
    tm = x_ref.shape[0]

    def issue(r, c):
        for k in range(MOE_TOP_K):
            pltpu.make_async_copy(ys_ref.at[pl.ds(dest_ref[0, 0, 2 * r + k], 1)], buf.at[k, pl.ds(r, 1)], sem).start()
        return c

    lax.fori_loop(0, tm, issue, 0, unroll=DMA_UNROLL)
    for _ in range(MOE_TOP_K * tm):
        pltpu.make_async_copy(ys_ref.at[pl.ds(0, 1)], buf.at[0, pl.ds(0, 1)], sem).wait()
    rg = rg_ref[...]
    ffn = buf[0] * rg[:, 0:1] + buf[1] * rg[:, 1:2]
    o_ref[...] = _layer_norm(DEEPNORM_ALPHA * x_ref[...] + ffn, g_ref[...], b_ref[...])


def _combine_ln(dest3, rg, x1, g, b, ys):
    t = x1.shape[0]
    tm = TM_DMA
    row = lambda w: pl.BlockSpec((tm, w), lambda i: (i, 0))
    full = lambda a: pl.BlockSpec(a.shape, lambda i: (0,) * a.ndim)
    return pl.pallas_call(
        _combine_kernel,
        grid=(t // tm,),
        in_specs=[pl.BlockSpec((1, 1, 2 * tm), lambda i: (i, 0, 0), memory_space=pltpu.SMEM),
                  row(LANES), row(D_MODEL), full(g), full(b), pl.BlockSpec(memory_space=pl.ANY)],
        out_specs=row(D_MODEL),
        out_shape=jax.ShapeDtypeStruct((t, D_MODEL), F32),
        scratch_shapes=[pltpu.VMEM((2, tm, D_MODEL), F32), pltpu.SemaphoreType.DMA(())],
        compiler_params=_cparams("arbitrary"),
        name="moe_combine",
    )(dest3, rg, x1, g, b, ys)


def _moe(x1, ri, rg, cnt, w_gate, w_up, w_down, layer, g, b):
    t = x1.shape[0]
    n_slot = t * MOE_TOP_K + MOE_EXPERTS * MOE_BLOCK
    n_blk = n_slot // MOE_BLOCK
    counts = cnt[0, ROUTE_FINE0:ROUTE_FINE1].astype(jnp.int32)
    padded = (counts + MOE_BLOCK - 1) // MOE_BLOCK * MOE_BLOCK
    pad_end = jnp.cumsum(padded).astype(jnp.int32)
    pad_start = pad_end - padded
    onehot = ri[:, 0:2, None] == jnp.arange(MOE_EXPERTS, dtype=jnp.int32)
    dest = jnp.sum(jnp.where(onehot, pad_start, 0), axis=-1) + ri[:, 2:4]
    dest3 = dest.reshape(t // TM_DMA, 1, 2 * TM_DMA)
    blk_start = jnp.arange(n_blk, dtype=jnp.int32) * MOE_BLOCK
    blk_eid = jnp.minimum(jnp.sum((pad_end[None, :] <= blk_start[:, None]).astype(jnp.int32), axis=1),
                          MOE_EXPERTS - 1)
    n_used = pad_end[MOE_EXPERTS - 1:] // MOE_BLOCK
    xs = _dispatch(pad_end, dest3, x1, n_slot)
    ys = _expert_mlp(blk_eid, n_used, xs, w_gate, w_up, w_down, layer)
    return _combine_ln(dest3, rg, x1, g, b, ys)


def _rotate_half_cols(w):
    half = MLA_ROPE_DIM // 2
    return jnp.concatenate([-w[..., half:], w[..., :half]], axis=-1)


def _prep_ab(w_in, w_uq, w_ukv):
    wa = w_in[:, 0:3 * DA_WIDTH]
    c0 = 3 * DA_WIDTH
    w_cq = w_in[:, c0:c0 + MLA_Q_RANK]
    w_ckv = w_in[:, c0 + MLA_Q_RANK:c0 + MLA_Q_RANK + MLA_KV_RANK]
    w_kr = w_in[:, c0 + MLA_Q_RANK + MLA_KV_RANK:]
    z64 = jnp.zeros((D_MODEL, 64), F32)
    z32 = jnp.zeros((D_MODEL, 32), F32)
    wc = jnp.concatenate([w_cq, w_ckv, z64, w_kr, z32, z64, _rotate_half_cols(w_kr), z32], axis=1)
    qd = MLA_NOPE_DIM + MLA_ROPE_DIM
    wq = w_uq.reshape(MLA_Q_RANK, MLA_HEADS, qd)
    zq = jnp.zeros((MLA_Q_RANK, MLA_HEADS, LANES - qd), F32)
    plain = jnp.concatenate([wq, zq], axis=-1)
    rot = jnp.concatenate([jnp.zeros_like(wq[..., :MLA_NOPE_DIM]), _rotate_half_cols(wq[..., MLA_NOPE_DIM:]), zq],
                          axis=-1)
    wuq = jnp.concatenate([plain.reshape(MLA_Q_RANK, -1), rot.reshape(MLA_Q_RANK, -1)], axis=1)
    wkv = w_ukv.reshape(MLA_KV_RANK, MLA_HEADS, MLA_NOPE_DIM + MLA_V_DIM)
    k_pad = jnp.concatenate([wkv[..., :MLA_NOPE_DIM], jnp.zeros((MLA_KV_RANK, MLA_HEADS, 64), F32)], axis=-1)
    wukv = jnp.concatenate([k_pad.reshape(MLA_KV_RANK, -1), wkv[..., MLA_NOPE_DIM:].reshape(MLA_KV_RANK, -1)],
                           axis=1)
    return wa.astype(BF16), wc.astype(BF16), wuq.astype(BF16), wukv.astype(BF16)


def _rope_tables(seq):
    half = MLA_ROPE_DIM // 2
    inv_freq = jnp.power(ROPE_THETA, -jnp.arange(half, dtype=F32) * 2.0 / MLA_ROPE_DIM)
    ang = jnp.arange(seq, dtype=F32)[:, None] * inv_freq[None, :]
    cos = jnp.concatenate([jnp.cos(ang)] * 2, axis=-1)
    sin = jnp.concatenate([jnp.sin(ang)] * 2, axis=-1)
    scale = (MLA_NOPE_DIM + MLA_ROPE_DIM) ** -0.5 * LOG2E
    ones = jnp.ones((seq, 64), F32)
    z64 = jnp.zeros((seq, 64), F32)
    z32 = jnp.zeros((seq, 32), F32)
    cq = jnp.concatenate([ones, cos, z32], axis=-1) * scale
    sq = jnp.concatenate([z64, sin, z32], axis=-1) * scale
    ck = jnp.concatenate([z64, cos, z32], axis=-1)
    sk = jnp.concatenate([z64, sin, z32], axis=-1)
    return cq, sq, ck, sk


def _alibi_slopes(n_heads):
    start = 2.0 ** (-8.0 / n_heads)
    return jnp.asarray([start ** (i + 1) for i in range(n_heads)], dtype=F32)


def _band_table(dilation):
    win = BAND_SUB + 2 * BAND_HALO
    kk = jnp.arange(win, dtype=jnp.int32)[None, :]
    rel = kk - BAND_HALO - jnp.arange(BAND_SUB, dtype=jnp.int32)[:, None]
    dist = (jnp.abs(rel) * dilation).astype(F32)
    bias = -(_alibi_slopes(DIL_HEADS) * LOG2E)[:, None, None] * dist[None]
    interior = jnp.where((jnp.abs(rel) <= BAND_HALO)[None], bias, NEG_INF)
    first = jnp.where((kk >= BAND_HALO)[None], interior, NEG_INF)
    last = jnp.where((kk < BAND_SUB + BAND_HALO)[None], interior, NEG_INF)
    return jnp.stack([interior, first, last])


def _prep_router(w_group, b_group, w_route, b_route):
    pad = LANES - MOE_GROUPS - MOE_EXPERTS
    w = jnp.concatenate([w_group, w_route, jnp.zeros((D_MODEL, pad), F32)], axis=1)
    hi = w.astype(BF16)
    mid = (w - hi.astype(F32)).astype(BF16)
    bias = jnp.concatenate([b_group, b_route, jnp.zeros((pad,), F32)])[None, :]
    return hi, mid, bias


def kernel(x, w_in_ab, lam_q1, lam_k1, lam_q2, lam_k2, diff_norm_g, mla_q_norm_g, w_uq, mla_kv_norm_g, w_ukv,
           w_out_ab, w_in_c, w_out_c, ln_mix_g, ln_mix_b, moe_w_group, moe_b_group, moe_w_route, moe_b_route,
           moe_w_gate, moe_w_up, moe_w_down, ln_ffn_g, ln_ffn_b):
    batch, seq, d = x.shape
    t = batch * seq
    xs = x.reshape(t, d)

    wa, wc, wuq, wukv = _prep_ab(w_in_ab[0], w_uq[0], w_ukv[0])
    cq, sq, ck, sk = _rope_tables(seq)
    hcat = _proj_ab(xs, wa, wc, mla_q_norm_g[0][None, :], wuq, mla_kv_norm_g[0][None, :], wukv, cq, sq, ck, sk, seq)
    lam_init = 0.8 - 0.6 * math.exp(-0.3 * 0)
    lam = (jnp.exp(jnp.sum(lam_q1[0] * lam_k1[0])) - jnp.exp(jnp.sum(lam_q2[0] * lam_k2[0])) + lam_init)
    o_diff, o_mla = _attention_ab(hcat, lam[None].astype(F32), _alibi_slopes(DA_HEADS) * LOG2E,
                                  diff_norm_g[0][None, :], batch, seq, 1.0 - lam_init)
    wrh, wrm, br = _prep_router(moe_w_group[0], moe_b_group[0], moe_w_route[0], moe_b_route[0])
    row = lambda w: pl.BlockSpec((TM_ROW, w), lambda i: (i, 0))
    x1, ri, rg, cnt = _outproj_ln_route(_outproj_ab_kernel, (o_diff, o_mla), (row(512), row(512)),
                                        w_out_ab[0].astype(BF16), xs, ln_mix_g[0][None, :], ln_mix_b[0][None, :],
                                        wrh, wrm, br)
    xs = _moe(x1, ri, rg, cnt, moe_w_gate, moe_w_up, moe_w_down, 0, ln_ffn_g[0][None, :], ln_ffn_b[0][None, :])

    dil_w = DIL_HEADS * DIL_HEAD_DIM
    hqs = _proj_c(xs, w_in_c[0].astype(BF16), batch, seq)
    nb = seq // TM_ROW
    outs, lses, o_specs, l_specs = [], [], [], []
    for (window, dil), hq in zip(DIL_PATTERNS, hqs):
        assert window // (2 * dil) == BAND_HALO
        sub = seq // dil
        o_p, lse_p = _band_attention(hq, _band_table(dil), batch * dil, sub)
        outs.append(o_p.reshape(batch, dil, sub, dil_w))
        lses.append(lse_p.reshape(batch, dil, sub, LANES))
        o_specs.append(pl.BlockSpec((1, dil, TM_ROW // dil, dil_w), lambda i: (i // nb, 0, i % nb, 0)))
        l_specs.append(pl.BlockSpec((1, dil, TM_ROW // dil, LANES), lambda i: (i // nb, 0, i % nb, 0)))
    wrh, wrm, br = _prep_router(moe_w_group[1], moe_b_group[1], moe_w_route[1], moe_b_route[1])
    n_pat = len(DIL_PATTERNS)
    x1, ri, rg, cnt = _outproj_ln_route(_outproj_c_kernel, (*outs, *lses), (*o_specs, *l_specs),
                                        w_out_c[0].astype(BF16), xs, ln_mix_g[1][None, :], ln_mix_b[1][None, :],
                                        wrh, wrm, br,
                                        scratch=(pltpu.VMEM((n_pat, DIL_HEADS // 2, TM_ROW, LANES), F32),
                                                 pltpu.VMEM((n_pat, TM_ROW, LANES), F32)))
    xs = _moe(x1, ri, rg, cnt, moe_w_gate, moe_w_up, moe_w_down, 1, ln_ffn_g[1][None, :], ln_ffn_b[1][None, :])
    return xs.reshape(batch, seq, d)
```

```python
import functools
import math

import jax
import jax.numpy as jnp
from jax import lax
from jax.experimental import pallas as pl
from jax.experimental.pallas import tpu as pltpu

D_MODEL = 1024
DEPTH = 2
DA_HEADS = 4
DA_HEAD_DIM = 64
DA_WIDTH = DA_HEADS * 2 * DA_HEAD_DIM
MLA_HEADS = 4
MLA_Q_RANK = 256
MLA_KV_RANK = 128
MLA_NOPE_DIM = 64
MLA_ROPE_DIM = 32
MLA_V_DIM = 128
ROPE_THETA = 10000.0
DIL_HEADS = 16
DIL_HEAD_DIM = 64
DIL_PATTERNS = ((128, 1), (512, 4), (2048, 16))
MOE_GROUPS = 4
MOE_EXPERTS_PER_GROUP = 8
MOE_EXPERTS = MOE_GROUPS * MOE_EXPERTS_PER_GROUP
MOE_TOP_K = 2
MOE_HIDDEN = 512
MOE_BLOCK = 512
DEEPNORM_ALPHA = (2 * DEPTH) ** 0.25
NORM_EPS = 1e-5

LANES = 128
BF16 = jnp.bfloat16
F32 = jnp.float32
NEG_INF = float("-inf")
LOG2E = math.log2(math.e)

TM_PROJ = 512
TM_ROW = 512
TM_DMA = 256
TQ_ATTN = 256
MLA_ROW_GROUPS = 2
MLA_GROUP_ROWS = 256
TQ_BAND = 256
BAND_HALO = 64
BAND_SUB = 128
BAND_INTERIOR, BAND_FIRST, BAND_LAST = 0, 1, 2
VMEM_LIMIT = 56 * 1024 * 1024


def _cparams(*sem):
    return pltpu.CompilerParams(dimension_semantics=sem, vmem_limit_bytes=VMEM_LIMIT)


def _dot(a, b):
    return jnp.dot(a, b, preferred_element_type=F32)


def _dot_nt(a, b):
    return lax.dot_general(a, b, (((1,), (1,)), ((), ())), preferred_element_type=F32)


def _rms(x, g):
    return x * lax.rsqrt(jnp.mean(x * x, axis=-1, keepdims=True) + NORM_EPS) * g


def _layer_norm(y, g, b):
    mu = jnp.mean(y, axis=-1, keepdims=True)
    yc = y - mu
    var = jnp.mean(yc * yc, axis=-1, keepdims=True)
    return yc * lax.rsqrt(var + NORM_EPS) * g + b


def _proj_ab_kernel(x_ref, wa_ref, wc_ref, gq_ref, wuq_ref, gkv_ref, wukv_ref, cq_ref, sq_ref, ck_ref, sk_ref,
                    o_ref):
    xb = x_ref[...].astype(BF16)
    ha = _dot(xb, wa_ref[...])
    o_ref[:, 0:DA_WIDTH] = (ha[:, 0:DA_WIDTH] * (DA_HEAD_DIM ** -0.5 * LOG2E)).astype(BF16)
    o_ref[:, DA_WIDTH:3 * DA_WIDTH] = ha[:, DA_WIDTH:3 * DA_WIDTH].astype(BF16)
    hc = _dot(xb, wc_ref[...])
    c_q = hc[:, 0:MLA_Q_RANK]
    c_kv = hc[:, MLA_Q_RANK:MLA_Q_RANK + MLA_KV_RANK]
    kr = hc[:, 384:512]
    kr_rot = hc[:, 512:640]
    q2 = _dot(_rms(c_q, gq_ref[...]).astype(BF16), wuq_ref[...])
    kv2 = _dot(_rms(c_kv, gkv_ref[...]).astype(BF16), wukv_ref[...])
    cq, sq, ck, sk = cq_ref[...], sq_ref[...], ck_ref[...], sk_ref[...]
    k_rope = kr * ck + kr_rot * sk
    base = 3 * DA_WIDTH
    for h in range(MLA_HEADS):
        lo, hi = h * LANES, (h + 1) * LANES
        qh = q2[:, lo:hi] * cq + q2[:, 512 + lo:512 + hi] * sq
        o_ref[:, base + lo:base + hi] = qh.astype(BF16)
        o_ref[:, base + 512 + lo:base + 512 + hi] = (kv2[:, lo:hi] + k_rope).astype(BF16)
    o_ref[:, base + 1024:base + 1536] = kv2[:, 512:1024].astype(BF16)


def _proj_ab(x2d, wa, wc, gq, wuq, gkv, wukv, cq, sq, ck, sk, seq):
    t = x2d.shape[0]
    tm = TM_PROJ
    nsb = seq // tm
    full = lambda a: pl.BlockSpec(a.shape, lambda i: (0,) * a.ndim)
    tab = pl.BlockSpec((tm, LANES), lambda i: (i % nsb, 0))
    return pl.pallas_call(
        _proj_ab_kernel,
        grid=(t // tm,),
        in_specs=[pl.BlockSpec((tm, D_MODEL), lambda i: (i, 0)), full(wa), full(wc), full(gq), full(wuq),
                  full(gkv), full(wukv), tab, tab, tab, tab],
        out_specs=pl.BlockSpec((tm, 3072), lambda i: (i, 0)),
        out_shape=jax.ShapeDtypeStruct((t, 3072), BF16),
        compiler_params=_cparams("parallel"),
        name="proj_ab",
    )(x2d, wa, wc, gq, wuq, gkv, wukv, cq, sq, ck, sk)


def _diff_attn_kernel(lam_ref, q_ref, k_ref, v_ref, g_ref, u_ref, o_ref, *, tq, seq, out_scale):
    i = pl.program_id(2)
    lam = lam_ref[0]
    q = q_ref[...]
    lane = lax.broadcasted_iota(jnp.int32, q.shape, 1)
    zero = jnp.zeros_like(q)
    q1 = jnp.where(lane < DA_HEAD_DIM, q, zero)
    q2 = jnp.where(lane >= DA_HEAD_DIM, q, zero)
    k = k_ref[...]
    bias = u_ref[0, :, pl.ds(pl.multiple_of(seq - tq - i * tq, tq), seq)]

    v = v_ref[...]

    def softmax_pv(qm):
        s = _dot_nt(qm, k) + bias
        p = jnp.exp2(s - jnp.max(s, axis=-1, keepdims=True))
        l = jnp.sum(p, axis=-1, keepdims=True)
        return _dot(p.astype(BF16), v) * (1.0 / l)

    o = softmax_pv(q1) - lam * softmax_pv(q2)
    o_ref[...] = (_rms(o, g_ref[...]) * out_scale).astype(BF16)


def _mla_attn_kernel(q_ref, k_ref, v_ref, o_ref, *, tq):
    k = k_ref[...]
    v = v_ref[...]
    for r in range(0, q_ref.shape[0], tq):
        s = _dot_nt(q_ref[r:r + tq, :], k)
        p = jnp.exp2(s - jnp.max(s, axis=-1, keepdims=True))
        l = jnp.sum(p, axis=-1, keepdims=True)
        o_ref[r:r + tq, :] = (_dot(p.astype(BF16), v) * (1.0 / l)).astype(BF16)


def _alibi_table(slopes, tq, seq):
    r = jnp.arange(tq, dtype=jnp.int32)[:, None]
    m = jnp.arange(2 * seq, dtype=jnp.int32)[None, :]
    dist = jnp.abs(r + (seq - tq) - m).astype(F32)
    return -slopes[:, None, None] * dist[None]


def _attention_ab(hcat, lam, slopes, diff_g, batch, seq, out_scale):
    t = hcat.shape[0]
    tq = TQ_ATTN
    nq = seq // tq
    table = _alibi_table(slopes, tq, seq)

    o_diff = pl.pallas_call(
        functools.partial(_diff_attn_kernel, tq=tq, seq=seq, out_scale=out_scale),
        grid_spec=pltpu.PrefetchScalarGridSpec(
            num_scalar_prefetch=1, grid=(DA_HEADS, batch, nq),
            in_specs=[pl.BlockSpec((tq, LANES), lambda h, b, i, *_: (b * nq + i, h)),
                      pl.BlockSpec((seq, LANES), lambda h, b, i, *_: (b, 4 + h)),
                      pl.BlockSpec((seq, LANES), lambda h, b, i, *_: (b, 8 + h)),
                      pl.BlockSpec((1, LANES), lambda h, b, i, *_: (0, 0)),
                      pl.BlockSpec((1, tq, 2 * seq), lambda h, b, i, *_: (h, 0, 0), pipeline_mode=pl.Buffered(1))],
            out_specs=pl.BlockSpec((tq, LANES), lambda h, b, i, *_: (b * nq + i, h))),
        out_shape=jax.ShapeDtypeStruct((t, DA_WIDTH), BF16),
        compiler_params=_cparams("parallel", "parallel", "parallel"),
        name="diff_attn",
    )(lam, hcat, hcat, hcat, diff_g, table)
    rows = MLA_ROW_GROUPS * MLA_GROUP_ROWS
    nr = seq // rows
    o_mla = pl.pallas_call(
        functools.partial(_mla_attn_kernel, tq=MLA_GROUP_ROWS),
        grid=(batch, MLA_HEADS, nr),
        in_specs=[pl.BlockSpec((rows, LANES), lambda b, h, i: (b * nr + i, 12 + h)),
                  pl.BlockSpec((seq, LANES), lambda b, h, i: (b, 16 + h)),
                  pl.BlockSpec((seq, LANES), lambda b, h, i: (b, 20 + h))],
        out_specs=pl.BlockSpec((rows, LANES), lambda b, h, i: (b * nr + i, h)),
        out_shape=jax.ShapeDtypeStruct((t, MLA_HEADS * MLA_V_DIM), BF16),
        compiler_params=_cparams("parallel", "parallel", "parallel"),
        name="mla_attn",
    )(hcat, hcat, hcat)
    return o_diff, o_mla


def _proj_c_kernel(x_ref, w_ref, *refs):
    (o1_ref, o4_ref, o16_ref), (h_scr, h4_scr) = refs[:3], refs[3:]
    assert [d for _, d in DIL_PATTERNS] == [1, 4, 16]
    tm = x_ref.shape[0]
    xb = x_ref[...].astype(BF16)
    for n in range(3):
        cols = slice(n * 1024, (n + 1) * 1024)
        h = _dot(xb, w_ref[:, cols])
        if n == 0:
            h = h * (DIL_HEAD_DIM ** -0.5 * LOG2E)
        o1_ref[0, 0, :, cols] = h.astype(BF16)
        for c in range(1024 // LANES):
            lanes = slice(cols.start + c * LANES, cols.start + (c + 1) * LANES)
            h_scr[...] = h[:, c * LANES:(c + 1) * LANES]
            for r4 in range(4):
                piece = h_scr[pl.ds(r4, tm // 4, stride=4), :]
                o4_ref[0, r4, :, lanes] = piece.astype(BF16)
                h4_scr[r4] = piece
            for r in range(16):
                o16_ref[0, r, :, lanes] = h4_scr[r % 4, pl.ds(r // 4, tm // 16, stride=4), :].astype(BF16)


def _proj_c(x2d, w, batch, seq):
    tm = TM_PROJ
    nb = seq // tm
    outs = pl.pallas_call(
        _proj_c_kernel,
        grid=(batch, nb),
        in_specs=[pl.BlockSpec((tm, D_MODEL), lambda b, i: (b * nb + i, 0)), pl.BlockSpec(w.shape, lambda b, i: (0, 0))],
        out_specs=[pl.BlockSpec((1, dil, tm // dil, 3072), lambda b, i: (b, 0, i, 0)) for _, dil in DIL_PATTERNS],
        out_shape=[jax.ShapeDtypeStruct((batch, dil, seq // dil, 3072), BF16) for _, dil in DIL_PATTERNS],
        scratch_shapes=[pltpu.VMEM((tm, LANES), F32), pltpu.VMEM((4, tm // 4, LANES), F32)],
        compiler_params=_cparams("parallel", "parallel"),
        name="proj_c",
    )(x2d, w)
    return [o.reshape(batch * seq, 3072) for o in outs]


def _band_attn_kernel(q_ref, kp_ref, kc_ref, kn_ref, vp_ref, vc_ref, vn_ref, band_ref, o_ref, lse_ref,
                      *, tq, sub_len):
    i = pl.program_id(1)
    n_sub = tq // BAND_SUB
    assert n_sub >= 2
    win = BAND_SUB + 2 * BAND_HALO
    lane = lax.broadcasted_iota(jnp.int32, (BAND_SUB, LANES), 1)
    first_half = lane < DIL_HEAD_DIM
    variant = [BAND_INTERIOR] * n_sub
    variant[0] = jnp.where(i == 0, BAND_FIRST, BAND_INTERIOR)
    variant[-1] = jnp.where(i == sub_len // tq - 1, BAND_LAST, BAND_INTERIOR)
    for pair in range(DIL_HEADS // 2):
        cs = slice(pair * LANES, (pair + 1) * LANES)
        kcat = jnp.concatenate([kp_ref[:, cs], kc_ref[:, cs], kn_ref[:, cs]], axis=0)
        vcat = jnp.concatenate([vp_ref[:, cs], vc_ref[:, cs], vn_ref[:, cs]], axis=0)
        for sb in range(n_sub):
            r0 = sb * BAND_SUB
            qp = q_ref[r0:r0 + BAND_SUB, cs]
            kw = kcat[r0:r0 + win]
            vw = vcat[r0:r0 + win]
            outs, ms, ls = [], [], []
            for half in range(2):
                hd = 2 * pair + half
                qh = jnp.where(first_half if half == 0 else jnp.logical_not(first_half), qp, jnp.zeros_like(qp))
                s = _dot_nt(qh, kw) + band_ref[variant[sb], hd]
                m = jnp.max(s, axis=-1, keepdims=True)
                p = jnp.exp2(s - m)
                ms.append(m)
                ls.append(jnp.sum(p, axis=-1, keepdims=True))
                outs.append(_dot(p.astype(BF16), vw))
            o_ref[r0:r0 + BAND_SUB, cs] = jnp.where(first_half, outs[0], outs[1]).astype(BF16)
            st = lse_ref[r0:r0 + BAND_SUB, :] if pair > 0 else jnp.zeros((BAND_SUB, LANES), F32)
            for half in range(2):
                st = jnp.where(lane == 2 * pair + half, ms[half], st)
                st = jnp.where(lane == DIL_HEADS + 2 * pair + half, ls[half], st)
            lse_ref[r0:r0 + BAND_SUB, :] = st


def _band_attention(hq, band, n_seq, sub_len):
    t = hq.shape[0]
    tq = TQ_BAND
    nq = sub_len // tq
    hb = tq // BAND_HALO

    def cur(col):
        return pl.BlockSpec((tq, 1024), lambda s, i: (s * nq + i, col))

    def prv(col):
        return pl.BlockSpec((BAND_HALO, 1024), lambda s, i: (jnp.maximum((s * nq + i) * hb - 1, s * nq * hb), col))

    def nxt(col):
        return pl.BlockSpec((BAND_HALO, 1024),
                            lambda s, i: (jnp.minimum((s * nq + i + 1) * hb, (s + 1) * nq * hb - 1), col))

    return pl.pallas_call(
        functools.partial(_band_attn_kernel, tq=tq, sub_len=sub_len),
        grid=(n_seq, nq),
        in_specs=[cur(0), prv(1), cur(1), nxt(1), prv(2), cur(2), nxt(2),
                  pl.BlockSpec(band.shape, lambda s, i: (0, 0, 0, 0), pipeline_mode=pl.Buffered(1))],
        out_specs=[pl.BlockSpec((tq, 1024), lambda s, i: (s * nq + i, 0)),
                   pl.BlockSpec((tq, LANES), lambda s, i: (s * nq + i, 0))],
        out_shape=[jax.ShapeDtypeStruct((t, 1024), BF16), jax.ShapeDtypeStruct((t, LANES), F32)],
        compiler_params=_cparams("parallel", "parallel"),
        name="band_attn",
    )(hq, hq, hq, hq, hq, hq, hq, band)


ROUTE_FINE0 = MOE_GROUPS
ROUTE_FINE1 = MOE_GROUPS + MOE_EXPERTS


def _route_and_rank(x1, wrh_ref, wrm_ref, br_ref, ri_ref, rg_ref, cnt_ref):
    tm = x1.shape[0]
    hi = x1.astype(BF16)
    mid = (x1 - hi.astype(F32)).astype(BF16)
    logits = _dot(hi, wrh_ref[...]) + (_dot(hi, wrm_ref[...]) + _dot(mid, wrh_ref[...])) + br_ref[...]
    lane = lax.broadcasted_iota(jnp.int32, (tm, LANES), 1)
    lane_f = lane.astype(F32)

    def first_lane(mask):
        return jnp.min(jnp.where(mask, lane_f, float(LANES)), axis=-1, keepdims=True).astype(jnp.int32)

    coarse = jnp.where(lane < MOE_GROUPS, logits, NEG_INF)
    cmax = jnp.max(coarse, axis=-1, keepdims=True)
    gsel = first_lane(coarse == cmax)
    p_group = 1.0 / jnp.sum(jnp.exp(coarse - cmax), axis=-1, keepdims=True)
    in_group = ((lane >= ROUTE_FINE0) & (lane < ROUTE_FINE1)
                & (jnp.right_shift(lane - ROUTE_FINE0, 3) == gsel))
    fine = jnp.where(in_group, logits, NEG_INF)
    v1 = jnp.max(fine, axis=-1, keepdims=True)
    i1 = first_lane(fine == v1)
    fine2 = jnp.where(lane == i1, NEG_INF, fine)
    v2 = jnp.max(fine2, axis=-1, keepdims=True)
    i2 = first_lane(fine2 == v2)
    e2 = jnp.exp(v2 - v1)
    den = 1.0 + e2
    g1 = p_group * (1.0 / den)
    g2 = p_group * (e2 / den)
    sel1 = lane == i1
    sel2 = lane == i2
    onehot = jnp.where(sel1 | sel2, 1.0, 0.0)
    row = lax.broadcasted_iota(jnp.int32, (tm, tm), 0)
    col = lax.broadcasted_iota(jnp.int32, (tm, tm), 1)
    ltri = jnp.where(col < row, 1.0, 0.0).astype(BF16)
    before = _dot(ltri, onehot.astype(BF16)) + cnt_ref[...]
    rank1 = jnp.sum(jnp.where(sel1, before, 0.0), axis=-1, keepdims=True).astype(jnp.int32)
    rank2 = jnp.sum(jnp.where(sel2, before, 0.0), axis=-1, keepdims=True).astype(jnp.int32)
    cnt_ref[...] = cnt_ref[...] + jnp.sum(onehot, axis=0, keepdims=True)
    zi = jnp.zeros((tm, LANES), jnp.int32)
    ri_ref[...] = jnp.where(lane == 0, i1 - ROUTE_FINE0,
                            jnp.where(lane == 1, i2 - ROUTE_FINE0,
                                      jnp.where(lane == 2, rank1, jnp.where(lane == 3, rank2, zi))))
    rg_ref[...] = jnp.where(lane == 0, g1, jnp.where(lane == 1, g2, jnp.zeros((tm, LANES), F32)))


def _outproj_ab_kernel(oa_ref, ob_ref, wo_ref, x_ref, g_ref, b_ref, wrh_ref, wrm_ref, br_ref,
                       x1_ref, ri_ref, rg_ref, cnt_ref):
    @pl.when(pl.program_id(0) == 0)
    def _():
        cnt_ref[...] = jnp.zeros_like(cnt_ref)

    mixed = _dot(oa_ref[...], wo_ref[0:512, :]) + _dot(ob_ref[...], wo_ref[512:1024, :])
    x1 = _layer_norm(DEEPNORM_ALPHA * x_ref[...] + mixed, g_ref[...], b_ref[...])
    x1_ref[...] = x1
    _route_and_rank(x1, wrh_ref, wrm_ref, br_ref, ri_ref, rg_ref, cnt_ref)


def _outproj_c_kernel(o1_ref, o2_ref, o3_ref, l1_ref, l2_ref, l3_ref, wo_ref, x_ref, g_ref, b_ref,
                      wrh_ref, wrm_ref, br_ref, x1_ref, ri_ref, rg_ref, cnt_ref, o_nat, l_nat):
    @pl.when(pl.program_id(0) == 0)
    def _():
        cnt_ref[...] = jnp.zeros_like(cnt_ref)

    tm = x_ref.shape[0]
    for p, ((_, dil), o_ref, l_ref) in enumerate(zip(DIL_PATTERNS, (o1_ref, o2_ref, o3_ref), (l1_ref, l2_ref, l3_ref))):
        for r in range(dil):
            rows = pl.ds(r, tm // dil, stride=dil)
            for c in range(DIL_HEADS // 2):
                o_nat[p, c, rows, :] = o_ref[0, r, :, c * LANES:(c + 1) * LANES].astype(F32)
            l_nat[p, rows, :] = l_ref[0, r]
    stats = [l_nat[p] for p in range(len(DIL_PATTERNS))]
    m_max = functools.reduce(jnp.maximum, stats)
    es = [jnp.exp2(st - m_max) for st in stats]
    den = sum(e * pltpu.roll(st, LANES - DIL_HEADS, 1) for e, st in zip(es, stats))
    inv = 1.0 / den
    w1, w2, w3 = [e * inv for e in es]
    first_half = lax.broadcasted_iota(jnp.int32, (tm, LANES), 1) < DIL_HEAD_DIM
    mixed = jnp.zeros((tm, D_MODEL), F32)
    for pair in range(DIL_HEADS // 2):
        cs = slice(pair * LANES, (pair + 1) * LANES)
        ha, hb = 2 * pair, 2 * pair + 1
        om = None
        for p, w in enumerate((w1, w2, w3)):
            wf = jnp.where(first_half, w[:, ha:ha + 1], w[:, hb:hb + 1])
            term = wf * o_nat[p, pair]
            om = term if om is None else om + term
        mixed = mixed + _dot(om.astype(BF16), wo_ref[cs, :])
    x1 = _layer_norm(DEEPNORM_ALPHA * x_ref[...] + mixed, g_ref[...], b_ref[...])
    x1_ref[...] = x1
    _route_and_rank(x1, wrh_ref, wrm_ref, br_ref, ri_ref, rg_ref, cnt_ref)


def _outproj_ln_route(kernel_fn, mix_inputs, mix_specs, wo, x2d, g, b, wrh, wrm, br, scratch=()):
    t = x2d.shape[0]
    tm = TM_ROW
    full = lambda a: pl.BlockSpec(a.shape, lambda i: (0,) * a.ndim)
    row = lambda w: pl.BlockSpec((tm, w), lambda i: (i, 0))
    return pl.pallas_call(
        kernel_fn,
        grid=(t // tm,),
        in_specs=list(mix_specs) + [full(wo), row(D_MODEL), full(g), full(b), full(wrh), full(wrm), full(br)],
        out_specs=[row(D_MODEL), row(LANES), row(LANES), pl.BlockSpec((1, LANES), lambda i: (0, 0))],
        out_shape=[jax.ShapeDtypeStruct((t, D_MODEL), F32), jax.ShapeDtypeStruct((t, LANES), jnp.int32),
                   jax.ShapeDtypeStruct((t, LANES), F32), jax.ShapeDtypeStruct((1, LANES), F32)],
        scratch_shapes=list(scratch),
        compiler_params=_cparams("arbitrary"),
        name=kernel_fn.__name__.strip("_"),
    )(*mix_inputs, wo, x2d, g, b, wrh, wrm, br)


DMA_UNROLL = 8


def _dispatch_kernel(pad_end_ref, dest_ref, x_ref, xs_ref, zbuf, sem, zsem):
    tm = x_ref.shape[0]
    n_blk = xs_ref.shape[0] // MOE_BLOCK

    @pl.when(pl.program_id(0) == 0)
    def _():
        zbuf[...] = jnp.zeros_like(zbuf)
        n_used = pad_end_ref[MOE_EXPERTS - 1] // MOE_BLOCK

        def zero_copy(row0):
            return pltpu.make_async_copy(zbuf, xs_ref.at[pl.ds(pl.multiple_of(row0, MOE_BLOCK), MOE_BLOCK)], zsem)

        def has_block(e):
            return pad_end_ref[e] > (pad_end_ref[e - 1] if e > 0 else 0)

        def tail_start(b, c):
            zero_copy(b * MOE_BLOCK).start()
            return c

        def tail_wait(b, c):
            zero_copy(b * MOE_BLOCK).wait()
            return c

        for e in range(MOE_EXPERTS):
            @pl.when(has_block(e))
            def _():
                zero_copy(pad_end_ref[e] - MOE_BLOCK).start()
        lax.fori_loop(n_used, n_blk, tail_start, 0)
        for e in range(MOE_EXPERTS):
            @pl.when(has_block(e))
            def _():
                zero_copy(pad_end_ref[e] - MOE_BLOCK).wait()
        lax.fori_loop(n_used, n_blk, tail_wait, 0)

    def issue(r, c):
        for k in range(MOE_TOP_K):
            pltpu.make_async_copy(x_ref.at[pl.ds(r, 1)], xs_ref.at[pl.ds(dest_ref[0, 0, 2 * r + k], 1)], sem).start()
        return c

    lax.fori_loop(0, tm, issue, 0, unroll=DMA_UNROLL)
    for _ in range(MOE_TOP_K * tm):
        pltpu.make_async_copy(x_ref.at[pl.ds(0, 1)], xs_ref.at[pl.ds(0, 1)], sem).wait()


def _dispatch(pad_end, dest3, x1, n_slot):
    t = x1.shape[0]
    tm = TM_DMA
    return pl.pallas_call(
        _dispatch_kernel,
        grid_spec=pltpu.PrefetchScalarGridSpec(
            num_scalar_prefetch=1, grid=(t // tm,),
            in_specs=[pl.BlockSpec((1, 1, 2 * tm), lambda i, pe: (i, 0, 0), memory_space=pltpu.SMEM),
                      pl.BlockSpec((tm, D_MODEL), lambda i, pe: (i, 0))],
            out_specs=pl.BlockSpec(memory_space=pl.ANY),
            scratch_shapes=[pltpu.VMEM((MOE_BLOCK, D_MODEL), F32), pltpu.SemaphoreType.DMA(()),
                            pltpu.SemaphoreType.DMA(())]),
        out_shape=jax.ShapeDtypeStruct((n_slot, D_MODEL), F32),
        compiler_params=_cparams("arbitrary"),
        name="moe_dispatch",
    )(pad_end, dest3, x1)


def _expert_kernel(eid_ref, nused_ref, xs_ref, wg_ref, wu_ref, wd_ref, ys_ref, wgu_bf, wd_bf):
    i = pl.program_id(0)

    @pl.when((i == 0) | (eid_ref[i] != eid_ref[jnp.maximum(i - 1, 0)]))
    def _():
        wgu_bf[:, 0:MOE_HIDDEN] = wg_ref[0, 0].astype(BF16)
        wgu_bf[:, MOE_HIDDEN:2 * MOE_HIDDEN] = wu_ref[0, 0].astype(BF16)
        wd_bf[...] = wd_ref[0, 0].astype(BF16)

    @pl.when(i < nused_ref[0])
    def _():
        xb = xs_ref[...].astype(BF16)
        gu = _dot(xb, wgu_bf[...])
        gate, up = gu[:, 0:MOE_HIDDEN], gu[:, MOE_HIDDEN:2 * MOE_HIDDEN]
        hidden = gate * (1.0 / (1.0 + jnp.exp(-gate))) * up
        ys_ref[...] = _dot(hidden.astype(BF16), wd_bf[...])

    @pl.when(i >= nused_ref[0])
    def _():
        ys_ref[...] = jnp.zeros_like(ys_ref)


def _expert_mlp(blk_eid, n_used, xs, w_gate, w_up, w_down, layer):
    n_slot = xs.shape[0]
    n_blk = n_slot // MOE_BLOCK
    wspec = lambda k, n: pl.BlockSpec((1, 1, k, n), lambda i, e, u: (layer, e[i], 0, 0))
    return pl.pallas_call(
        _expert_kernel,
        grid_spec=pltpu.PrefetchScalarGridSpec(
            num_scalar_prefetch=2, grid=(n_blk,),
            in_specs=[pl.BlockSpec((MOE_BLOCK, D_MODEL), lambda i, e, u: (i, 0)),
                      wspec(D_MODEL, MOE_HIDDEN), wspec(D_MODEL, MOE_HIDDEN), wspec(MOE_HIDDEN, D_MODEL)],
            out_specs=pl.BlockSpec((MOE_BLOCK, D_MODEL), lambda i, e, u: (i, 0)),
            scratch_shapes=[pltpu.VMEM((D_MODEL, 2 * MOE_HIDDEN), BF16), pltpu.VMEM((MOE_HIDDEN, D_MODEL), BF16)]),
        out_shape=jax.ShapeDtypeStruct((n_slot, D_MODEL), F32),
        compiler_params=_cparams("arbitrary"),
        name="expert_mlp",
    )(blk_eid, n_used, xs, w_gate, w_up, w_down)


def _combine_kernel(dest_ref, rg_ref, x_ref, g_ref, b_ref, ys_ref, o_ref, buf, sem):
    tm = x_ref.shape[0]

    def issue(r, c):
        for k in range(MOE_TOP_K):
            pltpu.make_async_copy(ys_ref.at[pl.ds(dest_ref[0, 0, 2 * r + k], 1)], buf.at[k, pl.ds(r, 1)], sem).start()
        return c

    lax.fori_loop(0, tm, issue, 0, unroll=DMA_UNROLL)
    for _ in range(MOE_TOP_K * tm):
        pltpu.make_async_copy(ys_ref.at[pl.ds(0, 1)], buf.at[0, pl.ds(0, 1)], sem).wait()
    rg = rg_ref[...]
    ffn = buf[0] * rg[:, 0:1] + buf[1] * rg[:, 1:2]
    o_ref[...] = _layer_norm(DEEPNORM_ALPHA * x_ref[...] + ffn, g_ref[...], b_ref[...])


def _combine_ln(dest3, rg, x1, g, b, ys):
    t = x1.shape[0]
    tm = TM_DMA
    row = lambda w: pl.BlockSpec((tm, w), lambda i: (i, 0))
    full = lambda a: pl.BlockSpec(a.shape, lambda i: (0,) * a.ndim)
    return pl.pallas_call(
        _combine_kernel,
        grid=(t // tm,),
        in_specs=[pl.BlockSpec((1, 1, 2 * tm), lambda i: (i, 0, 0), memory_space=pltpu.SMEM),
                  row(LANES), row(D_MODEL), full(g), full(b), pl.BlockSpec(memory_space=pl.ANY)],
        out_specs=row(D_MODEL),
        out_shape=jax.ShapeDtypeStruct((t, D_MODEL), F32),
        scratch_shapes=[pltpu.VMEM((2, tm, D_MODEL), F32), pltpu.SemaphoreType.DMA(())],
        compiler_params=_cparams("arbitrary"),
        name="moe_combine",
    )(dest3, rg, x1, g, b, ys)


def _moe(x1, ri, rg, cnt, w_gate, w_up, w_down, layer, g, b):
    t = x1.shape[0]
    n_slot = t * MOE_TOP_K + MOE_EXPERTS * MOE_BLOCK
    n_blk = n_slot // MOE_BLOCK
    counts = cnt[0, ROUTE_FINE0:ROUTE_FINE1].astype(jnp.int32)
    padded = (counts + MOE_BLOCK - 1) // MOE_BLOCK * MOE_BLOCK
    pad_end = jnp.cumsum(padded).astype(jnp.int32)
    pad_start = pad_end - padded
    onehot = ri[:, 0:2, None] == jnp.arange(MOE_EXPERTS, dtype=jnp.int32)
    dest = jnp.sum(jnp.where(onehot, pad_start, 0), axis=-1) + ri[:, 2:4]
    dest3 = dest.reshape(t // TM_DMA, 1, 2 * TM_DMA)
    blk_start = jnp.arange(n_blk, dtype=jnp.int32) * MOE_BLOCK
    blk_eid = jnp.minimum(jnp.sum((pad_end[None, :] <= blk_start[:, None]).astype(jnp.int32), axis=1),
                          MOE_EXPERTS - 1)
    n_used = pad_end[MOE_EXPERTS - 1:] // MOE_BLOCK
    xs = _dispatch(pad_end, dest3, x1, n_slot)
    ys = _expert_mlp(blk_eid, n_used, xs, w_gate, w_up, w_down, layer)
    return _combine_ln(dest3, rg, x1, g, b, ys)


def _rotate_half_cols(w):
    half = MLA_ROPE_DIM // 2
    return jnp.concatenate([-w[..., half:], w[..., :half]], axis=-1)


def _prep_ab(w_in, w_uq, w_ukv):
    wa = w_in[:, 0:3 * DA_WIDTH]
    c0 = 3 * DA_WIDTH
    w_cq = w_in[:, c0:c0 + MLA_Q_RANK]
    w_ckv = w_in[:, c0 + MLA_Q_RANK:c0 + MLA_Q_RANK + MLA_KV_RANK]
    w_kr = w_in[:, c0 + MLA_Q_RANK + MLA_KV_RANK:]
    z64 = jnp.zeros((D_MODEL, 64), F32)
    z32 = jnp.zeros((D_MODEL, 32), F32)
    wc = jnp.concatenate([w_cq, w_ckv, z64, w_kr, z32, z64, _rotate_half_cols(w_kr), z32], axis=1)
    qd = MLA_NOPE_DIM + MLA_ROPE_DIM
    wq = w_uq.reshape(MLA_Q_RANK, MLA_HEADS, qd)
    zq = jnp.zeros((MLA_Q_RANK, MLA_HEADS, LANES - qd), F32)
    plain = jnp.concatenate([wq, zq], axis=-1)
    rot = jnp.concatenate([jnp.zeros_like(wq[..., :MLA_NOPE_DIM]), _rotate_half_cols(wq[..., MLA_NOPE_DIM:]), zq],
                          axis=-1)
    wuq = jnp.concatenate([plain.reshape(MLA_Q_RANK, -1), rot.reshape(MLA_Q_RANK, -1)], axis=1)
    wkv = w_ukv.reshape(MLA_KV_RANK, MLA_HEADS, MLA_NOPE_DIM + MLA_V_DIM)
    k_pad = jnp.concatenate([wkv[..., :MLA_NOPE_DIM], jnp.zeros((MLA_KV_RANK, MLA_HEADS, 64), F32)], axis=-1)
    wukv = jnp.concatenate([k_pad.reshape(MLA_KV_RANK, -1), wkv[..., MLA_NOPE_DIM:].reshape(MLA_KV_RANK, -1)],
                           axis=1)
    return wa.astype(BF16), wc.astype(BF16), wuq.astype(BF16), wukv.astype(BF16)


def _rope_tables(seq):
    half = MLA_ROPE_DIM // 2
    inv_freq = jnp.power(ROPE_THETA, -jnp.arange(half, dtype=F32) * 2.0 / MLA_ROPE_DIM)
    ang = jnp.arange(seq, dtype=F32)[:, None] * inv_freq[None, :]
    cos = jnp.concatenate([jnp.cos(ang)] * 2, axis=-1)
    sin = jnp.concatenate([jnp.sin(ang)] * 2, axis=-1)
    scale = (MLA_NOPE_DIM + MLA_ROPE_DIM) ** -0.5 * LOG2E
    ones = jnp.ones((seq, 64), F32)
    z64 = jnp.zeros((seq, 64), F32)
    z32 = jnp.zeros((seq, 32), F32)
    cq = jnp.concatenate([ones, cos, z32], axis=-1) * scale
    sq = jnp.concatenate([z64, sin, z32], axis=-1) * scale
    ck = jnp.concatenate([z64, cos, z32], axis=-1)
    sk = jnp.concatenate([z64, sin, z32], axis=-1)
    return cq, sq, ck, sk


def _alibi_slopes(n_heads):
    start = 2.0 ** (-8.0 / n_heads)
    return jnp.asarray([start ** (i + 1) for i in range(n_heads)], dtype=F32)


def _band_table(dilation):
    win = BAND_SUB + 2 * BAND_HALO
    kk = jnp.arange(win, dtype=jnp.int32)[None, :]
    rel = kk - BAND_HALO - jnp.arange(BAND_SUB, dtype=jnp.int32)[:, None]
    dist = (jnp.abs(rel) * dilation).astype(F32)
    bias = -(_alibi_slopes(DIL_HEADS) * LOG2E)[:, None, None] * dist[None]
    interior = jnp.where((jnp.abs(rel) <= BAND_HALO)[None], bias, NEG_INF)
    first = jnp.where((kk >= BAND_HALO)[None], interior, NEG_INF)
    last = jnp.where((kk < BAND_SUB + BAND_HALO)[None], interior, NEG_INF)
    return jnp.stack([interior, first, last])


def _prep_router(w_group, b_group, w_route, b_route):
    pad = LANES - MOE_GROUPS - MOE_EXPERTS
    w = jnp.concatenate([w_group, w_route, jnp.zeros((D_MODEL, pad), F32)], axis=1)
    hi = w.astype(BF16)
    mid = (w - hi.astype(F32)).astype(BF16)
    bias = jnp.concatenate([b_group, b_route, jnp.zeros((pad,), F32)])[None, :]
    return hi, mid, bias


def kernel(x, w_in_ab, lam_q1, lam_k1, lam_q2, lam_k2, diff_norm_g, mla_q_norm_g, w_uq, mla_kv_norm_g, w_ukv,
           w_out_ab, w_in_c, w_out_c, ln_mix_g, ln_mix_b, moe_w_group, moe_b_group, moe_w_route, moe_b_route,
           moe_w_gate, moe_w_up, moe_w_down, ln_ffn_g, ln_ffn_b):
    batch, seq, d = x.shape
    t = batch * seq
    xs = x.reshape(t, d)

    wa, wc, wuq, wukv = _prep_ab(w_in_ab[0], w_uq[0], w_ukv[0])
    cq, sq, ck, sk = _rope_tables(seq)
    hcat = _proj_ab(xs, wa, wc, mla_q_norm_g[0][None, :], wuq, mla_kv_norm_g[0][None, :], wukv, cq, sq, ck, sk, seq)
    lam_init = 0.8 - 0.6 * math.exp(-0.3 * 0)
    lam = (jnp.exp(jnp.sum(lam_q1[0] * lam_k1[0])) - jnp.exp(jnp.sum(lam_q2[0] * lam_k2[0])) + lam_init)
    o_diff, o_mla = _attention_ab(hcat, lam[None].astype(F32), _alibi_slopes(DA_HEADS) * LOG2E,
                                  diff_norm_g[0][None, :], batch, seq, 1.0 - lam_init)
    wrh, wrm, br = _prep_router(moe_w_group[0], moe_b_group[0], moe_w_route[0], moe_b_route[0])
    row = lambda w: pl.BlockSpec((TM_ROW, w), lambda i: (i, 0))
    x1, ri, rg, cnt = _outproj_ln_route(_outproj_ab_kernel, (o_diff, o_mla), (row(512), row(512)),
                                        w_out_ab[0].astype(BF16), xs, ln_mix_g[0][None, :], ln_mix_b[0][None, :],
                                        wrh, wrm, br)
    xs = _moe(x1, ri, rg, cnt, moe_w_gate, moe_w_up, moe_w_down, 0, ln_ffn_g[0][None, :], ln_ffn_b[0][None, :])

    dil_w = DIL_HEADS * DIL_HEAD_DIM
    hqs = _proj_c(xs, w_in_c[0].astype(BF16), batch, seq)
    nb = seq // TM_ROW
    outs, lses, o_specs, l_specs = [], [], [], []
    for (window, dil), hq in zip(DIL_PATTERNS, hqs):
        assert window // (2 * dil) == BAND_HALO
        sub = seq // dil
        o_p, lse_p = _band_attention(hq, _band_table(dil), batch * dil, sub)
        outs.append(o_p.reshape(batch, dil, sub, dil_w))
        lses.append(lse_p.reshape(batch, dil, sub, LANES))
        o_specs.append(pl.BlockSpec((1, dil, TM_ROW // dil, dil_w), lambda i: (i // nb, 0, i % nb, 0)))
        l_specs.append(pl.BlockSpec((1, dil, TM_ROW // dil, LANES), lambda i: (i // nb, 0, i % nb, 0)))
    wrh, wrm, br = _prep_router(moe_w_group[1], moe_b_group[1], moe_w_route[1], moe_b_route[1])
    n_pat = len(DIL_PATTERNS)
    x1, ri, rg, cnt = _outproj_ln_route(_outproj_c_kernel, (*outs, *lses), (*o_specs, *l_specs),
                                        w_out_c[0].astype(BF16), xs, ln_mix_g[1][None, :], ln_mix_b[1][None, :],
                                        wrh, wrm, br,
                                        scratch=(pltpu.VMEM((n_pat, DIL_HEADS // 2, TM_ROW, LANES), F32),
                                                 pltpu.VMEM((n_pat, TM_ROW, LANES), F32)))
    xs = _moe(x1, ri, rg, cnt, moe_w_gate, moe_w_up, moe_w_down, 1, ln_ffn_g[1][None, :], ln_ffn_b[1][None, :])
    return xs.reshape(batch, seq, d)
```

```python
import functools
import math

import jax
import jax.numpy as jnp
from jax import lax
from jax.experimental import pallas as pl
from jax.experimental.pallas import tpu as pltpu

D_MODEL = 1024
DEPTH = 2
DA_HEADS = 4
DA_HEAD_DIM = 64
DA_WIDTH = DA_HEADS * 2 * DA_HEAD_DIM
MLA_HEADS = 4
MLA_Q_RANK = 256
MLA_KV_RANK = 128
MLA_NOPE_DIM = 64
MLA_ROPE_DIM = 32
MLA_V_DIM = 128
ROPE_THETA = 10000.0
DIL_HEADS = 16
DIL_HEAD_DIM = 64
DIL_PATTERNS = ((128, 1), (512, 4), (2048, 16))
MOE_GROUPS = 4
MOE_EXPERTS_PER_GROUP = 8
MOE_EXPERTS = MOE_GROUPS * MOE_EXPERTS_PER_GROUP
MOE_TOP_K = 2
MOE_HIDDEN = 512
MOE_BLOCK = 512
DEEPNORM_ALPHA = (2 * DEPTH) ** 0.25
NORM_EPS = 1e-5

LANES = 128
BF16 = jnp.bfloat16
F32 = jnp.float32
NEG_INF = float("-inf")
LOG2E = math.log2(math.e)

TM_PROJ = 512
TM_ROW = 512
TM_DMA = 256
TQ_ATTN = 256
MLA_ROW_GROUPS = 2
MLA_GROUP_ROWS = 256
TQ_BAND = 256
BAND_HALO = 64
BAND_SUB = 128
BAND_INTERIOR, BAND_FIRST, BAND_LAST = 0, 1, 2
VMEM_LIMIT = 56 * 1024 * 1024


def _cparams(*sem):
    return pltpu.CompilerParams(dimension_semantics=sem, vmem_limit_bytes=VMEM_LIMIT)


def _dot(a, b):
    return jnp.dot(a, b, preferred_element_type=F32)


def _dot_nt(a, b):
    return lax.dot_general(a, b, (((1,), (1,)), ((), ())), preferred_element_type=F32)


def _rms(x, g):
    return x * lax.rsqrt(jnp.mean(x * x, axis=-1, keepdims=True) + NORM_EPS) * g


def _layer_norm(y, g, b):
    mu = jnp.mean(y, axis=-1, keepdims=True)
    yc = y - mu
    var = jnp.mean(yc * yc, axis=-1, keepdims=True)
    return yc * lax.rsqrt(var + NORM_EPS) * g + b


def _proj_ab_kernel(x_ref, wa_ref, wc_ref, gq_ref, wuq_ref, gkv_ref, wukv_ref, cq_ref, sq_ref, ck_ref, sk_ref,
                    o_ref):
    xb = x_ref[...].astype(BF16)
    ha = _dot(xb, wa_ref[...])
    o_ref[:, 0:DA_WIDTH] = (ha[:, 0:DA_WIDTH] * (DA_HEAD_DIM ** -0.5 * LOG2E)).astype(BF16)
    o_ref[:, DA_WIDTH:3 * DA_WIDTH] = ha[:, DA_WIDTH:3 * DA_WIDTH].astype(BF16)
    hc = _dot(xb, wc_ref[...])
    c_q = hc[:, 0:MLA_Q_RANK]
    c_kv = hc[:, MLA_Q_RANK:MLA_Q_RANK + MLA_KV_RANK]
    kr = hc[:, 384:512]
    kr_rot = hc[:, 512:640]
    q2 = _dot(_rms(c_q, gq_ref[...]).astype(BF16), wuq_ref[...])
    kv2 = _dot(_rms(c_kv, gkv_ref[...]).astype(BF16), wukv_ref[...])
    cq, sq, ck, sk = cq_ref[...], sq_ref[...], ck_ref[...], sk_ref[...]
    k_rope = kr * ck + kr_rot * sk
    base = 3 * DA_WIDTH
    for h in range(MLA_HEADS):
        lo, hi = h * LANES, (h + 1) * LANES
        qh = q2[:, lo:hi] * cq + q2[:, 512 + lo:512 + hi] * sq
        o_ref[:, base + lo:base + hi] = qh.astype(BF16)
        o_ref[:, base + 512 + lo:base + 512 + hi] = (kv2[:, lo:hi] + k_rope).astype(BF16)
    o_ref[:, base + 1024:base + 1536] = kv2[:, 512:1024].astype(BF16)


def _proj_ab(x2d, wa, wc, gq, wuq, gkv, wukv, cq, sq, ck, sk, seq):
    t = x2d.shape[0]
    tm = TM_PROJ
    nsb = seq // tm
    full = lambda a: pl.BlockSpec(a.shape, lambda i: (0,) * a.ndim)
    tab = pl.BlockSpec((tm, LANES), lambda i: (i % nsb, 0))
    return pl.pallas_call(
        _proj_ab_kernel,
        grid=(t // tm,),
        in_specs=[pl.BlockSpec((tm, D_MODEL), lambda i: (i, 0)), full(wa), full(wc), full(gq), full(wuq),
                  full(gkv), full(wukv), tab, tab, tab, tab],
        out_specs=pl.BlockSpec((tm, 3072), lambda i: (i, 0)),
        out_shape=jax.ShapeDtypeStruct((t, 3072), BF16),
        compiler_params=_cparams("parallel"),
        name="proj_ab",
    )(x2d, wa, wc, gq, wuq, gkv, wukv, cq, sq, ck, sk)


def _diff_attn_kernel(lam_ref, q_ref, k_ref, v_ref, g_ref, u_ref, o_ref, *, tq, seq, out_scale):
    i = pl.program_id(2)
    lam = lam_ref[0]
    q = q_ref[...]
    lane = lax.broadcasted_iota(jnp.int32, q.shape, 1)
    zero = jnp.zeros_like(q)
    q1 = jnp.where(lane < DA_HEAD_DIM, q, zero)
    q2 = jnp.where(lane >= DA_HEAD_DIM, q, zero)
    k = k_ref[...]
    bias = u_ref[0, :, pl.ds(pl.multiple_of(seq - tq - i * tq, tq), seq)]

    v = v_ref[...]

    def softmax_pv(qm):
        s = _dot_nt(qm, k) + bias
        p = jnp.exp2(s - jnp.max(s, axis=-1, keepdims=True))
        l = jnp.sum(p, axis=-1, keepdims=True)
        return _dot(p.astype(BF16), v) * (1.0 / l)

    o = softmax_pv(q1) - lam * softmax_pv(q2)
    o_ref[...] = (_rms(o, g_ref[...]) * out_scale).astype(BF16)


def _mla_attn_kernel(q_ref, k_ref, v_ref, o_ref, *, tq):
    k = k_ref[...]
    v = v_ref[...]
    for r in range(0, q_ref.shape[0], tq):
        s = _dot_nt(q_ref[r:r + tq, :], k)
        p = jnp.exp2(s - jnp.max(s, axis=-1, keepdims=True))
        l = jnp.sum(p, axis=-1, keepdims=True)
        o_ref[r:r + tq, :] = (_dot(p.astype(BF16), v) * (1.0 / l)).astype(BF16)


def _alibi_table(slopes, tq, seq):
    r = jnp.arange(tq, dtype=jnp.int32)[:, None]
    m = jnp.arange(2 * seq, dtype=jnp.int32)[None, :]
    dist = jnp.abs(r + (seq - tq) - m).astype(F32)
    return -slopes[:, None, None] * dist[None]


def _attention_ab(hcat, lam, slopes, diff_g, batch, seq, out_scale):
    t = hcat.shape[0]
    tq = TQ_ATTN
    nq = seq // tq
    table = _alibi_table(slopes, tq, seq)

    o_diff = pl.pallas_call(
        functools.partial(_diff_attn_kernel, tq=tq, seq=seq, out_scale=out_scale),
        grid_spec=pltpu.PrefetchScalarGridSpec(
            num_scalar_prefetch=1, grid=(DA_HEADS, batch, nq),
            in_specs=[pl.BlockSpec((tq, LANES), lambda h, b, i, *_: (b * nq + i, h)),
                      pl.BlockSpec((seq, LANES), lambda h, b, i, *_: (b, 4 + h)),
                      pl.BlockSpec((seq, LANES), lambda h, b, i, *_: (b, 8 + h)),
                      pl.BlockSpec((1, LANES), lambda h, b, i, *_: (0, 0)),
                      pl.BlockSpec((1, tq, 2 * seq), lambda h, b, i, *_: (h, 0, 0), pipeline_mode=pl.Buffered(1))],
            out_specs=pl.BlockSpec((tq, LANES), lambda h, b, i, *_: (b * nq + i, h))),
        out_shape=jax.ShapeDtypeStruct((t, DA_WIDTH), BF16),
        compiler_params=_cparams("parallel", "parallel", "parallel"),
        name="diff_attn",
    )(lam, hcat, hcat, hcat, diff_g, table)
    rows = MLA_ROW_GROUPS * MLA_GROUP_ROWS
    nr = seq // rows
    o_mla = pl.pallas_call(
        functools.partial(_mla_attn_kernel, tq=MLA_GROUP_ROWS),
        grid=(batch, MLA_HEADS, nr),
        in_specs=[pl.BlockSpec((rows, LANES), lambda b, h, i: (b * nr + i, 12 + h)),
                  pl.BlockSpec((seq, LANES), lambda b, h, i: (b, 16 + h)),
                  pl.BlockSpec((seq, LANES), lambda b, h, i: (b, 20 + h))],
        out_specs=pl.BlockSpec((rows, LANES), lambda b, h, i: (b * nr + i, h)),
        out_shape=jax.ShapeDtypeStruct((t, MLA_HEADS * MLA_V_DIM), BF16),
        compiler_params=_cparams("parallel", "parallel", "parallel"),
        name="mla_attn",
    )(hcat, hcat, hcat)
    return o_diff, o_mla


def _proj_c_kernel(x_ref, w_ref, *refs):
    (o1_ref, o4_ref, o16_ref), (h_scr, h4_scr) = refs[:3], refs[3:]
    assert [d for _, d in DIL_PATTERNS] == [1, 4, 16]
    tm = x_ref.shape[0]
    xb = x_ref[...].astype(BF16)
    for n in range(3):
        cols = slice(n * 1024, (n + 1) * 1024)
        h = _dot(xb, w_ref[:, cols])
        if n == 0:
            h = h * (DIL_HEAD_DIM ** -0.5 * LOG2E)
        o1_ref[0, 0, :, cols] = h.astype(BF16)
        for c in range(1024 // LANES):
            lanes = slice(cols.start + c * LANES, cols.start + (c + 1) * LANES)
            h_scr[...] = h[:, c * LANES:(c + 1) * LANES]
            for r4 in range(4):
                piece = h_scr[pl.ds(r4, tm // 4, stride=4), :]
                o4_ref[0, r4, :, lanes] = piece.astype(BF16)
                h4_scr[r4] = piece
            for r in range(16):
                o16_ref[0, r, :, lanes] = h4_scr[r % 4, pl.ds(r // 4, tm // 16, stride=4), :].astype(BF16)


def _proj_c(x2d, w, batch, seq):
    tm = TM_PROJ
    nb = seq // tm
    outs = pl.pallas_call(
        _proj_c_kernel,
        grid=(batch, nb),
        in_specs=[pl.BlockSpec((tm, D_MODEL), lambda b, i: (b * nb + i, 0)), pl.BlockSpec(w.shape, lambda b, i: (0, 0))],
        out_specs=[pl.BlockSpec((1, dil, tm // dil, 3072), lambda b, i: (b, 0, i, 0)) for _, dil in DIL_PATTERNS],
        out_shape=[jax.ShapeDtypeStruct((batch, dil, seq // dil, 3072), BF16) for _, dil in DIL_PATTERNS],
        scratch_shapes=[pltpu.VMEM((tm, LANES), F32), pltpu.VMEM((4, tm // 4, LANES), F32)],
        compiler_params=_cparams("parallel", "parallel"),
        name="proj_c",
    )(x2d, w)
    return [o.reshape(batch * seq, 3072) for o in outs]


def _band_attn_kernel(q_ref, kp_ref, kc_ref, kn_ref, vp_ref, vc_ref, vn_ref, band_ref, o_ref, lse_ref,
                      *, tq, sub_len):
    i = pl.program_id(1)
    n_sub = tq // BAND_SUB
    assert n_sub >= 2
    win = BAND_SUB + 2 * BAND_HALO
    lane = lax.broadcasted_iota(jnp.int32, (BAND_SUB, LANES), 1)
    first_half = lane < DIL_HEAD_DIM
    variant = [BAND_INTERIOR] * n_sub
    variant[0] = jnp.where(i == 0, BAND_FIRST, BAND_INTERIOR)
    variant[-1] = jnp.where(i == sub_len // tq - 1, BAND_LAST, BAND_INTERIOR)
    for pair in range(DIL_HEADS // 2):
        cs = slice(pair * LANES, (pair + 1) * LANES)
        kcat = jnp.concatenate([kp_ref[:, cs], kc_ref[:, cs], kn_ref[:, cs]], axis=0)
        vcat = jnp.concatenate([vp_ref[:, cs], vc_ref[:, cs], vn_ref[:, cs]], axis=0)
        for sb in range(n_sub):
            r0 = sb * BAND_SUB
            qp = q_ref[r0:r0 + BAND_SUB, cs]
            kw = kcat[r0:r0 + win]
            vw = vcat[r0:r0 + win]
            outs, ms, ls = [], [], []
            for half in range(2):
                hd = 2 * pair + half
                qh = jnp.where(first_half if half == 0 else jnp.logical_not(first_half), qp, jnp.zeros_like(qp))
                s = _dot_nt(qh, kw) + band_ref[variant[sb], hd]
                m = jnp.max(s, axis=-1, keepdims=True)
                p = jnp.exp2(s - m)
                ms.append(m)
                ls.append(jnp.sum(p, axis=-1, keepdims=True))
                outs.append(_dot(p.astype(BF16), vw))
            o_ref[r0:r0 + BAND_SUB, cs] = jnp.where(first_half, outs[0], outs[1]).astype(BF16)
            st = lse_ref[r0:r0 + BAND_SUB, :] if pair > 0 else jnp.zeros((BAND_SUB, LANES), F32)
            for half in range(2):
                st = jnp.where(lane == 2 * pair + half, ms[half], st)
                st = jnp.where(lane == DIL_HEADS + 2 * pair + half, ls[half], st)
            lse_ref[r0:r0 + BAND_SUB, :] = st


def _band_attention(hq, band, n_seq, sub_len):
    t = hq.shape[0]
    tq = TQ_BAND
    nq = sub_len // tq
    hb = tq // BAND_HALO

    def cur(col):
        return pl.BlockSpec((tq, 1024), lambda s, i: (s * nq + i, col))

    def prv(col):
        return pl.BlockSpec((BAND_HALO, 1024), lambda s, i: (jnp.maximum((s * nq + i) * hb - 1, s * nq * hb), col))

    def nxt(col):
        return pl.BlockSpec((BAND_HALO, 1024),
                            lambda s, i: (jnp.minimum((s * nq + i + 1) * hb, (s + 1) * nq * hb - 1), col))

    return pl.pallas_call(
        functools.partial(_band_attn_kernel, tq=tq, sub_len=sub_len),
        grid=(n_seq, nq),
        in_specs=[cur(0), prv(1), cur(1), nxt(1), prv(2), cur(2), nxt(2),
                  pl.BlockSpec(band.shape, lambda s, i: (0, 0, 0, 0), pipeline_mode=pl.Buffered(1))],
        out_specs=[pl.BlockSpec((tq, 1024), lambda s, i: (s * nq + i, 0)),
                   pl.BlockSpec((tq, LANES), lambda s, i: (s * nq + i, 0))],
        out_shape=[jax.ShapeDtypeStruct((t, 1024), BF16), jax.ShapeDtypeStruct((t, LANES), F32)],
        compiler_params=_cparams("parallel", "parallel"),
        name="band_attn",
    )(hq, hq, hq, hq, hq, hq, hq, band)


ROUTE_FINE0 = MOE_GROUPS
ROUTE_FINE1 = MOE_GROUPS + MOE_EXPERTS


def _route_and_rank(x1, wrh_ref, wrm_ref, br_ref, ri_ref, rg_ref, cnt_ref):
    tm = x1.shape[0]
    hi = x1.astype(BF16)
    mid = (x1 - hi.astype(F32)).astype(BF16)
    logits = _dot(hi, wrh_ref[...]) + (_dot(hi, wrm_ref[...]) + _dot(mid, wrh_ref[...])) + br_ref[...]
    lane = lax.broadcasted_iota(jnp.int32, (tm, LANES), 1)
    lane_f = lane.astype(F32)

    def first_lane(mask):
        return jnp.min(jnp.where(mask, lane_f, float(LANES)), axis=-1, keepdims=True).astype(jnp.int32)

    coarse = jnp.where(lane < MOE_GROUPS, logits, NEG_INF)
    cmax = jnp.max(coarse, axis=-1, keepdims=True)
    gsel = first_lane(coarse == cmax)
    p_group = 1.0 / jnp.sum(jnp.exp(coarse - cmax), axis=-1, keepdims=True)
    in_group = ((lane >= ROUTE_FINE0) & (lane < ROUTE_FINE1)
                & (jnp.right_shift(lane - ROUTE_FINE0, 3) == gsel))
    fine = jnp.where(in_group, logits, NEG_INF)
    v1 = jnp.max(fine, axis=-1, keepdims=True)
    i1 = first_lane(fine == v1)
    fine2 = jnp.where(lane == i1, NEG_INF, fine)
    v2 = jnp.max(fine2, axis=-1, keepdims=True)
    i2 = first_lane(fine2 == v2)
    e2 = jnp.exp(v2 - v1)
    den = 1.0 + e2
    g1 = p_group * (1.0 / den)
    g2 = p_group * (e2 / den)
    sel1 = lane == i1
    sel2 = lane == i2
    onehot = jnp.where(sel1 | sel2, 1.0, 0.0)
    row = lax.broadcasted_iota(jnp.int32, (tm, tm), 0)
    col = lax.broadcasted_iota(jnp.int32, (tm, tm), 1)
    ltri = jnp.where(col < row, 1.0, 0.0).astype(BF16)
    before = _dot(ltri, onehot.astype(BF16)) + cnt_ref[...]
    rank1 = jnp.sum(jnp.where(sel1, before, 0.0), axis=-1, keepdims=True).astype(jnp.int32)
    rank2 = jnp.sum(jnp.where(sel2, before, 0.0), axis=-1, keepdims=True).astype(jnp.int32)
    cnt_ref[...] = cnt_ref[...] + jnp.sum(onehot, axis=0, keepdims=True)
    zi = jnp.zeros((tm, LANES), jnp.int32)
    ri_ref[...] = jnp.where(lane == 0, i1 - ROUTE_FINE0,
                            jnp.where(lane == 1, i2 - ROUTE_FINE0,
                                      jnp.where(lane == 2, rank1, jnp.where(lane == 3, rank2, zi))))
    rg_ref[...] = jnp.where(lane == 0, g1, jnp.where(lane == 1, g2, jnp.zeros((tm, LANES), F32)))


def _outproj_ab_kernel(oa_ref, ob_ref, wo_ref, x_ref, g_ref, b_ref, wrh_ref, wrm_ref, br_ref,
                       x1_ref, ri_ref, rg_ref, cnt_ref):
    @pl.when(pl.program_id(0) == 0)
    def _():
        cnt_ref[...] = jnp.zeros_like(cnt_ref)

    mixed = _dot(oa_ref[...], wo_ref[0:512, :]) + _dot(ob_ref[...], wo_ref[512:1024, :])
    x1 = _layer_norm(DEEPNORM_ALPHA * x_ref[...] + mixed, g_ref[...], b_ref[...])
    x1_ref[...] = x1
    _route_and_rank(x1, wrh_ref, wrm_ref, br_ref, ri_ref, rg_ref, cnt_ref)


def _outproj_c_kernel(o1_ref, o2_ref, o3_ref, l1_ref, l2_ref, l3_ref, wo_ref, x_ref, g_ref, b_ref,
                      wrh_ref, wrm_ref, br_ref, x1_ref, ri_ref, rg_ref, cnt_ref, o_nat, l_nat):
    @pl.when(pl.program_id(0) == 0)
    def _():
        cnt_ref[...] = jnp.zeros_like(cnt_ref)

    tm = x_ref.shape[0]
    for p, ((_, dil), o_ref, l_ref) in enumerate(zip(DIL_PATTERNS, (o1_ref, o2_ref, o3_ref), (l1_ref, l2_ref, l3_ref))):
        for r in range(dil):
            rows = pl.ds(r, tm // dil, stride=dil)
            for c in range(DIL_HEADS // 2):
                o_nat[p, c, rows, :] = o_ref[0, r, :, c * LANES:(c + 1) * LANES].astype(F32)
            l_nat[p, rows, :] = l_ref[0, r]
    stats = [l_nat[p] for p in range(len(DIL_PATTERNS))]
    m_max = functools.reduce(jnp.maximum, stats)
    es = [jnp.exp2(st - m_max) for st in stats]
    den = sum(e * pltpu.roll(st, LANES - DIL_HEADS, 1) for e, st in zip(es, stats))
    inv = 1.0 / den
    w1, w2, w3 = [e * inv for e in es]
    first_half = lax.broadcasted_iota(jnp.int32, (tm, LANES), 1) < DIL_HEAD_DIM
    mixed = jnp.zeros((tm, D_MODEL), F32)
    for pair in range(DIL_HEADS // 2):
        cs = slice(pair * LANES, (pair + 1) * LANES)
        ha, hb = 2 * pair, 2 * pair + 1
        om = None
        for p, w in enumerate((w1, w2, w3)):
            wf = jnp.where(first_half, w[:, ha:ha + 1], w[:, hb:hb + 1])
            term = wf * o_nat[p, pair]
            om = term if om is None else om + term
        mixed = mixed + _dot(om.astype(BF16), wo_ref[cs, :])
    x1 = _layer_norm(DEEPNORM_ALPHA * x_ref[...] + mixed, g_ref[...], b_ref[...])
    x1_ref[...] = x1
    _route_and_rank(x1, wrh_ref, wrm_ref, br_ref, ri_ref, rg_ref, cnt_ref)


def _outproj_ln_route(kernel_fn, mix_inputs, mix_specs, wo, x2d, g, b, wrh, wrm, br, scratch=()):
    t = x2d.shape[0]
    tm = TM_ROW
    full = lambda a: pl.BlockSpec(a.shape, lambda i: (0,) * a.ndim)
    row = lambda w: pl.BlockSpec((tm, w), lambda i: (i, 0))
    return pl.pallas_call(
        kernel_fn,
        grid=(t // tm,),
        in_specs=list(mix_specs) + [full(wo), row(D_MODEL), full(g), full(b), full(wrh), full(wrm), full(br)],
        out_specs=[row(D_MODEL), row(LANES), row(LANES), pl.BlockSpec((1, LANES), lambda i: (0, 0))],
        out_shape=[jax.ShapeDtypeStruct((t, D_MODEL), F32), jax.ShapeDtypeStruct((t, LANES), jnp.int32),
                   jax.ShapeDtypeStruct((t, LANES), F32), jax.ShapeDtypeStruct((1, LANES), F32)],
        scratch_shapes=list(scratch),
        compiler_params=_cparams("arbitrary"),
        name=kernel_fn.__name__.strip("_"),
    )(*mix_inputs, wo, x2d, g, b, wrh, wrm, br)


DMA_UNROLL = 8


def _dispatch_kernel(pad_end_ref, dest_ref, x_ref, xs_ref, zbuf, sem, zsem):
    tm = x_ref.shape[0]
    n_blk = xs_ref.shape[0] // MOE_BLOCK

    @pl.when(pl.program_id(0) == 0)
    def _():
        zbuf[...] = jnp.zeros_like(zbuf)
        n_used = pad_end_ref[MOE_EXPERTS - 1] // MOE_BLOCK

        def zero_copy(row0):
            return pltpu.make_async_copy(zbuf, xs_ref.at[pl.ds(pl.multiple_of(row0, MOE_BLOCK), MOE_BLOCK)], zsem)

        def has_block(e):
            return pad_end_ref[e] > (pad_end_ref[e - 1] if e > 0 else 0)

        def tail_start(b, c):
            zero_copy(b * MOE_BLOCK).start()
            return c

        def tail_wait(b, c):
            zero_copy(b * MOE_BLOCK).wait()
            return c

        for e in range(MOE_EXPERTS):
            @pl.when(has_block(e))
            def _():
                zero_copy(pad_end_ref[e] - MOE_BLOCK).start()
        lax.fori_loop(n_used, n_blk, tail_start, 0)
        for e in range(MOE_EXPERTS):
            @pl.when(has_block(e))
            def _():
                zero_copy(pad_end_ref[e] - MOE_BLOCK).wait()
        lax.fori_loop(n_used, n_blk, tail_wait, 0)

    def issue(r, c):
        for k in range(MOE_TOP_K):
            pltpu.make_async_copy(x_ref.at[pl.ds(r, 1)], xs_ref.at[pl.ds(dest_ref[0, 0, 2 * r + k], 1)], sem).start()
        return c

    lax.fori_loop(0, tm, issue, 0, unroll=DMA_UNROLL)
    for _ in range(MOE_TOP_K * tm):
        pltpu.make_async_copy(x_ref.at[pl.ds(0, 1)], xs_ref.at[pl.ds(0, 1)], sem).wait()


def _dispatch(pad_end, dest3, x1, n_slot):
    t = x1.shape[0]
    tm = TM_DMA
    return pl.pallas_call(
        _dispatch_kernel,
        grid_spec=pltpu.PrefetchScalarGridSpec(
            num_scalar_prefetch=1, grid=(t // tm,),
            in_specs=[pl.BlockSpec((1, 1, 2 * tm), lambda i, pe: (i, 0, 0), memory_space=pltpu.SMEM),
                      pl.BlockSpec((tm, D_MODEL), lambda i, pe: (i, 0))],
            out_specs=pl.BlockSpec(memory_space=pl.ANY),
            scratch_shapes=[pltpu.VMEM((MOE_BLOCK, D_MODEL), F32), pltpu.SemaphoreType.DMA(()),
                            pltpu.SemaphoreType.DMA(())]),
        out_shape=jax.ShapeDtypeStruct((n_slot, D_MODEL), F32),
        compiler_params=_cparams("arbitrary"),
        name="moe_dispatch",
    )(pad_end, dest3, x1)


def _expert_kernel(eid_ref, nused_ref, xs_ref, wg_ref, wu_ref, wd_ref, ys_ref, wgu_bf, wd_bf):
    i = pl.program_id(0)

    @pl.when((i == 0) | (eid_ref[i] != eid_ref[jnp.maximum(i - 1, 0)]))
    def _():
        wgu_bf[:, 0:MOE_HIDDEN] = wg_ref[0, 0].astype(BF16)
        wgu_bf[:, MOE_HIDDEN:2 * MOE_HIDDEN] = wu_ref[0, 0].astype(BF16)
        wd_bf[...] = wd_ref[0, 0].astype(BF16)

    @pl.when(i < nused_ref[0])
    def _():
        xb = xs_ref[...].astype(BF16)
        gu = _dot(xb, wgu_bf[...])
        gate, up = gu[:, 0:MOE_HIDDEN], gu[:, MOE_HIDDEN:2 * MOE_HIDDEN]
        hidden = gate * (1.0 / (1.0 + jnp.exp(-gate))) * up
        ys_ref[...] = _dot(hidden.astype(BF16), wd_bf[...])

    @pl.when(i >= nused_ref[0])
    def _():
        ys_ref[...] = jnp.zeros_like(ys_ref)


def _expert_mlp(blk_eid, n_used, xs, w_gate, w_up, w_down, layer):
    n_slot = xs.shape[0]
    n_blk = n_slot // MOE_BLOCK
    wspec = lambda k, n: pl.BlockSpec((1, 1, k, n), lambda i, e, u: (layer, e[i], 0, 0))
    return pl.pallas_call(
        _expert_kernel,
        grid_spec=pltpu.PrefetchScalarGridSpec(
            num_scalar_prefetch=2, grid=(n_blk,),
            in_specs=[pl.BlockSpec((MOE_BLOCK, D_MODEL), lambda i, e, u: (i, 0)),
                      wspec(D_MODEL, MOE_HIDDEN), wspec(D_MODEL, MOE_HIDDEN), wspec(MOE_HIDDEN, D_MODEL)],
            out_specs=pl.BlockSpec((MOE_BLOCK, D_MODEL), lambda i, e, u: (i, 0)),
            scratch_shapes=[pltpu.VMEM((D_MODEL, 2 * MOE_HIDDEN), BF16), pltpu.VMEM((MOE_HIDDEN, D_MODEL), BF16)]),
        out_shape=jax.ShapeDtypeStruct((n_slot, D_MODEL), F32),
        compiler_params=_cparams("arbitrary"),
        name="expert_mlp",
    )(blk_eid, n_used, xs, w_gate, w_up, w_down)


def _combine_kernel(dest0_ref, dest_next_ref, rg_ref, x_ref, g_ref, b_ref, ys_ref, o_ref, buf, sems):
    tm = x_ref.shape[0]
    i = pl.program_id(0)
    par = i % 2

    def gather_block(dest_ref, slot):
        def issue(r, c):
            for k in range(MOE_TOP_K):
                pltpu.make_async_copy(ys_ref.at[pl.ds(dest_ref[0, 0, 2 * r + k], 1)], buf.at[slot, k, pl.ds(r, 1)],
                                      sems.at[slot]).start()
            return c

        lax.fori_loop(0, tm, issue, 0, unroll=DMA_UNROLL)

    @pl.when(i == 0)
    def _():
        gather_block(dest0_ref, 0)

    @pl.when(i + 1 < pl.num_programs(0))
    def _():
        gather_block(dest_next_ref, 1 - par)

    for _ in range(MOE_TOP_K * tm):
        pltpu.make_async_copy(ys_ref.at[pl.ds(0, 1)], buf.at[par, 0, pl.ds(0, 1)], sems.at[par]).wait()
    rg = rg_ref[...]
    ffn = buf[par, 0] * rg[:, 0:1] + buf[par, 1] * rg[:, 1:2]
    o_ref[...] = _layer_norm(DEEPNORM_ALPHA * x_ref[...] + ffn, g_ref[...], b_ref[...])


def _combine_ln(dest3, rg, x1, g, b, ys):
    t = x1.shape[0]
    tm = TM_DMA
    row = lambda w: pl.BlockSpec((tm, w), lambda i: (i, 0))
    full = lambda a: pl.BlockSpec(a.shape, lambda i: (0,) * a.ndim)
    n = t // tm
    idx = lambda f: pl.BlockSpec((1, 1, 2 * tm), lambda i: (f(i), 0, 0), memory_space=pltpu.SMEM)
    return pl.pallas_call(
        _combine_kernel,
        grid=(n,),
        in_specs=[idx(lambda i: 0), idx(lambda i: jnp.minimum(i + 1, n - 1)),
                  row(LANES), row(D_MODEL), full(g), full(b), pl.BlockSpec(memory_space=pl.ANY)],
        out_specs=row(D_MODEL),
        out_shape=jax.ShapeDtypeStruct((t, D_MODEL), F32),
        scratch_shapes=[pltpu.VMEM((2, MOE_TOP_K, tm, D_MODEL), F32), pltpu.SemaphoreType.DMA((2,))],
        compiler_params=_cparams("arbitrary"),
        name="moe_combine",
    )(dest3, dest3, rg, x1, g, b, ys)


def _moe(x1, ri, rg, cnt, w_gate, w_up, w_down, layer, g, b):
    t = x1.shape[0]
    n_slot = t * MOE_TOP_K + MOE_EXPERTS * MOE_BLOCK
    n_blk = n_slot // MOE_BLOCK
    counts = cnt[0, ROUTE_FINE0:ROUTE_FINE1].astype(jnp.int32)
    padded = (counts + MOE_BLOCK - 1) // MOE_BLOCK * MOE_BLOCK
    pad_end = jnp.cumsum(padded).astype(jnp.int32)
    pad_start = pad_end - padded
    onehot = ri[:, 0:2, None] == jnp.arange(MOE_EXPERTS, dtype=jnp.int32)
    dest = jnp.sum(jnp.where(onehot, pad_start, 0), axis=-1) + ri[:, 2:4]
    dest3 = dest.reshape(t // TM_DMA, 1, 2 * TM_DMA)
    blk_start = jnp.arange(n_blk, dtype=jnp.int32) * MOE_BLOCK
    blk_eid = jnp.minimum(jnp.sum((pad_end[None, :] <= blk_start[:, None]).astype(jnp.int32), axis=1),
                          MOE_EXPERTS - 1)
    n_used = pad_end[MOE_EXPERTS - 1:] // MOE_BLOCK
    xs = _dispatch(pad_end, dest3, x1, n_slot)
    ys = _expert_mlp(blk_eid, n_used, xs, w_gate, w_up, w_down, layer)
    return _combine_ln(dest3, rg, x1, g, b, ys)


def _rotate_half_cols(w):
    half = MLA_ROPE_DIM // 2
    return jnp.concatenate([-w[..., half:], w[..., :half]], axis=-1)


def _prep_ab(w_in, w_uq, w_ukv):
    wa = w_in[:, 0:3 * DA_WIDTH]
    c0 = 3 * DA_WIDTH
    w_cq = w_in[:, c0:c0 + MLA_Q_RANK]
    w_ckv = w_in[:, c0 + MLA_Q_RANK:c0 + MLA_Q_RANK + MLA_KV_RANK]
    w_kr = w_in[:, c0 + MLA_Q_RANK + MLA_KV_RANK:]
    z64 = jnp.zeros((D_MODEL, 64), F32)
    z32 = jnp.zeros((D_MODEL, 32), F32)
    wc = jnp.concatenate([w_cq, w_ckv, z64, w_kr, z32, z64, _rotate_half_cols(w_kr), z32], axis=1)
    qd = MLA_NOPE_DIM + MLA_ROPE_DIM
    wq = w_uq.reshape(MLA_Q_RANK, MLA_HEADS, qd)
    zq = jnp.zeros((MLA_Q_RANK, MLA_HEADS, LANES - qd), F32)
    plain = jnp.concatenate([wq, zq], axis=-1)
    rot = jnp.concatenate([jnp.zeros_like(wq[..., :MLA_NOPE_DIM]), _rotate_half_cols(wq[..., MLA_NOPE_DIM:]), zq],
                          axis=-1)
    wuq = jnp.concatenate([plain.reshape(MLA_Q_RANK, -1), rot.reshape(MLA_Q_RANK, -1)], axis=1)
    wkv = w_ukv.reshape(MLA_KV_RANK, MLA_HEADS, MLA_NOPE_DIM + MLA_V_DIM)
    k_pad = jnp.concatenate([wkv[..., :MLA_NOPE_DIM], jnp.zeros((MLA_KV_RANK, MLA_HEADS, 64), F32)], axis=-1)
    wukv = jnp.concatenate([k_pad.reshape(MLA_KV_RANK, -1), wkv[..., MLA_NOPE_DIM:].reshape(MLA_KV_RANK, -1)],
                           axis=1)
    return wa.astype(BF16), wc.astype(BF16), wuq.astype(BF16), wukv.astype(BF16)


def _rope_tables(seq):
    half = MLA_ROPE_DIM // 2
    inv_freq = jnp.power(ROPE_THETA, -jnp.arange(half, dtype=F32) * 2.0 / MLA_ROPE_DIM)
    ang = jnp.arange(seq, dtype=F32)[:, None] * inv_freq[None, :]
    cos = jnp.concatenate([jnp.cos(ang)] * 2, axis=-1)
    sin = jnp.concatenate([jnp.sin(ang)] * 2, axis=-1)
    scale = (MLA_NOPE_DIM + MLA_ROPE_DIM) ** -0.5 * LOG2E
    ones = jnp.ones((seq, 64), F32)
    z64 = jnp.zeros((seq, 64), F32)
    z32 = jnp.zeros((seq, 32), F32)
    cq = jnp.concatenate([ones, cos, z32], axis=-1) * scale
    sq = jnp.concatenate([z64, sin, z32], axis=-1) * scale
    ck = jnp.concatenate([z64, cos, z32], axis=-1)
    sk = jnp.concatenate([z64, sin, z32], axis=-1)
    return cq, sq, ck, sk


def _alibi_slopes(n_heads):
    start = 2.0 ** (-8.0 / n_heads)
    return jnp.asarray([start ** (i + 1) for i in range(n_heads)], dtype=F32)


def _band_table(dilation):
    win = BAND_SUB + 2 * BAND_HALO
    kk = jnp.arange(win, dtype=jnp.int32)[None, :]
    rel = kk - BAND_HALO - jnp.arange(BAND_SUB, dtype=jnp.int32)[:, None]
    dist = (jnp.abs(rel) * dilation).astype(F32)
    bias = -(_alibi_slopes(DIL_HEADS) * LOG2E)[:, None, None] * dist[None]
    interior = jnp.where((jnp.abs(rel) <= BAND_HALO)[None], bias, NEG_INF)
    first = jnp.where((kk >= BAND_HALO)[None], interior, NEG_INF)
    last = jnp.where((kk < BAND_SUB + BAND_HALO)[None], interior, NEG_INF)
    return jnp.stack([interior, first, last])


def _prep_router(w_group, b_group, w_route, b_route):
    pad = LANES - MOE_GROUPS - MOE_EXPERTS
    w = jnp.concatenate([w_group, w_route, jnp.zeros((D_MODEL, pad), F32)], axis=1)
    hi = w.astype(BF16)
    mid = (w - hi.astype(F32)).astype(BF16)
    bias = jnp.concatenate([b_group, b_route, jnp.zeros((pad,), F32)])[None, :]
    return hi, mid, bias


def kernel(x, w_in_ab, lam_q1, lam_k1, lam_q2, lam_k2, diff_norm_g, mla_q_norm_g, w_uq, mla_kv_norm_g, w_ukv,
           w_out_ab, w_in_c, w_out_c, ln_mix_g, ln_mix_b, moe_w_group, moe_b_group, moe_w_route, moe_b_route,
           moe_w_gate, moe_w_up, moe_w_down, ln_ffn_g, ln_ffn_b):
    batch, seq, d = x.shape
    t = batch * seq
    xs = x.reshape(t, d)

    wa, wc, wuq, wukv = _prep_ab(w_in_ab[0], w_uq[0], w_ukv[0])
    cq, sq, ck, sk = _rope_tables(seq)
    hcat = _proj_ab(xs, wa, wc, mla_q_norm_g[0][None, :], wuq, mla_kv_norm_g[0][None, :], wukv, cq, sq, ck, sk, seq)
    lam_init = 0.8 - 0.6 * math.exp(-0.3 * 0)
    lam = (jnp.exp(jnp.sum(lam_q1[0] * lam_k1[0])) - jnp.exp(jnp.sum(lam_q2[0] * lam_k2[0])) + lam_init)
    o_diff, o_mla = _attention_ab(hcat, lam[None].astype(F32), _alibi_slopes(DA_HEADS) * LOG2E,
                                  diff_norm_g[0][None, :], batch, seq, 1.0 - lam_init)
    wrh, wrm, br = _prep_router(moe_w_group[0], moe_b_group[0], moe_w_route[0], moe_b_route[0])
    row = lambda w: pl.BlockSpec((TM_ROW, w), lambda i: (i, 0))
    x1, ri, rg, cnt = _outproj_ln_route(_outproj_ab_kernel, (o_diff, o_mla), (row(512), row(512)),
                                        w_out_ab[0].astype(BF16), xs, ln_mix_g[0][None, :], ln_mix_b[0][None, :],
                                        wrh, wrm, br)
    xs = _moe(x1, ri, rg, cnt, moe_w_gate, moe_w_up, moe_w_down, 0, ln_ffn_g[0][None, :], ln_ffn_b[0][None, :])

    dil_w = DIL_HEADS * DIL_HEAD_DIM
    hqs = _proj_c(xs, w_in_c[0].astype(BF16), batch, seq)
    nb = seq // TM_ROW
    outs, lses, o_specs, l_specs = [], [], [], []
    for (window, dil), hq in zip(DIL_PATTERNS, hqs):
        assert window // (2 * dil) == BAND_HALO
        sub = seq // dil
        o_p, lse_p = _band_attention(hq, _band_table(dil), batch * dil, sub)
        outs.append(o_p.reshape(batch, dil, sub, dil_w))
        lses.append(lse_p.reshape(batch, dil, sub, LANES))
        o_specs.append(pl.BlockSpec((1, dil, TM_ROW // dil, dil_w), lambda i: (i // nb, 0, i % nb, 0)))
        l_specs.append(pl.BlockSpec((1, dil, TM_ROW // dil, LANES), lambda i: (i // nb, 0, i % nb, 0)))
    wrh, wrm, br = _prep_router(moe_w_group[1], moe_b_group[1], moe_w_route[1], moe_b_route[1])
    n_pat = len(DIL_PATTERNS)
    x1, ri, rg, cnt = _outproj_ln_route(_outproj_c_kernel, (*outs, *lses), (*o_specs, *l_specs),
                                        w_out_c[0].astype(BF16), xs, ln_mix_g[1][None, :], ln_mix_b[1][None, :],
                                        wrh, wrm, br,
                                        scratch=(pltpu.VMEM((n_pat, DIL_HEADS // 2, TM_ROW, LANES), F32),
                                                 pltpu.VMEM((n_pat, TM_ROW, LANES), F32)))
    xs = _moe(x1, ri, rg, cnt, moe_w_gate, moe_w_up, moe_w_down, 1, ln_ffn_g[1][None, :], ln_ffn_b[1][None, :])
    return xs.reshape(batch, seq, d)
```

```python
import functools
import math

import jax
import jax.numpy as jnp
from jax import lax
from jax.experimental import pallas as pl
from jax.experimental.pallas import tpu as pltpu

D_MODEL = 1024
DEPTH = 2
DA_HEADS = 4
DA_HEAD_DIM = 64
DA_WIDTH = DA_HEADS * 2 * DA_HEAD_DIM
MLA_HEADS = 4
MLA_Q_RANK = 256
MLA_KV_RANK = 128
MLA_NOPE_DIM = 64
MLA_ROPE_DIM = 32
MLA_V_DIM = 128
ROPE_THETA = 10000.0
DIL_HEADS = 16
DIL_HEAD_DIM = 64
DIL_PATTERNS = ((128, 1), (512, 4), (2048, 16))
MOE_GROUPS = 4
MOE_EXPERTS_PER_GROUP = 8
MOE_EXPERTS = MOE_GROUPS * MOE_EXPERTS_PER_GROUP
MOE_TOP_K = 2
MOE_HIDDEN = 512
MOE_BLOCK = 512
DEEPNORM_ALPHA = (2 * DEPTH) ** 0.25
NORM_EPS = 1e-5

LANES = 128
BF16 = jnp.bfloat16
F32 = jnp.float32
NEG_INF = float("-inf")
LOG2E = math.log2(math.e)

TM_PROJ = 512
TM_ROW = 512
TM_DMA = 256
TQ_ATTN = 256
MLA_ROW_GROUPS = 2
MLA_GROUP_ROWS = 256
TQ_BAND = 256
BAND_HALO = 64
BAND_SUB = 128
BAND_INTERIOR, BAND_FIRST, BAND_LAST = 0, 1, 2
VMEM_LIMIT = 56 * 1024 * 1024


def _cparams(*sem):
    return pltpu.CompilerParams(dimension_semantics=sem, vmem_limit_bytes=VMEM_LIMIT)


def _dot(a, b):
    return jnp.dot(a, b, preferred_element_type=F32)


def _dot_nt(a, b):
    return lax.dot_general(a, b, (((1,), (1,)), ((), ())), preferred_element_type=F32)


def _rms(x, g):
    return x * lax.rsqrt(jnp.mean(x * x, axis=-1, keepdims=True) + NORM_EPS) * g


def _layer_norm(y, g, b):
    mu = jnp.mean(y, axis=-1, keepdims=True)
    yc = y - mu
    var = jnp.mean(yc * yc, axis=-1, keepdims=True)
    return yc * lax.rsqrt(var + NORM_EPS) * g + b


def _proj_ab_kernel(x_ref, wa_ref, wc_ref, gq_ref, wuq_ref, gkv_ref, wukv_ref, cq_ref, sq_ref, ck_ref, sk_ref,
                    o_ref):
    xb = x_ref[...].astype(BF16)
    ha = _dot(xb, wa_ref[...])
    o_ref[:, 0:DA_WIDTH] = (ha[:, 0:DA_WIDTH] * (DA_HEAD_DIM ** -0.5 * LOG2E)).astype(BF16)
    o_ref[:, DA_WIDTH:3 * DA_WIDTH] = ha[:, DA_WIDTH:3 * DA_WIDTH].astype(BF16)
    hc = _dot(xb, wc_ref[...])
    c_q = hc[:, 0:MLA_Q_RANK]
    c_kv = hc[:, MLA_Q_RANK:MLA_Q_RANK + MLA_KV_RANK]
    kr = hc[:, 384:512]
    kr_rot = hc[:, 512:640]
    q2 = _dot(_rms(c_q, gq_ref[...]).astype(BF16), wuq_ref[...])
    kv2 = _dot(_rms(c_kv, gkv_ref[...]).astype(BF16), wukv_ref[...])
    cq, sq, ck, sk = cq_ref[...], sq_ref[...], ck_ref[...], sk_ref[...]
    k_rope = kr * ck + kr_rot * sk
    base = 3 * DA_WIDTH
    for h in range(MLA_HEADS):
        lo, hi = h * LANES, (h + 1) * LANES
        qh = q2[:, lo:hi] * cq + q2[:, 512 + lo:512 + hi] * sq
        o_ref[:, base + lo:base + hi] = qh.astype(BF16)
        o_ref[:, base + 512 + lo:base + 512 + hi] = (kv2[:, lo:hi] + k_rope).astype(BF16)
    o_ref[:, base + 1024:base + 1536] = kv2[:, 512:1024].astype(BF16)


def _proj_ab(x2d, wa, wc, gq, wuq, gkv, wukv, cq, sq, ck, sk, seq):
    t = x2d.shape[0]
    tm = TM_PROJ
    nsb = seq // tm
    full = lambda a: pl.BlockSpec(a.shape, lambda i: (0,) * a.ndim)
    tab = pl.BlockSpec((tm, LANES), lambda i: (i % nsb, 0))
    return pl.pallas_call(
        _proj_ab_kernel,
        grid=(t // tm,),
        in_specs=[pl.BlockSpec((tm, D_MODEL), lambda i: (i, 0)), full(wa), full(wc), full(gq), full(wuq),
                  full(gkv), full(wukv), tab, tab, tab, tab],
        out_specs=pl.BlockSpec((tm, 3072), lambda i: (i, 0)),
        out_shape=jax.ShapeDtypeStruct((t, 3072), BF16),
        compiler_params=_cparams("parallel"),
        name="proj_ab",
    )(x2d, wa, wc, gq, wuq, gkv, wukv, cq, sq, ck, sk)


def _diff_attn_kernel(lam_ref, q_ref, k_ref, v_ref, g_ref, u_ref, o_ref, *, tq, seq, out_scale):
    i = pl.program_id(2)
    lam = lam_ref[0]
    q = q_ref[...]
    lane = lax.broadcasted_iota(jnp.int32, q.shape, 1)
    zero = jnp.zeros_like(q)
    q1 = jnp.where(lane < DA_HEAD_DIM, q, zero)
    q2 = jnp.where(lane >= DA_HEAD_DIM, q, zero)
    k = k_ref[...]
    bias = u_ref[0, :, pl.ds(pl.multiple_of(seq - tq - i * tq, tq), seq)]

    v = v_ref[...]

    def softmax_pv(qm):
        s = _dot_nt(qm, k) + bias
        p = jnp.exp2(s - jnp.max(s, axis=-1, keepdims=True))
        l = jnp.sum(p, axis=-1, keepdims=True)
        return _dot(p.astype(BF16), v) * (1.0 / l)

    o = softmax_pv(q1) - lam * softmax_pv(q2)
    o_ref[...] = (_rms(o, g_ref[...]) * out_scale).astype(BF16)


def _mla_attn_kernel(q_ref, k_ref, v_ref, o_ref, *, tq):
    k = k_ref[...]
    v = v_ref[...]
    for r in range(0, q_ref.shape[0], tq):
        s = _dot_nt(q_ref[r:r + tq, :], k)
        p = jnp.exp2(s - jnp.max(s, axis=-1, keepdims=True))
        l = jnp.sum(p, axis=-1, keepdims=True)
        o_ref[r:r + tq, :] = (_dot(p.astype(BF16), v) * (1.0 / l)).astype(BF16)


def _alibi_table(slopes, tq, seq):
    r = jnp.arange(tq, dtype=jnp.int32)[:, None]
    m = jnp.arange(2 * seq, dtype=jnp.int32)[None, :]
    dist = jnp.abs(r + (seq - tq) - m).astype(F32)
    return -slopes[:, None, None] * dist[None]


def _attention_ab(hcat, lam, slopes, diff_g, batch, seq, out_scale):
    t = hcat.shape[0]
    tq = TQ_ATTN
    nq = seq // tq
    table = _alibi_table(slopes, tq, seq)

    o_diff = pl.pallas_call(
        functools.partial(_diff_attn_kernel, tq=tq, seq=seq, out_scale=out_scale),
        grid_spec=pltpu.PrefetchScalarGridSpec(
            num_scalar_prefetch=1, grid=(DA_HEADS, batch, nq),
            in_specs=[pl.BlockSpec((tq, LANES), lambda h, b, i, *_: (b * nq + i, h)),
                      pl.BlockSpec((seq, LANES), lambda h, b, i, *_: (b, 4 + h)),
                      pl.BlockSpec((seq, LANES), lambda h, b, i, *_: (b, 8 + h)),
                      pl.BlockSpec((1, LANES), lambda h, b, i, *_: (0, 0)),
                      pl.BlockSpec((1, tq, 2 * seq), lambda h, b, i, *_: (h, 0, 0), pipeline_mode=pl.Buffered(1))],
            out_specs=pl.BlockSpec((tq, LANES), lambda h, b, i, *_: (b * nq + i, h))),
        out_shape=jax.ShapeDtypeStruct((t, DA_WIDTH), BF16),
        compiler_params=_cparams("parallel", "parallel", "parallel"),
        name="diff_attn",
    )(lam, hcat, hcat, hcat, diff_g, table)
    rows = MLA_ROW_GROUPS * MLA_GROUP_ROWS
    nr = seq // rows
    o_mla = pl.pallas_call(
        functools.partial(_mla_attn_kernel, tq=MLA_GROUP_ROWS),
        grid=(batch, MLA_HEADS, nr),
        in_specs=[pl.BlockSpec((rows, LANES), lambda b, h, i: (b * nr + i, 12 + h)),
                  pl.BlockSpec((seq, LANES), lambda b, h, i: (b, 16 + h)),
                  pl.BlockSpec((seq, LANES), lambda b, h, i: (b, 20 + h))],
        out_specs=pl.BlockSpec((rows, LANES), lambda b, h, i: (b * nr + i, h)),
        out_shape=jax.ShapeDtypeStruct((t, MLA_HEADS * MLA_V_DIM), BF16),
        compiler_params=_cparams("parallel", "parallel", "parallel"),
        name="mla_attn",
    )(hcat, hcat, hcat)
    return o_diff, o_mla


def _proj_c_kernel(x_ref, w_ref, *refs):
    (o1_ref, o4_ref, o16_ref), (h_scr, h4_scr) = refs[:3], refs[3:]
    assert [d for _, d in DIL_PATTERNS] == [1, 4, 16]
    tm = x_ref.shape[0]
    xb = x_ref[...].astype(BF16)
    for n in range(3):
        cols = slice(n * 1024, (n + 1) * 1024)
        h = _dot(xb, w_ref[:, cols])
        if n == 0:
            h = h * (DIL_HEAD_DIM ** -0.5 * LOG2E)
        o1_ref[0, 0, :, cols] = h.astype(BF16)
        for c in range(1024 // LANES):
            lanes = slice(cols.start + c * LANES, cols.start + (c + 1) * LANES)
            h_scr[...] = h[:, c * LANES:(c + 1) * LANES]
            for r4 in range(4):
                piece = h_scr[pl.ds(r4, tm // 4, stride=4), :]
                o4_ref[0, r4, :, lanes] = piece.astype(BF16)
                h4_scr[r4] = piece
            for r in range(16):
                o16_ref[0, r, :, lanes] = h4_scr[r % 4, pl.ds(r // 4, tm // 16, stride=4), :].astype(BF16)


def _proj_c(x2d, w, batch, seq):
    tm = TM_PROJ
    nb = seq // tm
    outs = pl.pallas_call(
        _proj_c_kernel,
        grid=(batch, nb),
        in_specs=[pl.BlockSpec((tm, D_MODEL), lambda b, i: (b * nb + i, 0)), pl.BlockSpec(w.shape, lambda b, i: (0, 0))],
        out_specs=[pl.BlockSpec((1, dil, tm // dil, 3072), lambda b, i: (b, 0, i, 0)) for _, dil in DIL_PATTERNS],
        out_shape=[jax.ShapeDtypeStruct((batch, dil, seq // dil, 3072), BF16) for _, dil in DIL_PATTERNS],
        scratch_shapes=[pltpu.VMEM((tm, LANES), F32), pltpu.VMEM((4, tm // 4, LANES), F32)],
        compiler_params=_cparams("parallel", "parallel"),
        name="proj_c",
    )(x2d, w)
    return [o.reshape(batch * seq, 3072) for o in outs]


def _band_attn_kernel(q_ref, kp_ref, kc_ref, kn_ref, vp_ref, vc_ref, vn_ref, band_ref, o_ref, lse_ref,
                      *, tq, sub_len):
    i = pl.program_id(1)
    n_sub = tq // BAND_SUB
    assert n_sub >= 2
    win = BAND_SUB + 2 * BAND_HALO
    lane = lax.broadcasted_iota(jnp.int32, (BAND_SUB, LANES), 1)
    first_half = lane < DIL_HEAD_DIM
    variant = [BAND_INTERIOR] * n_sub
    variant[0] = jnp.where(i == 0, BAND_FIRST, BAND_INTERIOR)
    variant[-1] = jnp.where(i == sub_len // tq - 1, BAND_LAST, BAND_INTERIOR)
    for pair in range(DIL_HEADS // 2):
        cs = slice(pair * LANES, (pair + 1) * LANES)
        kcat = jnp.concatenate([kp_ref[:, cs], kc_ref[:, cs], kn_ref[:, cs]], axis=0)
        vcat = jnp.concatenate([vp_ref[:, cs], vc_ref[:, cs], vn_ref[:, cs]], axis=0)
        for sb in range(n_sub):
            r0 = sb * BAND_SUB
            qp = q_ref[r0:r0 + BAND_SUB, cs]
            kw = kcat[r0:r0 + win]
            vw = vcat[r0:r0 + win]
            outs, ms, ls = [], [], []
            for half in range(2):
                hd = 2 * pair + half
                qh = jnp.where(first_half if half == 0 else jnp.logical_not(first_half), qp, jnp.zeros_like(qp))
                s = _dot_nt(qh, kw) + band_ref[variant[sb], hd]
                m = jnp.max(s, axis=-1, keepdims=True)
                p = jnp.exp2(s - m)
                ms.append(m)
                ls.append(jnp.sum(p, axis=-1, keepdims=True))
                outs.append(_dot(p.astype(BF16), vw))
            o_ref[r0:r0 + BAND_SUB, cs] = jnp.where(first_half, outs[0], outs[1]).astype(BF16)
            st = lse_ref[r0:r0 + BAND_SUB, :] if pair > 0 else jnp.zeros((BAND_SUB, LANES), F32)
            for half in range(2):
                st = jnp.where(lane == 2 * pair + half, ms[half], st)
                st = jnp.where(lane == DIL_HEADS + 2 * pair + half, ls[half], st)
            lse_ref[r0:r0 + BAND_SUB, :] = st


def _band_attention(hq, band, n_seq, sub_len):
    t = hq.shape[0]
    tq = TQ_BAND
    nq = sub_len // tq
    hb = tq // BAND_HALO

    def cur(col):
        return pl.BlockSpec((tq, 1024), lambda s, i: (s * nq + i, col))

    def prv(col):
        return pl.BlockSpec((BAND_HALO, 1024), lambda s, i: (jnp.maximum((s * nq + i) * hb - 1, s * nq * hb), col))

    def nxt(col):
        return pl.BlockSpec((BAND_HALO, 1024),
                            lambda s, i: (jnp.minimum((s * nq + i + 1) * hb, (s + 1) * nq * hb - 1), col))

    return pl.pallas_call(
        functools.partial(_band_attn_kernel, tq=tq, sub_len=sub_len),
        grid=(n_seq, nq),
        in_specs=[cur(0), prv(1), cur(1), nxt(1), prv(2), cur(2), nxt(2),
                  pl.BlockSpec(band.shape, lambda s, i: (0, 0, 0, 0), pipeline_mode=pl.Buffered(1))],
        out_specs=[pl.BlockSpec((tq, 1024), lambda s, i: (s * nq + i, 0)),
                   pl.BlockSpec((tq, LANES), lambda s, i: (s * nq + i, 0))],
        out_shape=[jax.ShapeDtypeStruct((t, 1024), BF16), jax.ShapeDtypeStruct((t, LANES), F32)],
        compiler_params=_cparams("parallel", "parallel"),
        name="band_attn",
    )(hq, hq, hq, hq, hq, hq, hq, band)


ROUTE_FINE0 = MOE_GROUPS
ROUTE_FINE1 = MOE_GROUPS + MOE_EXPERTS


def _route_and_rank(x1, wrh_ref, wrm_ref, br_ref, ri_ref, rg_ref, cnt_ref):
    tm = x1.shape[0]
    hi = x1.astype(BF16)
    mid = (x1 - hi.astype(F32)).astype(BF16)
    logits = _dot(hi, wrh_ref[...]) + (_dot(hi, wrm_ref[...]) + _dot(mid, wrh_ref[...])) + br_ref[...]
    lane = lax.broadcasted_iota(jnp.int32, (tm, LANES), 1)
    lane_f = lane.astype(F32)

    def first_lane(mask):
        return jnp.min(jnp.where(mask, lane_f, float(LANES)), axis=-1, keepdims=True).astype(jnp.int32)

    coarse = jnp.where(lane < MOE_GROUPS, logits, NEG_INF)
    cmax = jnp.max(coarse, axis=-1, keepdims=True)
    gsel = first_lane(coarse == cmax)
    p_group = 1.0 / jnp.sum(jnp.exp(coarse - cmax), axis=-1, keepdims=True)
    in_group = ((lane >= ROUTE_FINE0) & (lane < ROUTE_FINE1)
                & (jnp.right_shift(lane - ROUTE_FINE0, 3) == gsel))
    fine = jnp.where(in_group, logits, NEG_INF)
    v1 = jnp.max(fine, axis=-1, keepdims=True)
    i1 = first_lane(fine == v1)
    fine2 = jnp.where(lane == i1, NEG_INF, fine)
    v2 = jnp.max(fine2, axis=-1, keepdims=True)
    i2 = first_lane(fine2 == v2)
    e2 = jnp.exp(v2 - v1)
    den = 1.0 + e2
    g1 = p_group * (1.0 / den)
    g2 = p_group * (e2 / den)
    sel1 = lane == i1
    sel2 = lane == i2
    onehot = jnp.where(sel1 | sel2, 1.0, 0.0)
    row = lax.broadcasted_iota(jnp.int32, (tm, tm), 0)
    col = lax.broadcasted_iota(jnp.int32, (tm, tm), 1)
    ltri = jnp.where(col < row, 1.0, 0.0).astype(BF16)
    before = _dot(ltri, onehot.astype(BF16)) + cnt_ref[...]
    rank1 = jnp.sum(jnp.where(sel1, before, 0.0), axis=-1, keepdims=True).astype(jnp.int32)
    rank2 = jnp.sum(jnp.where(sel2, before, 0.0), axis=-1, keepdims=True).astype(jnp.int32)
    cnt_ref[...] = cnt_ref[...] + jnp.sum(onehot, axis=0, keepdims=True)
    zi = jnp.zeros((tm, LANES), jnp.int32)
    ri_ref[...] = jnp.where(lane == 0, i1 - ROUTE_FINE0,
                            jnp.where(lane == 1, i2 - ROUTE_FINE0,
                                      jnp.where(lane == 2, rank1, jnp.where(lane == 3, rank2, zi))))
    rg_ref[...] = jnp.where(lane == 0, g1, jnp.where(lane == 1, g2, jnp.zeros((tm, LANES), F32)))


def _outproj_ab_kernel(oa_ref, ob_ref, wo_ref, x_ref, g_ref, b_ref, wrh_ref, wrm_ref, br_ref,
                       x1_ref, ri_ref, rg_ref, cnt_ref):
    @pl.when(pl.program_id(0) == 0)
    def _():
        cnt_ref[...] = jnp.zeros_like(cnt_ref)

    mixed = _dot(oa_ref[...], wo_ref[0:512, :]) + _dot(ob_ref[...], wo_ref[512:1024, :])
    x1 = _layer_norm(DEEPNORM_ALPHA * x_ref[...] + mixed, g_ref[...], b_ref[...])
    x1_ref[...] = x1
    _route_and_rank(x1, wrh_ref, wrm_ref, br_ref, ri_ref, rg_ref, cnt_ref)


def _outproj_c_kernel(o1_ref, o2_ref, o3_ref, l1_ref, l2_ref, l3_ref, wo_ref, x_ref, g_ref, b_ref,
                      wrh_ref, wrm_ref, br_ref, x1_ref, ri_ref, rg_ref, cnt_ref, o_nat, l_nat):
    @pl.when(pl.program_id(0) == 0)
    def _():
        cnt_ref[...] = jnp.zeros_like(cnt_ref)

    tm = x_ref.shape[0]
    for p, ((_, dil), o_ref, l_ref) in enumerate(zip(DIL_PATTERNS, (o1_ref, o2_ref, o3_ref), (l1_ref, l2_ref, l3_ref))):
        for r in range(dil):
            rows = pl.ds(r, tm // dil, stride=dil)
            for c in range(DIL_HEADS // 2):
                o_nat[p, c, rows, :] = o_ref[0, r, :, c * LANES:(c + 1) * LANES].astype(F32)
            l_nat[p, rows, :] = l_ref[0, r]
    stats = [l_nat[p] for p in range(len(DIL_PATTERNS))]
    m_max = functools.reduce(jnp.maximum, stats)
    es = [jnp.exp2(st - m_max) for st in stats]
    den = sum(e * pltpu.roll(st, LANES - DIL_HEADS, 1) for e, st in zip(es, stats))
    inv = 1.0 / den
    w1, w2, w3 = [e * inv for e in es]
    first_half = lax.broadcasted_iota(jnp.int32, (tm, LANES), 1) < DIL_HEAD_DIM
    mixed = jnp.zeros((tm, D_MODEL), F32)
    for pair in range(DIL_HEADS // 2):
        cs = slice(pair * LANES, (pair + 1) * LANES)
        ha, hb = 2 * pair, 2 * pair + 1
        om = None
        for p, w in enumerate((w1, w2, w3)):
            wf = jnp.where(first_half, w[:, ha:ha + 1], w[:, hb:hb + 1])
            term = wf * o_nat[p, pair]
            om = term if om is None else om + term
        mixed = mixed + _dot(om.astype(BF16), wo_ref[cs, :])
    x1 = _layer_norm(DEEPNORM_ALPHA * x_ref[...] + mixed, g_ref[...], b_ref[...])
    x1_ref[...] = x1
    _route_and_rank(x1, wrh_ref, wrm_ref, br_ref, ri_ref, rg_ref, cnt_ref)


def _outproj_ln_route(kernel_fn, mix_inputs, mix_specs, wo, x2d, g, b, wrh, wrm, br, scratch=()):
    t = x2d.shape[0]
    tm = TM_ROW
    full = lambda a: pl.BlockSpec(a.shape, lambda i: (0,) * a.ndim)
    row = lambda w: pl.BlockSpec((tm, w), lambda i: (i, 0))
    return pl.pallas_call(
        kernel_fn,
        grid=(t // tm,),
        in_specs=list(mix_specs) + [full(wo), row(D_MODEL), full(g), full(b), full(wrh), full(wrm), full(br)],
        out_specs=[row(D_MODEL), row(LANES), row(LANES), pl.BlockSpec((1, LANES), lambda i: (0, 0))],
        out_shape=[jax.ShapeDtypeStruct((t, D_MODEL), F32), jax.ShapeDtypeStruct((t, LANES), jnp.int32),
                   jax.ShapeDtypeStruct((t, LANES), F32), jax.ShapeDtypeStruct((1, LANES), F32)],
        scratch_shapes=list(scratch),
        compiler_params=_cparams("arbitrary"),
        name=kernel_fn.__name__.strip("_"),
    )(*mix_inputs, wo, x2d, g, b, wrh, wrm, br)


DMA_UNROLL = 8


def _pack_bf16_pairs(a, b):
    lo = pltpu.bitcast(a.astype(BF16).astype(F32), jnp.uint32) >> 16
    hi = pltpu.bitcast(b.astype(BF16).astype(F32), jnp.uint32) & jnp.uint32(0xFFFF0000)
    return lo | hi


def _unpack_bf16_pairs(w):
    return pltpu.bitcast(w << 16, F32), pltpu.bitcast(w & jnp.uint32(0xFFFF0000), F32)


HALF_D = D_MODEL // 2


def _dispatch_kernel(pad_end_ref, dest_ref, x_ref, xs_ref, zbuf, xp, sem, zsem):
    tm = x_ref.shape[0]
    n_blk = xs_ref.shape[0] // MOE_BLOCK
    xp[...] = _pack_bf16_pairs(x_ref[:, 0:HALF_D], x_ref[:, HALF_D:D_MODEL])

    @pl.when(pl.program_id(0) == 0)
    def _():
        zbuf[...] = jnp.zeros_like(zbuf)
        n_used = pad_end_ref[MOE_EXPERTS - 1] // MOE_BLOCK

        def zero_copy(row0):
            return pltpu.make_async_copy(zbuf, xs_ref.at[pl.ds(pl.multiple_of(row0, MOE_BLOCK), MOE_BLOCK)], zsem)

        def has_block(e):
            return pad_end_ref[e] > (pad_end_ref[e - 1] if e > 0 else 0)

        def tail_start(b, c):
            zero_copy(b * MOE_BLOCK).start()
            return c

        def tail_wait(b, c):
            zero_copy(b * MOE_BLOCK).wait()
            return c

        for e in range(MOE_EXPERTS):
            @pl.when(has_block(e))
            def _():
                zero_copy(pad_end_ref[e] - MOE_BLOCK).start()
        lax.fori_loop(n_used, n_blk, tail_start, 0)
        for e in range(MOE_EXPERTS):
            @pl.when(has_block(e))
            def _():
                zero_copy(pad_end_ref[e] - MOE_BLOCK).wait()
        lax.fori_loop(n_used, n_blk, tail_wait, 0)

    def issue(r, c):
        for k in range(MOE_TOP_K):
            pltpu.make_async_copy(xp.at[pl.ds(r, 1)], xs_ref.at[pl.ds(dest_ref[0, 0, 2 * r + k], 1)], sem).start()
        return c

    lax.fori_loop(0, tm, issue, 0, unroll=DMA_UNROLL)
    for _ in range(MOE_TOP_K * tm):
        pltpu.make_async_copy(xp.at[pl.ds(0, 1)], xs_ref.at[pl.ds(0, 1)], sem).wait()


def _dispatch(pad_end, dest3, x1, n_slot):
    t = x1.shape[0]
    tm = TM_DMA
    return pl.pallas_call(
        _dispatch_kernel,
        grid_spec=pltpu.PrefetchScalarGridSpec(
            num_scalar_prefetch=1, grid=(t // tm,),
            in_specs=[pl.BlockSpec((1, 1, 2 * tm), lambda i, pe: (i, 0, 0), memory_space=pltpu.SMEM),
                      pl.BlockSpec((tm, D_MODEL), lambda i, pe: (i, 0))],
            out_specs=pl.BlockSpec(memory_space=pl.ANY),
            scratch_shapes=[pltpu.VMEM((MOE_BLOCK, HALF_D), jnp.uint32), pltpu.VMEM((tm, HALF_D), jnp.uint32),
                            pltpu.SemaphoreType.DMA(()), pltpu.SemaphoreType.DMA(())]),
        out_shape=jax.ShapeDtypeStruct((n_slot, HALF_D), jnp.uint32),
        compiler_params=_cparams("arbitrary"),
        name="moe_dispatch",
    )(pad_end, dest3, x1)


def _expert_kernel(eid_ref, nused_ref, xs_ref, wg_ref, wu_ref, wd_ref, ys_ref, wgu_bf, wd_bf):
    i = pl.program_id(0)

    @pl.when((i == 0) | (eid_ref[i] != eid_ref[jnp.maximum(i - 1, 0)]))
    def _():
        wgu_bf[:, 0:MOE_HIDDEN] = wg_ref[0, 0].astype(BF16)
        wgu_bf[:, MOE_HIDDEN:2 * MOE_HIDDEN] = wu_ref[0, 0].astype(BF16)
        wd_bf[...] = wd_ref[0, 0].astype(BF16)

    @pl.when(i < nused_ref[0])
    def _():
        x_lo, x_hi = _unpack_bf16_pairs(xs_ref[...])
        xb = jnp.concatenate([x_lo.astype(BF16), x_hi.astype(BF16)], axis=1)
        gu = _dot(xb, wgu_bf[...])
        gate, up = gu[:, 0:MOE_HIDDEN], gu[:, MOE_HIDDEN:2 * MOE_HIDDEN]
        hidden = gate * (1.0 / (1.0 + jnp.exp(-gate))) * up
        y = _dot(hidden.astype(BF16), wd_bf[...])
        ys_ref[...] = _pack_bf16_pairs(y[:, 0:HALF_D], y[:, HALF_D:D_MODEL])

    @pl.when(i >= nused_ref[0])
    def _():
        ys_ref[...] = jnp.zeros_like(ys_ref)


def _expert_mlp(blk_eid, n_used, xs, w_gate, w_up, w_down, layer):
    n_slot = xs.shape[0]
    n_blk = n_slot // MOE_BLOCK
    wspec = lambda k, n: pl.BlockSpec((1, 1, k, n), lambda i, e, u: (layer, e[i], 0, 0))
    return pl.pallas_call(
        _expert_kernel,
        grid_spec=pltpu.PrefetchScalarGridSpec(
            num_scalar_prefetch=2, grid=(n_blk,),
            in_specs=[pl.BlockSpec((MOE_BLOCK, HALF_D), lambda i, e, u: (i, 0)),
                      wspec(D_MODEL, MOE_HIDDEN), wspec(D_MODEL, MOE_HIDDEN), wspec(MOE_HIDDEN, D_MODEL)],
            out_specs=pl.BlockSpec((MOE_BLOCK, HALF_D), lambda i, e, u: (i, 0)),
            scratch_shapes=[pltpu.VMEM((D_MODEL, 2 * MOE_HIDDEN), BF16), pltpu.VMEM((MOE_HIDDEN, D_MODEL), BF16)]),
        out_shape=jax.ShapeDtypeStruct((n_slot, HALF_D), jnp.uint32),
        compiler_params=_cparams("arbitrary"),
        name="expert_mlp",
    )(blk_eid, n_used, xs, w_gate, w_up, w_down)


def _combine_kernel(dest0_ref, dest_next_ref, rg_ref, x_ref, g_ref, b_ref, ys_ref, o_ref, buf, sems):
    tm = x_ref.shape[0]
    i = pl.program_id(0)
    par = i % 2

    def gather_block(dest_ref, slot):
        def issue(r, c):
            for k in range(MOE_TOP_K):
                pltpu.make_async_copy(ys_ref.at[pl.ds(dest_ref[0, 0, 2 * r + k], 1)], buf.at[slot, k, pl.ds(r, 1)],
                                      sems.at[slot]).start()
            return c

        lax.fori_loop(0, tm, issue, 0, unroll=DMA_UNROLL)

    @pl.when(i == 0)
    def _():
        gather_block(dest0_ref, 0)

    @pl.when(i + 1 < pl.num_programs(0))
    def _():
        gather_block(dest_next_ref, 1 - par)

    for _ in range(MOE_TOP_K * tm):
        pltpu.make_async_copy(ys_ref.at[pl.ds(0, 1)], buf.at[par, 0, pl.ds(0, 1)], sems.at[par]).wait()
    rg = rg_ref[...]
    a_lo, a_hi = _unpack_bf16_pairs(buf[par, 0])
    b_lo, b_hi = _unpack_bf16_pairs(buf[par, 1])
    g1, g2 = rg[:, 0:1], rg[:, 1:2]
    ffn = jnp.concatenate([a_lo * g1 + b_lo * g2, a_hi * g1 + b_hi * g2], axis=1)
    o_ref[...] = _layer_norm(DEEPNORM_ALPHA * x_ref[...] + ffn, g_ref[...], b_ref[...])


def _combine_ln(dest3, rg, x1, g, b, ys):
    t = x1.shape[0]
    tm = TM_DMA
    row = lambda w: pl.BlockSpec((tm, w), lambda i: (i, 0))
    full = lambda a: pl.BlockSpec(a.shape, lambda i: (0,) * a.ndim)
    n = t // tm
    idx = lambda f: pl.BlockSpec((1, 1, 2 * tm), lambda i: (f(i), 0, 0), memory_space=pltpu.SMEM)
    return pl.pallas_call(
        _combine_kernel,
        grid=(n,),
        in_specs=[idx(lambda i: 0), idx(lambda i: jnp.minimum(i + 1, n - 1)),
                  row(LANES), row(D_MODEL), full(g), full(b), pl.BlockSpec(memory_space=pl.ANY)],
        out_specs=row(D_MODEL),
        out_shape=jax.ShapeDtypeStruct((t, D_MODEL), F32),
        scratch_shapes=[pltpu.VMEM((2, MOE_TOP_K, tm, HALF_D), jnp.uint32), pltpu.SemaphoreType.DMA((2,))],
        compiler_params=_cparams("arbitrary"),
        name="moe_combine",
    )(dest3, dest3, rg, x1, g, b, ys)


def _moe(x1, ri, rg, cnt, w_gate, w_up, w_down, layer, g, b):
    t = x1.shape[0]
    n_slot = t * MOE_TOP_K + MOE_EXPERTS * MOE_BLOCK
    n_blk = n_slot // MOE_BLOCK
    counts = cnt[0, ROUTE_FINE0:ROUTE_FINE1].astype(jnp.int32)
    padded = (counts + MOE_BLOCK - 1) // MOE_BLOCK * MOE_BLOCK
    pad_end = jnp.cumsum(padded).astype(jnp.int32)
    pad_start = pad_end - padded
    onehot = ri[:, 0:2, None] == jnp.arange(MOE_EXPERTS, dtype=jnp.int32)
    dest = jnp.sum(jnp.where(onehot, pad_start, 0), axis=-1) + ri[:, 2:4]
    dest3 = dest.reshape(t // TM_DMA, 1, 2 * TM_DMA)
    blk_start = jnp.arange(n_blk, dtype=jnp.int32) * MOE_BLOCK
    blk_eid = jnp.minimum(jnp.sum((pad_end[None, :] <= blk_start[:, None]).astype(jnp.int32), axis=1),
                          MOE_EXPERTS - 1)
    n_used = pad_end[MOE_EXPERTS - 1:] // MOE_BLOCK
    xs = _dispatch(pad_end, dest3, x1, n_slot)
    ys = _expert_mlp(blk_eid, n_used, xs, w_gate, w_up, w_down, layer)
    return _combine_ln(dest3, rg, x1, g, b, ys)


def _rotate_half_cols(w):
    half = MLA_ROPE_DIM // 2
    return jnp.concatenate([-w[..., half:], w[..., :half]], axis=-1)


def _prep_ab(w_in, w_uq, w_ukv):
    wa = w_in[:, 0:3 * DA_WIDTH]
    c0 = 3 * DA_WIDTH
    w_cq = w_in[:, c0:c0 + MLA_Q_RANK]
    w_ckv = w_in[:, c0 + MLA_Q_RANK:c0 + MLA_Q_RANK + MLA_KV_RANK]
    w_kr = w_in[:, c0 + MLA_Q_RANK + MLA_KV_RANK:]
    z64 = jnp.zeros((D_MODEL, 64), F32)
    z32 = jnp.zeros((D_MODEL, 32), F32)
    wc = jnp.concatenate([w_cq, w_ckv, z64, w_kr, z32, z64, _rotate_half_cols(w_kr), z32], axis=1)
    qd = MLA_NOPE_DIM + MLA_ROPE_DIM
    wq = w_uq.reshape(MLA_Q_RANK, MLA_HEADS, qd)
    zq = jnp.zeros((MLA_Q_RANK, MLA_HEADS, LANES - qd), F32)
    plain = jnp.concatenate([wq, zq], axis=-1)
    rot = jnp.concatenate([jnp.zeros_like(wq[..., :MLA_NOPE_DIM]), _rotate_half_cols(wq[..., MLA_NOPE_DIM:]), zq],
                          axis=-1)
    wuq = jnp.concatenate([plain.reshape(MLA_Q_RANK, -1), rot.reshape(MLA_Q_RANK, -1)], axis=1)
    wkv = w_ukv.reshape(MLA_KV_RANK, MLA_HEADS, MLA_NOPE_DIM + MLA_V_DIM)
    k_pad = jnp.concatenate([wkv[..., :MLA_NOPE_DIM], jnp.zeros((MLA_KV_RANK, MLA_HEADS, 64), F32)], axis=-1)
    wukv = jnp.concatenate([k_pad.reshape(MLA_KV_RANK, -1), wkv[..., MLA_NOPE_DIM:].reshape(MLA_KV_RANK, -1)],
                           axis=1)
    return wa.astype(BF16), wc.astype(BF16), wuq.astype(BF16), wukv.astype(BF16)


def _rope_tables(seq):
    half = MLA_ROPE_DIM // 2
    inv_freq = jnp.power(ROPE_THETA, -jnp.arange(half, dtype=F32) * 2.0 / MLA_ROPE_DIM)
    ang = jnp.arange(seq, dtype=F32)[:, None] * inv_freq[None, :]
    cos = jnp.concatenate([jnp.cos(ang)] * 2, axis=-1)
    sin = jnp.concatenate([jnp.sin(ang)] * 2, axis=-1)
    scale = (MLA_NOPE_DIM + MLA_ROPE_DIM) ** -0.5 * LOG2E
    ones = jnp.ones((seq, 64), F32)
    z64 = jnp.zeros((seq, 64), F32)
    z32 = jnp.zeros((seq, 32), F32)
    cq = jnp.concatenate([ones, cos, z32], axis=-1) * scale
    sq = jnp.concatenate([z64, sin, z32], axis=-1) * scale
    ck = jnp.concatenate([z64, cos, z32], axis=-1)
    sk = jnp.concatenate([z64, sin, z32], axis=-1)
    return cq, sq, ck, sk


def _alibi_slopes(n_heads):
    start = 2.0 ** (-8.0 / n_heads)
    return jnp.asarray([start ** (i + 1) for i in range(n_heads)], dtype=F32)


def _band_table(dilation):
    win = BAND_SUB + 2 * BAND_HALO
    kk = jnp.arange(win, dtype=jnp.int32)[None, :]
    rel = kk - BAND_HALO - jnp.arange(BAND_SUB, dtype=jnp.int32)[:, None]
    dist = (jnp.abs(rel) * dilation).astype(F32)
    bias = -(_alibi_slopes(DIL_HEADS) * LOG2E)[:, None, None] * dist[None]
    interior = jnp.where((jnp.abs(rel) <= BAND_HALO)[None], bias, NEG_INF)
    first = jnp.where((kk >= BAND_HALO)[None], interior, NEG_INF)
    last = jnp.where((kk < BAND_SUB + BAND_HALO)[None], interior, NEG_INF)
    return jnp.stack([interior, first, last])


def _prep_router(w_group, b_group, w_route, b_route):
    pad = LANES - MOE_GROUPS - MOE_EXPERTS
    w = jnp.concatenate([w_group, w_route, jnp.zeros((D_MODEL, pad), F32)], axis=1)
    hi = w.astype(BF16)
    mid = (w - hi.astype(F32)).astype(BF16)
    bias = jnp.concatenate([b_group, b_route, jnp.zeros((pad,), F32)])[None, :]
    return hi, mid, bias


def kernel(x, w_in_ab, lam_q1, lam_k1, lam_q2, lam_k2, diff_norm_g, mla_q_norm_g, w_uq, mla_kv_norm_g, w_ukv,
           w_out_ab, w_in_c, w_out_c, ln_mix_g, ln_mix_b, moe_w_group, moe_b_group, moe_w_route, moe_b_route,
           moe_w_gate, moe_w_up, moe_w_down, ln_ffn_g, ln_ffn_b):
    batch, seq, d = x.shape
    t = batch * seq
    xs = x.reshape(t, d)

    wa, wc, wuq, wukv = _prep_ab(w_in_ab[0], w_uq[0], w_ukv[0])
    cq, sq, ck, sk = _rope_tables(seq)
    hcat = _proj_ab(xs, wa, wc, mla_q_norm_g[0][None, :], wuq, mla_kv_norm_g[0][None, :], wukv, cq, sq, ck, sk, seq)
    lam_init = 0.8 - 0.6 * math.exp(-0.3 * 0)
    lam = (jnp.exp(jnp.sum(lam_q1[0] * lam_k1[0])) - jnp.exp(jnp.sum(lam_q2[0] * lam_k2[0])) + lam_init)
    o_diff, o_mla = _attention_ab(hcat, lam[None].astype(F32), _alibi_slopes(DA_HEADS) * LOG2E,
                                  diff_norm_g[0][None, :], batch, seq, 1.0 - lam_init)
    wrh, wrm, br = _prep_router(moe_w_group[0], moe_b_group[0], moe_w_route[0], moe_b_route[0])
    row = lambda w: pl.BlockSpec((TM_ROW, w), lambda i: (i, 0))
    x1, ri, rg, cnt = _outproj_ln_route(_outproj_ab_kernel, (o_diff, o_mla), (row(512), row(512)),
                                        w_out_ab[0].astype(BF16), xs, ln_mix_g[0][None, :], ln_mix_b[0][None, :],
                                        wrh, wrm, br)
    xs = _moe(x1, ri, rg, cnt, moe_w_gate, moe_w_up, moe_w_down, 0, ln_ffn_g[0][None, :], ln_ffn_b[0][None, :])

    dil_w = DIL_HEADS * DIL_HEAD_DIM
    hqs = _proj_c(xs, w_in_c[0].astype(BF16), batch, seq)
    nb = seq // TM_ROW
    outs, lses, o_specs, l_specs = [], [], [], []
    for (window, dil), hq in zip(DIL_PATTERNS, hqs):
        assert window // (2 * dil) == BAND_HALO
        sub = seq // dil
        o_p, lse_p = _band_attention(hq, _band_table(dil), batch * dil, sub)
        outs.append(o_p.reshape(batch, dil, sub, dil_w))
        lses.append(lse_p.reshape(batch, dil, sub, LANES))
        o_specs.append(pl.BlockSpec((1, dil, TM_ROW // dil, dil_w), lambda i: (i // nb, 0, i % nb, 0)))
        l_specs.append(pl.BlockSpec((1, dil, TM_ROW // dil, LANES), lambda i: (i // nb, 0, i % nb, 0)))
    wrh, wrm, br = _prep_router(moe_w_group[1], moe_b_group[1], moe_w_route[1], moe_b_route[1])
    n_pat = len(DIL_PATTERNS)
    x1, ri, rg, cnt = _outproj_ln_route(_outproj_c_kernel, (*outs, *lses), (*o_specs, *l_specs),
                                        w_out_c[0].astype(BF16), xs, ln_mix_g[1][None, :], ln_mix_b[1][None, :],
                                        wrh, wrm, br,
                                        scratch=(pltpu.VMEM((n_pat, DIL_HEADS // 2, TM_ROW, LANES), F32),
                                                 pltpu.VMEM((n_pat, TM_ROW, LANES), F32)))
    xs = _moe(x1, ri, rg, cnt, moe_w_gate, moe_w_up, moe_w_down, 1, ln_ffn_g[1][None, :], ln_ffn_b[1][None, :])
    return xs.reshape(batch, seq, d)
```

```python
import functools
import math

import jax
import jax.numpy as jnp
from jax import lax
from jax.experimental import pallas as pl
from jax.experimental.pallas import tpu as pltpu

D_MODEL = 1024
DEPTH = 2
DA_HEADS = 4
DA_HEAD_DIM = 64
DA_WIDTH = DA_HEADS * 2 * DA_HEAD_DIM
MLA_HEADS = 4
MLA_Q_RANK = 256
MLA_KV_RANK = 128
MLA_NOPE_DIM = 64
MLA_ROPE_DIM = 32
MLA_V_DIM = 128
ROPE_THETA = 10000.0
DIL_HEADS = 16
DIL_HEAD_DIM = 64
DIL_PATTERNS = ((128, 1), (512, 4), (2048, 16))
MOE_GROUPS = 4
MOE_EXPERTS_PER_GROUP = 8
MOE_EXPERTS = MOE_GROUPS * MOE_EXPERTS_PER_GROUP
MOE_TOP_K = 2
MOE_HIDDEN = 512
MOE_BLOCK = 512
DEEPNORM_ALPHA = (2 * DEPTH) ** 0.25
NORM_EPS = 1e-5

LANES = 128
BF16 = jnp.bfloat16
F32 = jnp.float32
NEG_INF = float("-inf")
LOG2E = math.log2(math.e)

TM_PROJ = 512
TM_ROW = 512
TM_DMA = 512
TQ_ATTN = 256
MLA_ROW_GROUPS = 2
MLA_GROUP_ROWS = 256
TQ_BAND = 256
BAND_HALO = 64
BAND_SUB = 128
BAND_INTERIOR, BAND_FIRST, BAND_LAST = 0, 1, 2
VMEM_LIMIT = 56 * 1024 * 1024


def _cparams(*sem):
    return pltpu.CompilerParams(dimension_semantics=sem, vmem_limit_bytes=VMEM_LIMIT)


def _dot(a, b):
    return jnp.dot(a, b, preferred_element_type=F32)


def _dot_nt(a, b):
    return lax.dot_general(a, b, (((1,), (1,)), ((), ())), preferred_element_type=F32)


def _rms(x, g):
    return x * lax.rsqrt(jnp.mean(x * x, axis=-1, keepdims=True) + NORM_EPS) * g


def _layer_norm(y, g, b):
    mu = jnp.mean(y, axis=-1, keepdims=True)
    yc = y - mu
    var = jnp.mean(yc * yc, axis=-1, keepdims=True)
    return yc * lax.rsqrt(var + NORM_EPS) * g + b


def _proj_ab_kernel(x_ref, wa_ref, wc_ref, gq_ref, wuq_ref, gkv_ref, wukv_ref, cq_ref, sq_ref, ck_ref, sk_ref,
                    o_ref):
    xb = x_ref[...].astype(BF16)
    ha = _dot(xb, wa_ref[...])
    o_ref[:, 0:DA_WIDTH] = (ha[:, 0:DA_WIDTH] * (DA_HEAD_DIM ** -0.5 * LOG2E)).astype(BF16)
    o_ref[:, DA_WIDTH:3 * DA_WIDTH] = ha[:, DA_WIDTH:3 * DA_WIDTH].astype(BF16)
    hc = _dot(xb, wc_ref[...])
    c_q = hc[:, 0:MLA_Q_RANK]
    c_kv = hc[:, MLA_Q_RANK:MLA_Q_RANK + MLA_KV_RANK]
    kr = hc[:, 384:512]
    kr_rot = hc[:, 512:640]
    q2 = _dot(_rms(c_q, gq_ref[...]).astype(BF16), wuq_ref[...])
    kv2 = _dot(_rms(c_kv, gkv_ref[...]).astype(BF16), wukv_ref[...])
    cq, sq, ck, sk = cq_ref[...], sq_ref[...], ck_ref[...], sk_ref[...]
    k_rope = kr * ck + kr_rot * sk
    base = 3 * DA_WIDTH
    for h in range(MLA_HEADS):
        lo, hi = h * LANES, (h + 1) * LANES
        qh = q2[:, lo:hi] * cq + q2[:, 512 + lo:512 + hi] * sq
        o_ref[:, base + lo:base + hi] = qh.astype(BF16)
        o_ref[:, base + 512 + lo:base + 512 + hi] = (kv2[:, lo:hi] + k_rope).astype(BF16)
    o_ref[:, base + 1024:base + 1536] = kv2[:, 512:1024].astype(BF16)


def _proj_ab(x2d, wa, wc, gq, wuq, gkv, wukv, cq, sq, ck, sk, seq):
    t = x2d.shape[0]
    tm = TM_PROJ
    nsb = seq // tm
    full = lambda a: pl.BlockSpec(a.shape, lambda i: (0,) * a.ndim)
    tab = pl.BlockSpec((tm, LANES), lambda i: (i % nsb, 0))
    return pl.pallas_call(
        _proj_ab_kernel,
        grid=(t // tm,),
        in_specs=[pl.BlockSpec((tm, D_MODEL), lambda i: (i, 0)), full(wa), full(wc), full(gq), full(wuq),
                  full(gkv), full(wukv), tab, tab, tab, tab],
        out_specs=pl.BlockSpec((tm, 3072), lambda i: (i, 0)),
        out_shape=jax.ShapeDtypeStruct((t, 3072), BF16),
        compiler_params=_cparams("parallel"),
        name="proj_ab",
    )(x2d, wa, wc, gq, wuq, gkv, wukv, cq, sq, ck, sk)


def _diff_attn_kernel(lam_ref, q_ref, k_ref, v_ref, g_ref, u_ref, o_ref, *, tq, seq, out_scale):
    i = pl.program_id(2)
    lam = lam_ref[0]
    q = q_ref[...]
    lane = lax.broadcasted_iota(jnp.int32, q.shape, 1)
    zero = jnp.zeros_like(q)
    q1 = jnp.where(lane < DA_HEAD_DIM, q, zero)
    q2 = jnp.where(lane >= DA_HEAD_DIM, q, zero)
    k = k_ref[...]
    bias = u_ref[0, :, pl.ds(pl.multiple_of(seq - tq - i * tq, tq), seq)]

    v = v_ref[...]

    def softmax_pv(qm):
        s = _dot_nt(qm, k) + bias
        p = jnp.exp2(s - jnp.max(s, axis=-1, keepdims=True))
        l = jnp.sum(p, axis=-1, keepdims=True)
        return _dot(p.astype(BF16), v) * (1.0 / l)

    o = softmax_pv(q1) - lam * softmax_pv(q2)
    o_ref[...] = (_rms(o, g_ref[...]) * out_scale).astype(BF16)


def _mla_attn_kernel(q_ref, k_ref, v_ref, o_ref, *, tq):
    k = k_ref[...]
    v = v_ref[...]
    for r in range(0, q_ref.shape[0], tq):
        s = _dot_nt(q_ref[r:r + tq, :], k)
        p = jnp.exp2(s - jnp.max(s, axis=-1, keepdims=True))
        l = jnp.sum(p, axis=-1, keepdims=True)
        o_ref[r:r + tq, :] = (_dot(p.astype(BF16), v) * (1.0 / l)).astype(BF16)


def _alibi_table(slopes, tq, seq):
    r = jnp.arange(tq, dtype=jnp.int32)[:, None]
    m = jnp.arange(2 * seq, dtype=jnp.int32)[None, :]
    dist = jnp.abs(r + (seq - tq) - m).astype(F32)
    return -slopes[:, None, None] * dist[None]


def _attention_ab(hcat, lam, slopes, diff_g, batch, seq, out_scale):
    t = hcat.shape[0]
    tq = TQ_ATTN
    nq = seq // tq
    table = _alibi_table(slopes, tq, seq)

    o_diff = pl.pallas_call(
        functools.partial(_diff_attn_kernel, tq=tq, seq=seq, out_scale=out_scale),
        grid_spec=pltpu.PrefetchScalarGridSpec(
            num_scalar_prefetch=1, grid=(DA_HEADS, batch, nq),
            in_specs=[pl.BlockSpec((tq, LANES), lambda h, b, i, *_: (b * nq + i, h)),
                      pl.BlockSpec((seq, LANES), lambda h, b, i, *_: (b, 4 + h)),
                      pl.BlockSpec((seq, LANES), lambda h, b, i, *_: (b, 8 + h)),
                      pl.BlockSpec((1, LANES), lambda h, b, i, *_: (0, 0)),
                      pl.BlockSpec((1, tq, 2 * seq), lambda h, b, i, *_: (h, 0, 0), pipeline_mode=pl.Buffered(1))],
            out_specs=pl.BlockSpec((tq, LANES), lambda h, b, i, *_: (b * nq + i, h))),
        out_shape=jax.ShapeDtypeStruct((t, DA_WIDTH), BF16),
        compiler_params=_cparams("parallel", "parallel", "parallel"),
        name="diff_attn",
    )(lam, hcat, hcat, hcat, diff_g, table)
    rows = MLA_ROW_GROUPS * MLA_GROUP_ROWS
    nr = seq // rows
    o_mla = pl.pallas_call(
        functools.partial(_mla_attn_kernel, tq=MLA_GROUP_ROWS),
        grid=(batch, MLA_HEADS, nr),
        in_specs=[pl.BlockSpec((rows, LANES), lambda b, h, i: (b * nr + i, 12 + h)),
                  pl.BlockSpec((seq, LANES), lambda b, h, i: (b, 16 + h)),
                  pl.BlockSpec((seq, LANES), lambda b, h, i: (b, 20 + h))],
        out_specs=pl.BlockSpec((rows, LANES), lambda b, h, i: (b * nr + i, h)),
        out_shape=jax.ShapeDtypeStruct((t, MLA_HEADS * MLA_V_DIM), BF16),
        compiler_params=_cparams("parallel", "parallel", "parallel"),
        name="mla_attn",
    )(hcat, hcat, hcat)
    return o_diff, o_mla


def _proj_c_kernel(x_ref, w_ref, *refs):
    (o1_ref, o4_ref, o16_ref), (h_scr, h4_scr) = refs[:3], refs[3:]
    assert [d for _, d in DIL_PATTERNS] == [1, 4, 16]
    tm = x_ref.shape[0]
    xb = x_ref[...].astype(BF16)
    for n in range(3):
        cols = slice(n * 1024, (n + 1) * 1024)
        h = _dot(xb, w_ref[:, cols])
        if n == 0:
            h = h * (DIL_HEAD_DIM ** -0.5 * LOG2E)
        o1_ref[0, 0, :, cols] = h.astype(BF16)
        for c in range(1024 // LANES):
            lanes = slice(cols.start + c * LANES, cols.start + (c + 1) * LANES)
            h_scr[...] = h[:, c * LANES:(c + 1) * LANES]
            for r4 in range(4):
                piece = h_scr[pl.ds(r4, tm // 4, stride=4), :]
                o4_ref[0, r4, :, lanes] = piece.astype(BF16)
                h4_scr[r4] = piece
            for r in range(16):
                o16_ref[0, r, :, lanes] = h4_scr[r % 4, pl.ds(r // 4, tm // 16, stride=4), :].astype(BF16)


def _proj_c(x2d, w, batch, seq):
    tm = TM_PROJ
    nb = seq // tm
    outs = pl.pallas_call(
        _proj_c_kernel,
        grid=(batch, nb),
        in_specs=[pl.BlockSpec((tm, D_MODEL), lambda b, i: (b * nb + i, 0)), pl.BlockSpec(w.shape, lambda b, i: (0, 0))],
        out_specs=[pl.BlockSpec((1, dil, tm // dil, 3072), lambda b, i: (b, 0, i, 0)) for _, dil in DIL_PATTERNS],
        out_shape=[jax.ShapeDtypeStruct((batch, dil, seq // dil, 3072), BF16) for _, dil in DIL_PATTERNS],
        scratch_shapes=[pltpu.VMEM((tm, LANES), F32), pltpu.VMEM((4, tm // 4, LANES), F32)],
        compiler_params=_cparams("parallel", "parallel"),
        name="proj_c",
    )(x2d, w)
    return [o.reshape(batch * seq, 3072) for o in outs]


def _band_attn_kernel(q_ref, kp_ref, kc_ref, kn_ref, vp_ref, vc_ref, vn_ref, band_ref, o_ref, lse_ref,
                      *, tq, sub_len):
    i = pl.program_id(1)
    n_sub = tq // BAND_SUB
    assert n_sub >= 2
    win = BAND_SUB + 2 * BAND_HALO
    lane = lax.broadcasted_iota(jnp.int32, (BAND_SUB, LANES), 1)
    first_half = lane < DIL_HEAD_DIM
    variant = [BAND_INTERIOR] * n_sub
    variant[0] = jnp.where(i == 0, BAND_FIRST, BAND_INTERIOR)
    variant[-1] = jnp.where(i == sub_len // tq - 1, BAND_LAST, BAND_INTERIOR)
    for pair in range(DIL_HEADS // 2):
        cs = slice(pair * LANES, (pair + 1) * LANES)
        kcat = jnp.concatenate([kp_ref[:, cs], kc_ref[:, cs], kn_ref[:, cs]], axis=0)
        vcat = jnp.concatenate([vp_ref[:, cs], vc_ref[:, cs], vn_ref[:, cs]], axis=0)
        for sb in range(n_sub):
            r0 = sb * BAND_SUB
            qp = q_ref[r0:r0 + BAND_SUB, cs]
            kw = kcat[r0:r0 + win]
            vw = vcat[r0:r0 + win]
            outs, ms, ls = [], [], []
            for half in range(2):
                hd = 2 * pair + half
                qh = jnp.where(first_half if half == 0 else jnp.logical_not(first_half), qp, jnp.zeros_like(qp))
                s = _dot_nt(qh, kw) + band_ref[variant[sb], hd]
                m = jnp.max(s, axis=-1, keepdims=True)
                p = jnp.exp2(s - m)
                ms.append(m)
                ls.append(jnp.sum(p, axis=-1, keepdims=True))
                outs.append(_dot(p.astype(BF16), vw))
            o_ref[r0:r0 + BAND_SUB, cs] = jnp.where(first_half, outs[0], outs[1]).astype(BF16)
            st = lse_ref[r0:r0 + BAND_SUB, :] if pair > 0 else jnp.zeros((BAND_SUB, LANES), F32)
            for half in range(2):
                st = jnp.where(lane == 2 * pair + half, ms[half], st)
                st = jnp.where(lane == DIL_HEADS + 2 * pair + half, ls[half], st)
            lse_ref[r0:r0 + BAND_SUB, :] = st


def _band_attention(hq, band, n_seq, sub_len):
    t = hq.shape[0]
    tq = TQ_BAND
    nq = sub_len // tq
    hb = tq // BAND_HALO

    def cur(col):
        return pl.BlockSpec((tq, 1024), lambda s, i: (s * nq + i, col))

    def prv(col):
        return pl.BlockSpec((BAND_HALO, 1024), lambda s, i: (jnp.maximum((s * nq + i) * hb - 1, s * nq * hb), col))

    def nxt(col):
        return pl.BlockSpec((BAND_HALO, 1024),
                            lambda s, i: (jnp.minimum((s * nq + i + 1) * hb, (s + 1) * nq * hb - 1), col))

    return pl.pallas_call(
        functools.partial(_band_attn_kernel, tq=tq, sub_len=sub_len),
        grid=(n_seq, nq),
        in_specs=[cur(0), prv(1), cur(1), nxt(1), prv(2), cur(2), nxt(2),
                  pl.BlockSpec(band.shape, lambda s, i: (0, 0, 0, 0), pipeline_mode=pl.Buffered(1))],
        out_specs=[pl.BlockSpec((tq, 1024), lambda s, i: (s * nq + i, 0)),
                   pl.BlockSpec((tq, LANES), lambda s, i: (s * nq + i, 0))],
        out_shape=[jax.ShapeDtypeStruct((t, 1024), BF16), jax.ShapeDtypeStruct((t, LANES), F32)],
        compiler_params=_cparams("parallel", "parallel"),
        name="band_attn",
    )(hq, hq, hq, hq, hq, hq, hq, band)


ROUTE_FINE0 = MOE_GROUPS
ROUTE_FINE1 = MOE_GROUPS + MOE_EXPERTS


def _route_and_rank(x1, wrh_ref, wrm_ref, br_ref, ri_ref, rg_ref, cnt_ref):
    tm = x1.shape[0]
    hi = x1.astype(BF16)
    mid = (x1 - hi.astype(F32)).astype(BF16)
    logits = _dot(hi, wrh_ref[...]) + (_dot(hi, wrm_ref[...]) + _dot(mid, wrh_ref[...])) + br_ref[...]
    lane = lax.broadcasted_iota(jnp.int32, (tm, LANES), 1)
    lane_f = lane.astype(F32)

    def first_lane(mask):
        return jnp.min(jnp.where(mask, lane_f, float(LANES)), axis=-1, keepdims=True).astype(jnp.int32)

    coarse = jnp.where(lane < MOE_GROUPS, logits, NEG_INF)
    cmax = jnp.max(coarse, axis=-1, keepdims=True)
    gsel = first_lane(coarse == cmax)
    p_group = 1.0 / jnp.sum(jnp.exp(coarse - cmax), axis=-1, keepdims=True)
    in_group = ((lane >= ROUTE_FINE0) & (lane < ROUTE_FINE1)
                & (jnp.right_shift(lane - ROUTE_FINE0, 3) == gsel))
    fine = jnp.where(in_group, logits, NEG_INF)
    v1 = jnp.max(fine, axis=-1, keepdims=True)
    i1 = first_lane(fine == v1)
    fine2 = jnp.where(lane == i1, NEG_INF, fine)
    v2 = jnp.max(fine2, axis=-1, keepdims=True)
    i2 = first_lane(fine2 == v2)
    e2 = jnp.exp(v2 - v1)
    den = 1.0 + e2
    g1 = p_group * (1.0 / den)
    g2 = p_group * (e2 / den)
    sel1 = lane == i1
    sel2 = lane == i2
    onehot = jnp.where(sel1 | sel2, 1.0, 0.0)
    row = lax.broadcasted_iota(jnp.int32, (tm, tm), 0)
    col = lax.broadcasted_iota(jnp.int32, (tm, tm), 1)
    ltri = jnp.where(col < row, 1.0, 0.0).astype(BF16)
    before = _dot(ltri, onehot.astype(BF16)) + cnt_ref[...]
    rank1 = jnp.sum(jnp.where(sel1, before, 0.0), axis=-1, keepdims=True).astype(jnp.int32)
    rank2 = jnp.sum(jnp.where(sel2, before, 0.0), axis=-1, keepdims=True).astype(jnp.int32)
    cnt_ref[...] = cnt_ref[...] + jnp.sum(onehot, axis=0, keepdims=True)
    zi = jnp.zeros((tm, LANES), jnp.int32)
    ri_ref[...] = jnp.where(lane == 0, i1 - ROUTE_FINE0,
                            jnp.where(lane == 1, i2 - ROUTE_FINE0,
                                      jnp.where(lane == 2, rank1, jnp.where(lane == 3, rank2, zi))))
    rg_ref[...] = jnp.where(lane == 0, g1, jnp.where(lane == 1, g2, jnp.zeros((tm, LANES), F32)))


def _outproj_ab_kernel(oa_ref, ob_ref, wo_ref, x_ref, g_ref, b_ref, wrh_ref, wrm_ref, br_ref,
                       x1_ref, ri_ref, rg_ref, cnt_ref):
    @pl.when(pl.program_id(0) == 0)
    def _():
        cnt_ref[...] = jnp.zeros_like(cnt_ref)

    mixed = _dot(oa_ref[...], wo_ref[0:512, :]) + _dot(ob_ref[...], wo_ref[512:1024, :])
    x1 = _layer_norm(DEEPNORM_ALPHA * x_ref[...] + mixed, g_ref[...], b_ref[...])
    x1_ref[...] = x1
    _route_and_rank(x1, wrh_ref, wrm_ref, br_ref, ri_ref, rg_ref, cnt_ref)


def _outproj_c_kernel(o1_ref, o2_ref, o3_ref, l1_ref, l2_ref, l3_ref, wo_ref, x_ref, g_ref, b_ref,
                      wrh_ref, wrm_ref, br_ref, x1_ref, ri_ref, rg_ref, cnt_ref, o_nat, l_nat):
    @pl.when(pl.program_id(0) == 0)
    def _():
        cnt_ref[...] = jnp.zeros_like(cnt_ref)

    tm = x_ref.shape[0]
    for p, ((_, dil), o_ref, l_ref) in enumerate(zip(DIL_PATTERNS, (o1_ref, o2_ref, o3_ref), (l1_ref, l2_ref, l3_ref))):
        for r in range(dil):
            rows = pl.ds(r, tm // dil, stride=dil)
            for c in range(DIL_HEADS // 2):
                o_nat[p, c, rows, :] = o_ref[0, r, :, c * LANES:(c + 1) * LANES].astype(F32)
            l_nat[p, rows, :] = l_ref[0, r]
    stats = [l_nat[p] for p in range(len(DIL_PATTERNS))]
    m_max = functools.reduce(jnp.maximum, stats)
    es = [jnp.exp2(st - m_max) for st in stats]
    den = sum(e * pltpu.roll(st, LANES - DIL_HEADS, 1) for e, st in zip(es, stats))
    inv = 1.0 / den
    w1, w2, w3 = [e * inv for e in es]
    first_half = lax.broadcasted_iota(jnp.int32, (tm, LANES), 1) < DIL_HEAD_DIM
    mixed = jnp.zeros((tm, D_MODEL), F32)
    for pair in range(DIL_HEADS // 2):
        cs = slice(pair * LANES, (pair + 1) * LANES)
        ha, hb = 2 * pair, 2 * pair + 1
        om = None
        for p, w in enumerate((w1, w2, w3)):
            wf = jnp.where(first_half, w[:, ha:ha + 1], w[:, hb:hb + 1])
            term = wf * o_nat[p, pair]
            om = term if om is None else om + term
        mixed = mixed + _dot(om.astype(BF16), wo_ref[cs, :])
    x1 = _layer_norm(DEEPNORM_ALPHA * x_ref[...] + mixed, g_ref[...], b_ref[...])
    x1_ref[...] = x1
    _route_and_rank(x1, wrh_ref, wrm_ref, br_ref, ri_ref, rg_ref, cnt_ref)


def _outproj_ln_route(kernel_fn, mix_inputs, mix_specs, wo, x2d, g, b, wrh, wrm, br, scratch=()):
    t = x2d.shape[0]
    tm = TM_ROW
    full = lambda a: pl.BlockSpec(a.shape, lambda i: (0,) * a.ndim)
    row = lambda w: pl.BlockSpec((tm, w), lambda i: (i, 0))
    return pl.pallas_call(
        kernel_fn,
        grid=(t // tm,),
        in_specs=list(mix_specs) + [full(wo), row(D_MODEL), full(g), full(b), full(wrh), full(wrm), full(br)],
        out_specs=[row(D_MODEL), row(LANES), row(LANES), pl.BlockSpec((1, LANES), lambda i: (0, 0))],
        out_shape=[jax.ShapeDtypeStruct((t, D_MODEL), F32), jax.ShapeDtypeStruct((t, LANES), jnp.int32),
                   jax.ShapeDtypeStruct((t, LANES), F32), jax.ShapeDtypeStruct((1, LANES), F32)],
        scratch_shapes=list(scratch),
        compiler_params=_cparams("arbitrary"),
        name=kernel_fn.__name__.strip("_"),
    )(*mix_inputs, wo, x2d, g, b, wrh, wrm, br)


DMA_UNROLL = 8


def _pack_bf16_pairs(a, b):
    lo = pltpu.bitcast(a.astype(BF16).astype(F32), jnp.uint32) >> 16
    hi = pltpu.bitcast(b.astype(BF16).astype(F32), jnp.uint32) & jnp.uint32(0xFFFF0000)
    return lo | hi


def _unpack_bf16_pairs(w):
    return pltpu.bitcast(w << 16, F32), pltpu.bitcast(w & jnp.uint32(0xFFFF0000), F32)


HALF_D = D_MODEL // 2
WAIT_GROUP = 64


def _wait_rows(row_copy, n_rows):
    def group(j, c):
        for _ in range(WAIT_GROUP):
            row_copy.wait()
        return c

    lax.fori_loop(0, n_rows // WAIT_GROUP, group, 0)


def _dispatch_kernel(pad_end_ref, dest_ref, x_ref, xs_ref, zbuf, xp, sem, zsem):
    tm = x_ref.shape[0]
    n_blk = xs_ref.shape[0] // MOE_BLOCK
    xp[...] = _pack_bf16_pairs(x_ref[:, 0:HALF_D], x_ref[:, HALF_D:D_MODEL])

    @pl.when(pl.program_id(0) == 0)
    def _():
        zbuf[...] = jnp.zeros_like(zbuf)
        n_used = pad_end_ref[MOE_EXPERTS - 1] // MOE_BLOCK

        def zero_copy(row0):
            return pltpu.make_async_copy(zbuf, xs_ref.at[pl.ds(pl.multiple_of(row0, MOE_BLOCK), MOE_BLOCK)], zsem)

        def has_block(e):
            return pad_end_ref[e] > (pad_end_ref[e - 1] if e > 0 else 0)

        def tail_start(b, c):
            zero_copy(b * MOE_BLOCK).start()
            return c

        def tail_wait(b, c):
            zero_copy(b * MOE_BLOCK).wait()
            return c

        for e in range(MOE_EXPERTS):
            @pl.when(has_block(e))
            def _():
                zero_copy(pad_end_ref[e] - MOE_BLOCK).start()
        lax.fori_loop(n_used, n_blk, tail_start, 0)
        for e in range(MOE_EXPERTS):
            @pl.when(has_block(e))
            def _():
                zero_copy(pad_end_ref[e] - MOE_BLOCK).wait()
        lax.fori_loop(n_used, n_blk, tail_wait, 0)

    def issue(r, c):
        for k in range(MOE_TOP_K):
            pltpu.make_async_copy(xp.at[pl.ds(r, 1)], xs_ref.at[pl.ds(dest_ref[0, 0, 2 * r + k], 1)], sem).start()
        return c

    lax.fori_loop(0, tm, issue, 0, unroll=DMA_UNROLL)
    _wait_rows(pltpu.make_async_copy(xp.at[pl.ds(0, 1)], xs_ref.at[pl.ds(0, 1)], sem), MOE_TOP_K * tm)


def _dispatch(pad_end, dest3, x1, n_slot):
    t = x1.shape[0]
    tm = TM_DMA
    return pl.pallas_call(
        _dispatch_kernel,
        grid_spec=pltpu.PrefetchScalarGridSpec(
            num_scalar_prefetch=1, grid=(t // tm,),
            in_specs=[pl.BlockSpec((1, 1, 2 * tm), lambda i, pe: (i, 0, 0), memory_space=pltpu.SMEM),
                      pl.BlockSpec((tm, D_MODEL), lambda i, pe: (i, 0))],
            out_specs=pl.BlockSpec(memory_space=pl.ANY),
            scratch_shapes=[pltpu.VMEM((MOE_BLOCK, HALF_D), jnp.uint32), pltpu.VMEM((tm, HALF_D), jnp.uint32),
                            pltpu.SemaphoreType.DMA(()), pltpu.SemaphoreType.DMA(())]),
        out_shape=jax.ShapeDtypeStruct((n_slot, HALF_D), jnp.uint32),
        compiler_params=_cparams("arbitrary"),
        name="moe_dispatch",
    )(pad_end, dest3, x1)


def _expert_kernel(eid_ref, nused_ref, xs_ref, wg_ref, wu_ref, wd_ref, ys_ref, wgu_bf, wd_bf):
    i = pl.program_id(0)

    @pl.when((i == 0) | (eid_ref[i] != eid_ref[jnp.maximum(i - 1, 0)]))
    def _():
        wgu_bf[:, 0:MOE_HIDDEN] = wg_ref[0, 0].astype(BF16)
        wgu_bf[:, MOE_HIDDEN:2 * MOE_HIDDEN] = wu_ref[0, 0].astype(BF16)
        wd_bf[...] = wd_ref[0, 0].astype(BF16)

    @pl.when(i < nused_ref[0])
    def _():
        x_lo, x_hi = _unpack_bf16_pairs(xs_ref[...])
        xb = jnp.concatenate([x_lo.astype(BF16), x_hi.astype(BF16)], axis=1)
        gu = _dot(xb, wgu_bf[...])
        gate, up = gu[:, 0:MOE_HIDDEN], gu[:, MOE_HIDDEN:2 * MOE_HIDDEN]
        hidden = gate * (1.0 / (1.0 + jnp.exp(-gate))) * up
        y = _dot(hidden.astype(BF16), wd_bf[...])
        ys_ref[...] = _pack_bf16_pairs(y[:, 0:HALF_D], y[:, HALF_D:D_MODEL])

    @pl.when(i >= nused_ref[0])
    def _():
        ys_ref[...] = jnp.zeros_like(ys_ref)


def _expert_mlp(blk_eid, n_used, xs, w_gate, w_up, w_down, layer):
    n_slot = xs.shape[0]
    n_blk = n_slot // MOE_BLOCK
    wspec = lambda k, n: pl.BlockSpec((1, 1, k, n), lambda i, e, u: (layer, e[i], 0, 0))
    return pl.pallas_call(
        _expert_kernel,
        grid_spec=pltpu.PrefetchScalarGridSpec(
            num_scalar_prefetch=2, grid=(n_blk,),
            in_specs=[pl.BlockSpec((MOE_BLOCK, HALF_D), lambda i, e, u: (i, 0)),
                      wspec(D_MODEL, MOE_HIDDEN), wspec(D_MODEL, MOE_HIDDEN), wspec(MOE_HIDDEN, D_MODEL)],
            out_specs=pl.BlockSpec((MOE_BLOCK, HALF_D), lambda i, e, u: (i, 0)),
            scratch_shapes=[pltpu.VMEM((D_MODEL, 2 * MOE_HIDDEN), BF16), pltpu.VMEM((MOE_HIDDEN, D_MODEL), BF16)]),
        out_shape=jax.ShapeDtypeStruct((n_slot, HALF_D), jnp.uint32),
        compiler_params=_cparams("arbitrary"),
        name="expert_mlp",
    )(blk_eid, n_used, xs, w_gate, w_up, w_down)


def _combine_kernel(dest0_ref, dest_next_ref, rg_ref, x_ref, g_ref, b_ref, ys_ref, o_ref, buf, sems):
    tm = x_ref.shape[0]
    i = pl.program_id(0)
    par = i % 2

    def gather_block(dest_ref, slot):
        def issue(r, c):
            for k in range(MOE_TOP_K):
                pltpu.make_async_copy(ys_ref.at[pl.ds(dest_ref[0, 0, 2 * r + k], 1)], buf.at[slot, k, pl.ds(r, 1)],
                                      sems.at[slot]).start()
            return c

        lax.fori_loop(0, tm, issue, 0, unroll=DMA_UNROLL)

    @pl.when(i == 0)
    def _():
        gather_block(dest0_ref, 0)

    @pl.when(i + 1 < pl.num_programs(0))
    def _():
        gather_block(dest_next_ref, 1 - par)

    _wait_rows(pltpu.make_async_copy(ys_ref.at[pl.ds(0, 1)], buf.at[par, 0, pl.ds(0, 1)], sems.at[par]),
               MOE_TOP_K * tm)
    rg = rg_ref[...]
    a_lo, a_hi = _unpack_bf16_pairs(buf[par, 0])
    b_lo, b_hi = _unpack_bf16_pairs(buf[par, 1])
    g1, g2 = rg[:, 0:1], rg[:, 1:2]
    ffn = jnp.concatenate([a_lo * g1 + b_lo * g2, a_hi * g1 + b_hi * g2], axis=1)
    o_ref[...] = _layer_norm(DEEPNORM_ALPHA * x_ref[...] + ffn, g_ref[...], b_ref[...])


def _combine_ln(dest3, rg, x1, g, b, ys):
    t = x1.shape[0]
    tm = TM_DMA
    row = lambda w: pl.BlockSpec((tm, w), lambda i: (i, 0))
    full = lambda a: pl.BlockSpec(a.shape, lambda i: (0,) * a.ndim)
    n = t // tm
    idx = lambda f: pl.BlockSpec((1, 1, 2 * tm), lambda i: (f(i), 0, 0), memory_space=pltpu.SMEM)
    return pl.pallas_call(
        _combine_kernel,
        grid=(n,),
        in_specs=[idx(lambda i: 0), idx(lambda i: jnp.minimum(i + 1, n - 1)),
                  row(LANES), row(D_MODEL), full(g), full(b), pl.BlockSpec(memory_space=pl.ANY)],
        out_specs=row(D_MODEL),
        out_shape=jax.ShapeDtypeStruct((t, D_MODEL), F32),
        scratch_shapes=[pltpu.VMEM((2, MOE_TOP_K, tm, HALF_D), jnp.uint32), pltpu.SemaphoreType.DMA((2,))],
        compiler_params=_cparams("arbitrary"),
        name="moe_combine",
    )(dest3, dest3, rg, x1, g, b, ys)


def _moe(x1, ri, rg, cnt, w_gate, w_up, w_down, layer, g, b):
    t = x1.shape[0]
    n_slot = t * MOE_TOP_K + MOE_EXPERTS * MOE_BLOCK
    n_blk = n_slot // MOE_BLOCK
    counts = cnt[0, ROUTE_FINE0:ROUTE_FINE1].astype(jnp.int32)
    padded = (counts + MOE_BLOCK - 1) // MOE_BLOCK * MOE_BLOCK
    pad_end = jnp.cumsum(padded).astype(jnp.int32)
    pad_start = pad_end - padded
    onehot = ri[:, 0:2, None] == jnp.arange(MOE_EXPERTS, dtype=jnp.int32)
    dest = jnp.sum(jnp.where(onehot, pad_start, 0), axis=-1) + ri[:, 2:4]
    dest3 = dest.reshape(t // TM_DMA, 1, 2 * TM_DMA)
    blk_start = jnp.arange(n_blk, dtype=jnp.int32) * MOE_BLOCK
    blk_eid = jnp.minimum(jnp.sum((pad_end[None, :] <= blk_start[:, None]).astype(jnp.int32), axis=1),
                          MOE_EXPERTS - 1)
    n_used = pad_end[MOE_EXPERTS - 1:] // MOE_BLOCK
    xs = _dispatch(pad_end, dest3, x1, n_slot)
    ys = _expert_mlp(blk_eid, n_used, xs, w_gate, w_up, w_down, layer)
    return _combine_ln(dest3, rg, x1, g, b, ys)


def _rotate_half_cols(w):
    half = MLA_ROPE_DIM // 2
    return jnp.concatenate([-w[..., half:], w[..., :half]], axis=-1)


def _prep_ab(w_in, w_uq, w_ukv):
    wa = w_in[:, 0:3 * DA_WIDTH]
    c0 = 3 * DA_WIDTH
    w_cq = w_in[:, c0:c0 + MLA_Q_RANK]
    w_ckv = w_in[:, c0 + MLA_Q_RANK:c0 + MLA_Q_RANK + MLA_KV_RANK]
    w_kr = w_in[:, c0 + MLA_Q_RANK + MLA_KV_RANK:]
    z64 = jnp.zeros((D_MODEL, 64), F32)
    z32 = jnp.zeros((D_MODEL, 32), F32)
    wc = jnp.concatenate([w_cq, w_ckv, z64, w_kr, z32, z64, _rotate_half_cols(w_kr), z32], axis=1)
    qd = MLA_NOPE_DIM + MLA_ROPE_DIM
    wq = w_uq.reshape(MLA_Q_RANK, MLA_HEADS, qd)
    zq = jnp.zeros((MLA_Q_RANK, MLA_HEADS, LANES - qd), F32)
    plain = jnp.concatenate([wq, zq], axis=-1)
    rot = jnp.concatenate([jnp.zeros_like(wq[..., :MLA_NOPE_DIM]), _rotate_half_cols(wq[..., MLA_NOPE_DIM:]), zq],
                          axis=-1)
    wuq = jnp.concatenate([plain.reshape(MLA_Q_RANK, -1), rot.reshape(MLA_Q_RANK, -1)], axis=1)
    wkv = w_ukv.reshape(MLA_KV_RANK, MLA_HEADS, MLA_NOPE_DIM + MLA_V_DIM)
    k_pad = jnp.concatenate([wkv[..., :MLA_NOPE_DIM], jnp.zeros((MLA_KV_RANK, MLA_HEADS, 64), F32)], axis=-1)
    wukv = jnp.concatenate([k_pad.reshape(MLA_KV_RANK, -1), wkv[..., MLA_NOPE_DIM:].reshape(MLA_KV_RANK, -1)],
                           axis=1)
    return wa.astype(BF16), wc.astype(BF16), wuq.astype(BF16), wukv.astype(BF16)


def _rope_tables(seq):
    half = MLA_ROPE_DIM // 2
    inv_freq = jnp.power(ROPE_THETA, -jnp.arange(half, dtype=F32) * 2.0 / MLA_ROPE_DIM)
    ang = jnp.arange(seq, dtype=F32)[:, None] * inv_freq[None, :]
    cos = jnp.concatenate([jnp.cos(ang)] * 2, axis=-1)
    sin = jnp.concatenate([jnp.sin(ang)] * 2, axis=-1)
    scale = (MLA_NOPE_DIM + MLA_ROPE_DIM) ** -0.5 * LOG2E
    ones = jnp.ones((seq, 64), F32)
    z64 = jnp.zeros((seq, 64), F32)
    z32 = jnp.zeros((seq, 32), F32)
    cq = jnp.concatenate([ones, cos, z32], axis=-1) * scale
    sq = jnp.concatenate([z64, sin, z32], axis=-1) * scale
    ck = jnp.concatenate([z64, cos, z32], axis=-1)
    sk = jnp.concatenate([z64, sin, z32], axis=-1)
    return cq, sq, ck, sk


def _alibi_slopes(n_heads):
    start = 2.0 ** (-8.0 / n_heads)
    return jnp.asarray([start ** (i + 1) for i in range(n_heads)], dtype=F32)


def _band_table(dilation):
    win = BAND_SUB + 2 * BAND_HALO
    kk = jnp.arange(win, dtype=jnp.int32)[None, :]
    rel = kk - BAND_HALO - jnp.arange(BAND_SUB, dtype=jnp.int32)[:, None]
    dist = (jnp.abs(rel) * dilation).astype(F32)
    bias = -(_alibi_slopes(DIL_HEADS) * LOG2E)[:, None, None] * dist[None]
    interior = jnp.where((jnp.abs(rel) <= BAND_HALO)[None], bias, NEG_INF)
    first = jnp.where((kk >= BAND_HALO)[None], interior, NEG_INF)
    last = jnp.where((kk < BAND_SUB + BAND_HALO)[None], interior, NEG_INF)
    return jnp.stack([interior, first, last])


def _prep_router(w_group, b_group, w_route, b_route):
    pad = LANES - MOE_GROUPS - MOE_EXPERTS
    w = jnp.concatenate([w_group, w_route, jnp.zeros((D_MODEL, pad), F32)], axis=1)
    hi = w.astype(BF16)
    mid = (w - hi.astype(F32)).astype(BF16)
    bias = jnp.concatenate([b_group, b_route, jnp.zeros((pad,), F32)])[None, :]
    return hi, mid, bias


def kernel(x, w_in_ab, lam_q1, lam_k1, lam_q2, lam_k2, diff_norm_g, mla_q_norm_g, w_uq, mla_kv_norm_g, w_ukv,
           w_out_ab, w_in_c, w_out_c, ln_mix_g, ln_mix_b, moe_w_group, moe_b_group, moe_w_route, moe_b_route,
           moe_w_gate, moe_w_up, moe_w_down, ln_ffn_g, ln_ffn_b):
    batch, seq, d = x.shape
    t = batch * seq
    xs = x.reshape(t, d)

    wa, wc, wuq, wukv = _prep_ab(w_in_ab[0], w_uq[0], w_ukv[0])
    cq, sq, ck, sk = _rope_tables(seq)
    hcat = _proj_ab(xs, wa, wc, mla_q_norm_g[0][None, :], wuq, mla_kv_norm_g[0][None, :], wukv, cq, sq, ck, sk, seq)
    lam_init = 0.8 - 0.6 * math.exp(-0.3 * 0)
    lam = (jnp.exp(jnp.sum(lam_q1[0] * lam_k1[0])) - jnp.exp(jnp.sum(lam_q2[0] * lam_k2[0])) + lam_init)
    o_diff, o_mla = _attention_ab(hcat, lam[None].astype(F32), _alibi_slopes(DA_HEADS) * LOG2E,
                                  diff_norm_g[0][None, :], batch, seq, 1.0 - lam_init)
    wrh, wrm, br = _prep_router(moe_w_group[0], moe_b_group[0], moe_w_route[0], moe_b_route[0])
    row = lambda w: pl.BlockSpec((TM_ROW, w), lambda i: (i, 0))
    x1, ri, rg, cnt = _outproj_ln_route(_outproj_ab_kernel, (o_diff, o_mla), (row(512), row(512)),
                                        w_out_ab[0].astype(BF16), xs, ln_mix_g[0][None, :], ln_mix_b[0][None, :],
                                        wrh, wrm, br)
    xs = _moe(x1, ri, rg, cnt, moe_w_gate, moe_w_up, moe_w_down, 0, ln_ffn_g[0][None, :], ln_ffn_b[0][None, :])

    dil_w = DIL_HEADS * DIL_HEAD_DIM
    hqs = _proj_c(xs, w_in_c[0].astype(BF16), batch, seq)
    nb = seq // TM_ROW
    outs, lses, o_specs, l_specs = [], [], [], []
    for (window, dil), hq in zip(DIL_PATTERNS, hqs):
        assert window // (2 * dil) == BAND_HALO
        sub = seq // dil
        o_p, lse_p = _band_attention(hq, _band_table(dil), batch * dil, sub)
        outs.append(o_p.reshape(batch, dil, sub, dil_w))
        lses.append(lse_p.reshape(batch, dil, sub, LANES))
        o_specs.append(pl.BlockSpec((1, dil, TM_ROW // dil, dil_w), lambda i: (i // nb, 0, i % nb, 0)))
        l_specs.append(pl.BlockSpec((1, dil, TM_ROW // dil, LANES), lambda i: (i // nb, 0, i % nb, 0)))
    wrh, wrm, br = _prep_router(moe_w_group[1], moe_b_group[1], moe_w_route[1], moe_b_route[1])
    n_pat = len(DIL_PATTERNS)
    x1, ri, rg, cnt = _outproj_ln_route(_outproj_c_kernel, (*outs, *lses), (*o_specs, *l_specs),
                                        w_out_c[0].astype(BF16), xs, ln_mix_g[1][None, :], ln_mix_b[1][None, :],
                                        wrh, wrm, br,
                                        scratch=(pltpu.VMEM((n_pat, DIL_HEADS // 2, TM_ROW, LANES), F32),
                                                 pltpu.VMEM((n_pat, TM_ROW, LANES), F32)))
    xs = _moe(x1, ri, rg, cnt, moe_w_gate, moe_w_up, moe_w_down, 1, ln_ffn_g[1][None, :], ln_ffn_b[1][None, :])
    return xs.reshape(batch, seq, d)
```
